```python
import math
import jax, jax.numpy as jnp
from jax import lax
import numpy as np


D_MODEL = 1024
BATCH = 16
SEQ = 2048
DEPTH = 1
DEC_BATCH = 128
DEC_SEQ = 1
PAST_LEN = 8192
PAGE_SIZE = 128

MIX_WIDTH = D_MODEL
MLA_HEADS = 8
MLA_HEAD_DIM = 64
ROPE_DIM = 32
KV_LORA = D_MODEL // 4
ATTN_WIDTH = MLA_HEADS * MLA_HEAD_DIM
CONV_CH = MIX_WIDTH - ATTN_WIDTH
CONV_K = 31
CONV_STATE = CONV_K - 1
MEM_TOKENS = 256
MEM_HEADS = 4
MEM_HEAD_DIM = D_MODEL // MEM_HEADS
D_FF = 4 * D_MODEL
Q_BLOCK = 128
ROPE_BASE = 10000.0
LN_EPS = 1e-5
ALPHA = (2.0 * DEPTH) ** 0.25
BETA = (8.0 * DEPTH) ** -0.25
Q_COLS = MLA_HEADS * (MLA_HEAD_DIM + ROPE_DIM)
IN_COLS = Q_COLS + KV_LORA + ROPE_DIM + 2 * CONV_CH
MLA_SCALE = (MLA_HEAD_DIM + ROPE_DIM) ** -0.5
MEM_SCALE = MEM_HEAD_DIM ** -0.5

kernel_name = 'hymba_mla_conformer_deepnorm_decoder_step'


def layer_norm(x, g, b):
    xf = x.astype(jnp.float32)
    mu = jnp.mean(xf, axis=-1, keepdims=True)
    var = jnp.mean(jnp.square(xf - mu), axis=-1, keepdims=True)
    y = (xf - mu) * lax.rsqrt(var + LN_EPS) * g.astype(jnp.float32) + b.astype(jnp.float32)
    return y.astype(x.dtype)


def rms_norm(x, g):
    xf = x.astype(jnp.float32)
    y = xf * lax.rsqrt(jnp.mean(jnp.square(xf), axis=-1, keepdims=True) + LN_EPS) * g.astype(jnp.float32)
    return y.astype(x.dtype)


def rope(x, pos):
    half = ROPE_DIM // 2
    inv_freq = jnp.exp(-math.log(ROPE_BASE) * jnp.arange(half, dtype=jnp.float32) / half)
    ang = pos.astype(jnp.float32)[:, None] * inv_freq[None, :]
    shape = (ang.shape[0],) + (1,) * (x.ndim - 3) + (half,)
    cos = jnp.cos(ang).reshape(shape).astype(x.dtype)
    sin = jnp.sin(ang).reshape(shape).astype(x.dtype)
    x1, x2 = x[..., :half], x[..., half:]
    return jnp.concatenate([x1 * cos - x2 * sin, x2 * cos + x1 * sin], axis=-1)


def mixer_inputs(x, pos, w_in, kv_norm_g):
    B, T, _ = x.shape
    z = jnp.einsum('btd,dc->btc', x, w_in)
    q = z[..., :Q_COLS].reshape(B, T, MLA_HEADS, MLA_HEAD_DIM + ROPE_DIM)
    q_nope = q[..., :MLA_HEAD_DIM]
    q_rope = rope(q[..., MLA_HEAD_DIM:], pos)
    c_kv = rms_norm(z[..., Q_COLS:Q_COLS + KV_LORA], kv_norm_g)
    k_rope = rope(z[..., Q_COLS + KV_LORA:Q_COLS + KV_LORA + ROPE_DIM], pos)
    glu = z[..., Q_COLS + KV_LORA + ROPE_DIM:]
    u = glu[..., :CONV_CH] * jax.nn.sigmoid(glu[..., CONV_CH:])
    return q_nope, q_rope, c_kv, k_rope, u


def mla_prompt(q_nope, q_rope, c_kv, k_rope, w_uk, w_uv):
    B, S, H, _ = q_nope.shape
    nb = S // Q_BLOCK
    k_nope = jnp.einsum('bsl,lhd->bshd', c_kv, w_uk)
    v = jnp.einsum('bsl,lhd->bshd', c_kv, w_uv)
    qn_b = q_nope.reshape(B, nb, Q_BLOCK, H, MLA_HEAD_DIM).transpose(1, 0, 2, 3, 4)
    qr_b = q_rope.reshape(B, nb, Q_BLOCK, H, ROPE_DIM).transpose(1, 0, 2, 3, 4)
    k_pos = jnp.arange(S, dtype=jnp.int32)

    def block(args):
        qn, qr, i = args
        s = jnp.einsum('bqhd,bkhd->bhqk', qn, k_nope) + jnp.einsum('bqhr,bkr->bhqk', qr, k_rope)
        q_pos = i * Q_BLOCK + jnp.arange(Q_BLOCK, dtype=jnp.int32)
        mask = k_pos[None, :] <= q_pos[:, None]
        s = jnp.where(mask, s.astype(jnp.float32) * MLA_SCALE, -jnp.inf)
        p = jax.nn.softmax(s, axis=-1).astype(v.dtype)
        return jnp.einsum('bhqk,bkhd->bqhd', p, v)

    o = lax.map(block, (qn_b, qr_b, jnp.arange(nb, dtype=jnp.int32)))
    return o.transpose(1, 0, 2, 3, 4).reshape(B, S, H * MLA_HEAD_DIM)


def mla_sample(q_nope, q_rope, c_new, kr_new, pool_ckv, pool_krope, page_table, w_uk, w_uv):
    Bd, T, H, _ = q_nope.shape
    q_lat = jnp.einsum('bthd,lhd->bthl', q_nope, w_uk)
    past_c = pool_ckv[page_table].reshape(Bd, -1, KV_LORA)
    past_r = pool_krope[page_table].reshape(Bd, -1, ROPE_DIM)
    P = past_c.shape[1]
    s_past = jnp.einsum('bthl,bkl->bhtk', q_lat, past_c) + jnp.einsum('bthr,bkr->bhtk', q_rope, past_r)
    s_new = jnp.einsum('bthl,bkl->bhtk', q_lat, c_new) + jnp.einsum('bthr,bkr->bhtk', q_rope, kr_new)
    causal = jnp.arange(T)[None, :] <= jnp.arange(T)[:, None]
    s_new = jnp.where(causal, s_new.astype(jnp.float32) * MLA_SCALE, -jnp.inf)
    s = jnp.concatenate([s_past.astype(jnp.float32) * MLA_SCALE, s_new], axis=-1)
    p = jax.nn.softmax(s, axis=-1).astype(c_new.dtype)
    o_lat = jnp.einsum('bhtk,bkl->bthl', p[..., :P], past_c) + jnp.einsum('bhtk,bkl->bthl', p[..., P:], c_new)
    return jnp.einsum('bthl,lhd->bthd', o_lat, w_uv).reshape(Bd, T, H * MLA_HEAD_DIM)


def depthwise_causal_conv(u_ext, conv_w):
    return lax.conv_general_dilated(u_ext, conv_w[:, None, :].astype(u_ext.dtype), window_strides=(1,),
                                    padding='VALID', dimension_numbers=('NWC', 'WIO', 'NWC'),
                                    feature_group_count=CONV_CH)


def conv_tail(y, conv_b, g, b):
    return jax.nn.silu(layer_norm(y + conv_b.astype(y.dtype), g, b))


def mem_kv(mem, w_xk, w_xv):
    B, M, _ = mem.shape
    k = jnp.einsum('bmd,de->bme', mem, w_xk).reshape(B, M, MEM_HEADS, MEM_HEAD_DIM)
    v = jnp.einsum('bmd,de->bme', mem, w_xv).reshape(B, M, MEM_HEADS, MEM_HEAD_DIM)
    return k, v


def post_sublayers(x, mix, mem_k, mem_v, ln1_g, ln1_b, w_xq, w_xo, ln2_g, ln2_b, w_up, w_down, ln3_g, ln3_b):
    x = layer_norm(ALPHA * x + mix, ln1_g, ln1_b)
    B, T, _ = x.shape
    q = jnp.einsum('btd,de->bte', x, w_xq).reshape(B, T, MEM_HEADS, MEM_HEAD_DIM)
    s = jnp.einsum('bthd,bmhd->bhtm', q, mem_k).astype(jnp.float32) * MEM_SCALE
    p = jax.nn.softmax(s, axis=-1).astype(mem_v.dtype)
    o = jnp.einsum('bhtm,bmhd->bthd', p, mem_v).reshape(B, T, D_MODEL)
    x = layer_norm(ALPHA * x + o @ w_xo, ln2_g, ln2_b)
    h = jnp.square(jax.nn.relu(x @ w_up))
    return layer_norm(ALPHA * x + h @ w_down, ln3_g, ln3_b)


def setup_inputs(seed: int = 0) -> dict:
    key = jax.random.key(seed)
    k = jax.random.split(key, 30)
    f32 = jnp.float32

    def nrm(i, shape, scale):
        return jax.random.normal(k[i], shape, f32) * scale

    n_pages = PAST_LEN // PAGE_SIZE
    n_used = DEC_BATCH * n_pages
    n_pool = n_used + n_used // 4
    page_table = jax.random.permutation(k[7], n_pool)[:n_used].reshape(DEC_BATCH, n_pages).astype(jnp.int32)
    return {
        'x_prompt': nrm(0, (BATCH, SEQ, D_MODEL), 1.0),
        'x_sample': nrm(1, (DEC_BATCH, DEC_SEQ, D_MODEL), 1.0),
        'cache_ckv': nrm(2, (DEPTH, n_pool, PAGE_SIZE, KV_LORA), 1.0),
        'cache_krope': nrm(3, (DEPTH, n_pool, PAGE_SIZE, ROPE_DIM), 1.0),
        'state_conv': nrm(4, (DEPTH, DEC_BATCH, CONV_STATE, CONV_CH), 0.5),
        'cache_mem_k': nrm(5, (DEPTH, DEC_BATCH, MEM_TOKENS, MEM_HEADS, MEM_HEAD_DIM), 1.0),
        'cache_mem_v': nrm(6, (DEPTH, DEC_BATCH, MEM_TOKENS, MEM_HEADS, MEM_HEAD_DIM), BETA),
        'page_table': page_table,
        'mem_prompt': nrm(8, (BATCH, MEM_TOKENS, D_MODEL), 1.0),
        'w_in': nrm(9, (DEPTH, D_MODEL, IN_COLS), D_MODEL ** -0.5),
        'kv_norm_g': 1.0 + nrm(10, (DEPTH, KV_LORA), 0.01),
        'w_uk': nrm(11, (DEPTH, KV_LORA, MLA_HEADS, MLA_HEAD_DIM), KV_LORA ** -0.5),
        'w_uv': nrm(12, (DEPTH, KV_LORA, MLA_HEADS, MLA_HEAD_DIM), BETA * KV_LORA ** -0.5),
        'conv_w': nrm(13, (DEPTH, CONV_K, CONV_CH), CONV_K ** -0.5),
        'conv_b': nrm(14, (DEPTH, CONV_CH), 0.01),
        'conv_ln_g': 1.0 + nrm(15, (DEPTH, CONV_CH), 0.01),
        'conv_ln_b': nrm(16, (DEPTH, CONV_CH), 0.01),
        'w_o': nrm(17, (DEPTH, MIX_WIDTH, D_MODEL), BETA * MIX_WIDTH ** -0.5),
        'ln1_g': 1.0 + nrm(18, (DEPTH, D_MODEL), 0.01),
        'ln1_b': nrm(19, (DEPTH, D_MODEL), 0.01),
        'w_xq': nrm(20, (DEPTH, D_MODEL, D_MODEL), D_MODEL ** -0.5),
        'w_xk': nrm(21, (DEPTH, D_MODEL, D_MODEL), D_MODEL ** -0.5),
        'w_xv': nrm(22, (DEPTH, D_MODEL, D_MODEL), BETA * D_MODEL ** -0.5),
        'w_xo': nrm(23, (DEPTH, D_MODEL, D_MODEL), BETA * D_MODEL ** -0.5),
        'ln2_g': 1.0 + nrm(24, (DEPTH, D_MODEL), 0.01),
        'ln2_b': nrm(25, (DEPTH, D_MODEL), 0.01),
        'w_up': nrm(26, (DEPTH, D_MODEL, D_FF), D_MODEL ** -0.5),
        'w_down': nrm(27, (DEPTH, D_FF, D_MODEL), BETA * D_FF ** -0.5),
        'ln3_g': 1.0 + nrm(28, (DEPTH, D_MODEL), 0.01),
        'ln3_b': nrm(29, (DEPTH, D_MODEL), 0.01),
    }


def reference(x_prompt, x_sample, cache_ckv, cache_krope, state_conv, cache_mem_k, cache_mem_v, page_table,
              mem_prompt, w_in, kv_norm_g, w_uk, w_uv, conv_w, conv_b, conv_ln_g, conv_ln_b, w_o, ln1_g, ln1_b,
              w_xq, w_xk, w_xv, w_xo, ln2_g, ln2_b, w_up, w_down, ln3_g, ln3_b):
    B, S, _ = x_prompt.shape
    T = x_sample.shape[1]
    past_len = page_table.shape[1] * cache_ckv.shape[2]
    pos_p = jnp.arange(S, dtype=jnp.int32)
    pos_s = past_len + jnp.arange(T, dtype=jnp.int32)

    xp, xs = x_prompt, x_sample
    ckv_p, kr_p, cs_p, mk_p, mv_p = [], [], [], [], []
    ckv_s, kr_s, cs_s = [], [], []
    for l in range(DEPTH):
        qn, qr, c_kv, k_r, u = mixer_inputs(xp, pos_p, w_in[l], kv_norm_g[l])
        attn = mla_prompt(qn, qr, c_kv, k_r, w_uk[l], w_uv[l])
        u_ext = jnp.concatenate([jnp.zeros((B, CONV_STATE, CONV_CH), u.dtype), u], axis=1)
        conv = conv_tail(depthwise_causal_conv(u_ext, conv_w[l]), conv_b[l], conv_ln_g[l], conv_ln_b[l])
        mix = jnp.concatenate([attn, conv], axis=-1) @ w_o[l]
        mk, mv = mem_kv(mem_prompt, w_xk[l], w_xv[l])
        xp = post_sublayers(xp, mix, mk, mv, ln1_g[l], ln1_b[l], w_xq[l], w_xo[l], ln2_g[l], ln2_b[l],
                            w_up[l], w_down[l], ln3_g[l], ln3_b[l])
        ckv_p.append(c_kv)
        kr_p.append(k_r)
        cs_p.append(u_ext[:, -CONV_STATE:])
        mk_p.append(mk)
        mv_p.append(mv)
        qn, qr, c_kv, k_r, u = mixer_inputs(xs, pos_s, w_in[l], kv_norm_g[l])
        attn = mla_sample(qn, qr, c_kv, k_r, cache_ckv[l], cache_krope[l], page_table, w_uk[l], w_uv[l])
        u_ext = jnp.concatenate([state_conv[l].astype(u.dtype), u], axis=1)
        conv = conv_tail(depthwise_causal_conv(u_ext, conv_w[l]), conv_b[l], conv_ln_g[l], conv_ln_b[l])
        mix = jnp.concatenate([attn, conv], axis=-1) @ w_o[l]
        xs = post_sublayers(xs, mix, cache_mem_k[l], cache_mem_v[l], ln1_g[l], ln1_b[l], w_xq[l], w_xo[l],
                            ln2_g[l], ln2_b[l], w_up[l], w_down[l], ln3_g[l], ln3_b[l])
        ckv_s.append(c_kv)
        kr_s.append(k_r)
        cs_s.append(u_ext[:, -CONV_STATE:])

    return (xp, xs, jnp.stack(ckv_p), jnp.stack(kr_p), jnp.stack(cs_p), jnp.stack(mk_p), jnp.stack(mv_p),
            jnp.stack(ckv_s), jnp.stack(kr_s), jnp.stack(cs_s))
```

```python
import functools
import math

import jax
import jax.numpy as jnp
from jax import lax
from jax.experimental import pallas as pl
from jax.experimental.pallas import tpu as pltpu

D_MODEL = 1024
MLA_HEADS = 8
MLA_HEAD_DIM = 64
ROPE_DIM = 32
KV_LORA = D_MODEL // 4
ATTN_WIDTH = MLA_HEADS * MLA_HEAD_DIM
CONV_CH = D_MODEL - ATTN_WIDTH
CONV_K = 31
CONV_STATE = CONV_K - 1
MEM_HEADS = 4
MEM_HEAD_DIM = D_MODEL // MEM_HEADS
D_FF = 4 * D_MODEL
ROPE_BASE = 10000.0
LN_EPS = 1e-5
DEPTH = 1
ALPHA = (2.0 * DEPTH) ** 0.25
Q_COLS = MLA_HEADS * (MLA_HEAD_DIM + ROPE_DIM)
MLA_SCALE = (MLA_HEAD_DIM + ROPE_DIM) ** -0.5
MEM_SCALE = MEM_HEAD_DIM ** -0.5

LANES = 128
HEAD_BLOCK = LANES
HALF_ROPE = ROPE_DIM // 2
NEG_BIG = -1e30
MIB = 1024 * 1024

F32 = jnp.float32
BF16 = jnp.bfloat16

QK_COLS = (MLA_HEADS + 1) * HEAD_BLOCK
IN_COLS_PADDED = QK_COLS + KV_LORA + 2 * CONV_CH


def _dot(a, b):
    return jnp.dot(a, b, preferred_element_type=F32)


def _dot_nt(a, b):
    return lax.dot_general(a, b, (((1,), (1,)), ((), ())), preferred_element_type=F32)


def _layer_norm(x, g, b):
    mu = jnp.mean(x, axis=-1, keepdims=True)
    xc = x - mu
    var = jnp.mean(xc * xc, axis=-1, keepdims=True)
    return xc * lax.rsqrt(var + LN_EPS) * g + b


def _resident(shape):
    nd = len(shape)
    return pl.BlockSpec(shape, lambda *_: (0,) * nd, pipeline_mode=pl.Buffered(1))


def _rope_block(zb, cos_t, sin_a, sin_b):
    return zb * cos_t + pltpu.roll(zb, LANES - HALF_ROPE, 1) * sin_a + pltpu.roll(zb, HALF_ROPE, 1) * sin_b


def _inproj_common(x_ref, w_ref, tab_ref, g_ref):
    x = x_ref[...].astype(BF16)
    tab = tab_ref[...]
    cos_t, sin_a, sin_b = tab[:, :LANES], tab[:, LANES:2 * LANES], tab[:, 2 * LANES:]
    zq = _dot(x, w_ref[:, :QK_COLS])
    blocks = [_rope_block(zq[:, h * HEAD_BLOCK:(h + 1) * HEAD_BLOCK], cos_t, sin_a, sin_b)
              for h in range(MLA_HEADS + 1)]
    zc = _dot(x, w_ref[:, QK_COLS:QK_COLS + KV_LORA])
    c = zc * lax.rsqrt(jnp.mean(zc * zc, axis=-1, keepdims=True) + LN_EPS) * g_ref[...]
    zg = _dot(x, w_ref[:, QK_COLS + KV_LORA:])
    u = zg[:, :CONV_CH] * jax.nn.sigmoid(zg[:, CONV_CH:])
    return blocks, c, u


def _inproj_prompt_kernel(x_ref, w_ref, tab_ref, g_ref, wukv_ref,
                          q_ref, ckv_ref, kr_ref, u_ref, k_ref, v_ref):
    blocks, c, u = _inproj_common(x_ref, w_ref, tab_ref, g_ref)
    kr = blocks[MLA_HEADS]
    for h in range(MLA_HEADS):
        q_ref[:, h * HEAD_BLOCK:(h + 1) * HEAD_BLOCK] = (blocks[h] * MLA_SCALE).astype(BF16)
    kr_ref[...] = kr[:, :ROPE_DIM]
    ckv_ref[...] = c
    u_ref[...] = u
    kv = _dot(c.astype(BF16), wukv_ref[...])
    for h in range(MLA_HEADS):
        k_ref[:, h * HEAD_BLOCK:(h + 1) * HEAD_BLOCK] = (kv[:, h * HEAD_BLOCK:(h + 1) * HEAD_BLOCK] + kr).astype(BF16)
    v_ref[...] = kv[:, MLA_HEADS * HEAD_BLOCK:].astype(BF16)


def _inproj_sample_kernel(x_ref, w_ref, tab_ref, g_ref, wq2l_ref,
                          q_ref, ckv_ref, kr_ref, u_ref, krblk_ref, qlat_ref):
    blocks, c, u = _inproj_common(x_ref, w_ref, tab_ref, g_ref)
    kr = blocks[MLA_HEADS]
    for h in range(MLA_HEADS):
        qb = blocks[h] * MLA_SCALE
        q_ref[:, h * HEAD_BLOCK:(h + 1) * HEAD_BLOCK] = qb
        qlat_ref[:, h * KV_LORA:(h + 1) * KV_LORA] = _dot(qb.astype(BF16), wq2l_ref[h])
    kr_ref[...] = kr[:, :ROPE_DIM]
    krblk_ref[...] = kr
    ckv_ref[...] = c
    u_ref[...] = u


def _inproj(x2d, w_all, tab, g, extra_w, *, tm, sample):
    n = x2d.shape[0]
    n_tab_blocks = tab.shape[0] // tm
    row = lambda width: pl.BlockSpec((tm, width), lambda i: (i, 0))
    in_specs = [row(D_MODEL), _resident(w_all.shape),
                pl.BlockSpec((tm, 3 * LANES), lambda i: (i % n_tab_blocks, 0)),
                _resident(g.shape), _resident(extra_w.shape)]
    common_shapes = [(D_MODEL, F32 if sample else BF16), (KV_LORA, F32), (ROPE_DIM, F32), (CONV_CH, F32)]
    if sample:
        shapes = common_shapes + [(HEAD_BLOCK, F32), (MLA_HEADS * KV_LORA, F32)]
        body = _inproj_sample_kernel
    else:
        shapes = common_shapes + [(MLA_HEADS * HEAD_BLOCK, BF16), (ATTN_WIDTH, BF16)]
        body = _inproj_prompt_kernel
    return pl.pallas_call(
        body,
        grid=(n // tm,),
        in_specs=in_specs,
        out_specs=[row(w) for w, _ in shapes],
        out_shape=[jax.ShapeDtypeStruct((n, w), dt) for w, dt in shapes],
        compiler_params=pltpu.CompilerParams(dimension_semantics=("arbitrary",), vmem_limit_bytes=48 * MIB),
        name="inproj_sample" if sample else "inproj_prompt",
    )(x2d, w_all, tab, g, extra_w)


CONV_HALO = 32
CONV_ROWS = 64


def _conv_tail(y, cb, g, b):
    t = _layer_norm(y + cb, g, b)
    return t * jax.nn.sigmoid(t)


def _conv_prompt_kernel(u_ref, w_ref, p_ref, o_ref, ext_ref, *, ts):
    @pl.when(pl.program_id(1) == 0)
    def _():
        ext_ref[0:CONV_HALO, :] = jnp.zeros((CONV_HALO, CONV_CH), F32)

    ext_ref[CONV_HALO:CONV_HALO + ts, :] = u_ref[...]
    cb, g, b = p_ref[0:1, :], p_ref[1:2, :], p_ref[2:3, :]
    off = CONV_HALO - CONV_STATE

    for r0 in range(0, ts, CONV_ROWS):
        acc = jnp.zeros((CONV_ROWS, CONV_CH), F32)
        for k in range(CONV_K):
            acc = acc + w_ref[k:k + 1, :] * ext_ref[r0 + k + off:r0 + k + off + CONV_ROWS, :]
        o_ref[r0:r0 + CONV_ROWS, :] = _conv_tail(acc, cb, g, b).astype(o_ref.dtype)
    ext_ref[0:CONV_HALO, :] = ext_ref[ts:ts + CONV_HALO, :]


def _conv_prompt(u2d, conv_w, conv_params, *, batch, seq, ts):
    n_s = seq // ts
    return pl.pallas_call(
        functools.partial(_conv_prompt_kernel, ts=ts),
        grid=(batch, n_s),
        in_specs=[pl.BlockSpec((ts, CONV_CH), lambda b, s: (b * n_s + s, 0)),
                  _resident(conv_w.shape), _resident(conv_params.shape)],
        out_specs=pl.BlockSpec((ts, CONV_CH), lambda b, s: (b * n_s + s, 0)),
        out_shape=jax.ShapeDtypeStruct((batch * seq, CONV_CH), BF16),
        scratch_shapes=[pltpu.VMEM((ts + CONV_HALO, CONV_CH), F32)],
        compiler_params=pltpu.CompilerParams(dimension_semantics=("arbitrary", "arbitrary"),
                                             vmem_limit_bytes=32 * MIB),
        name="conv_prompt",
    )(u2d, conv_w, conv_params)


def _conv_sample_kernel(s_ref, u_ref, w_ref, p_ref, o_ref, ns_ref):
    st = s_ref[...]
    u = u_ref[...]
    w_hist = w_ref[0:CONV_STATE, :]
    y = jnp.sum(st * w_hist[None, :, :], axis=1, keepdims=True) + u * w_ref[CONV_STATE:CONV_K, :][None]
    o_ref[...] = _conv_tail(y, p_ref[0:1, :][None], p_ref[1:2, :][None], p_ref[2:3, :][None]).astype(o_ref.dtype)
    ns_ref[:, 0:CONV_STATE - 1, :] = s_ref[:, 1:CONV_STATE, :]
    ns_ref[:, CONV_STATE - 1:CONV_STATE, :] = u


def _conv_sample(state, u, conv_w, conv_params, *, bb):
    nb = state.shape[0]
    return pl.pallas_call(
        _conv_sample_kernel,
        grid=(nb // bb,),
        in_specs=[pl.BlockSpec((bb, CONV_STATE, CONV_CH), lambda i: (i, 0, 0)),
                  pl.BlockSpec((bb, 1, CONV_CH), lambda i: (i, 0, 0)),
                  _resident(conv_w.shape), _resident(conv_params.shape)],
        out_specs=[pl.BlockSpec((bb, 1, CONV_CH), lambda i: (i, 0, 0)),
                   pl.BlockSpec((bb, CONV_STATE, CONV_CH), lambda i: (i, 0, 0))],
        out_shape=[jax.ShapeDtypeStruct((nb, 1, CONV_CH), BF16),
                   jax.ShapeDtypeStruct(state.shape, F32)],
        compiler_params=pltpu.CompilerParams(dimension_semantics=("arbitrary",), vmem_limit_bytes=32 * MIB),
        name="conv_sample",
    )(state, u, conv_w, conv_params)


def _flash_kernel(q_ref, k_ref, v_ref, o_ref, *, tq):
    i = pl.program_id(1)
    row = lax.broadcasted_iota(jnp.int32, (tq, tq), 0)
    col = lax.broadcasted_iota(jnp.int32, (tq, tq), 1)
    causal = col <= row
    lane = lax.broadcasted_iota(jnp.int32, (tq, LANES), 1)

    def head(h):
        pair = h // 2
        hs = slice(h * HEAD_BLOCK, (h + 1) * HEAD_BLOCK)
        vs = slice(pair * LANES, (pair + 1) * LANES)
        qh = q_ref[:, hs]

        def update(s, j, carry):
            m, l, acc = carry
            m_new = jnp.maximum(m, jnp.max(s, axis=-1, keepdims=True))
            alpha = jnp.exp(m - m_new)
            p = jnp.exp(s - m_new)
            l = alpha * l + jnp.sum(p, axis=-1, keepdims=True)
            acc = alpha * acc + _dot(p.astype(BF16), v_ref[pl.ds(j * tq, tq), vs])
            return m_new, l, acc

        def off_diag(j, carry):
            s = _dot_nt(qh, k_ref[pl.ds(pl.multiple_of(j * tq, tq), tq), hs])
            return update(s, j, carry)

        init = (jnp.full((tq, 1), NEG_BIG, F32), jnp.zeros((tq, 1), F32), jnp.zeros((tq, LANES), F32))
        carry = lax.fori_loop(0, i, off_diag, init)
        s = _dot_nt(qh, k_ref[pl.ds(pl.multiple_of(i * tq, tq), tq), hs])
        m, l, acc = update(jnp.where(causal, s, NEG_BIG), i, carry)
        return acc / l

    for pair in range(MLA_HEADS // 2):
        o_even, o_odd = head(2 * pair), head(2 * pair + 1)
        o_ref[:, pair * LANES:(pair + 1) * LANES] = jnp.where(lane < MLA_HEAD_DIM, o_even, o_odd).astype(o_ref.dtype)


def _flash(q, k, v, *, batch, seq, tq):
    nq = seq // tq
    return pl.pallas_call(
        functools.partial(_flash_kernel, tq=tq),
        grid=(batch, nq),
        in_specs=[pl.BlockSpec((tq, q.shape[1]), lambda b, i: (b * nq + i, 0)),
                  pl.BlockSpec((seq, k.shape[1]), lambda b, i: (b, 0)),
                  pl.BlockSpec((seq, v.shape[1]), lambda b, i: (b, 0))],
        out_specs=pl.BlockSpec((tq, ATTN_WIDTH), lambda b, i: (b * nq + i, 0)),
        out_shape=jax.ShapeDtypeStruct((batch * seq, ATTN_WIDTH), BF16),
        compiler_params=pltpu.CompilerParams(dimension_semantics=("arbitrary", "arbitrary"),
                                             vmem_limit_bytes=40 * MIB),
        name="mla_prompt_flash",
    )(q, k, v)


def _memkv_kernel(m_ref, w_ref, k_ref, v_ref):
    kv = _dot(m_ref[...].astype(BF16), w_ref[...])
    k_ref[...] = kv[:, :D_MODEL]
    v_ref[...] = kv[:, D_MODEL:]


def _memkv(mem2d, w_xkv, *, tm):
    n = mem2d.shape[0]
    row = pl.BlockSpec((tm, D_MODEL), lambda i: (i, 0))
    return pl.pallas_call(
        _memkv_kernel,
        grid=(n // tm,),
        in_specs=[row, _resident(w_xkv.shape)],
        out_specs=[row, row],
        out_shape=[jax.ShapeDtypeStruct((n, D_MODEL), F32)] * 2,
        compiler_params=pltpu.CompilerParams(dimension_semantics=("arbitrary",), vmem_limit_bytes=40 * MIB),
        name="mem_kv",
    )(mem2d, w_xkv)


FFN_CHUNK = 1024


def _mix_ln1_q(x, mix, ln_ref, wxq_ref):
    x1 = _layer_norm(ALPHA * x + mix, ln_ref[0:1, :], ln_ref[1:2, :])
    return x1, _dot(x1.astype(BF16), wxq_ref[...])


def _xo_ffn(x1, o_heads, ln_ref, wxo_ref, wup_ref, wdown_ref):
    att = None
    for h, oh in enumerate(o_heads):
        t = _dot(oh.astype(BF16), wxo_ref[h * MEM_HEAD_DIM:(h + 1) * MEM_HEAD_DIM, :])
        att = t if att is None else att + t
    x2 = _layer_norm(ALPHA * x1 + att, ln_ref[2:3, :], ln_ref[3:4, :])
    x2b = x2.astype(BF16)
    ffn = None
    for c in range(D_FF // FFN_CHUNK):
        cs = slice(c * FFN_CHUNK, (c + 1) * FFN_CHUNK)
        hcol = jnp.maximum(_dot(x2b, wup_ref[:, cs]), 0.0)
        t = _dot((hcol * hcol).astype(BF16), wdown_ref[cs, :])
        ffn = t if ffn is None else ffn + t
    return _layer_norm(ALPHA * x2 + ffn, ln_ref[4:5, :], ln_ref[5:6, :])


def _post_prompt_kernel(x_ref, a_ref, c_ref, mk_ref, mv_ref, ln_ref,
                        wo_ref, wxq_ref, wxo_ref, wup_ref, wdown_ref, y_ref):
    mix = _dot(a_ref[...], wo_ref[0:ATTN_WIDTH, :]) + _dot(c_ref[...], wo_ref[ATTN_WIDTH:, :])
    x1, q2 = _mix_ln1_q(x_ref[...], mix, ln_ref, wxq_ref)
    o_heads = []
    for h in range(MEM_HEADS):
        hs = slice(h * MEM_HEAD_DIM, (h + 1) * MEM_HEAD_DIM)
        s = _dot_nt((q2[:, hs] * MEM_SCALE).astype(BF16), mk_ref[:, hs].astype(BF16))
        p = jnp.exp(s - jnp.max(s, axis=-1, keepdims=True))
        inv_l = 1.0 / jnp.sum(p, axis=-1, keepdims=True)
        o_heads.append(_dot(p.astype(BF16), mv_ref[:, hs].astype(BF16)) * inv_l)
    y_ref[...] = _xo_ffn(x1, o_heads, ln_ref, wxo_ref, wup_ref, wdown_ref)


def _post_prompt(x2d, attn, conv, mk, mv, ln, w_o, w_xq, w_xo, w_up, w_down, *, batch, seq, tm, mem_tokens):
    n_t = seq // tm
    row = lambda width: pl.BlockSpec((tm, width), lambda b, t: (b * n_t + t, 0))
    mem = pl.BlockSpec((mem_tokens, D_MODEL), lambda b, t: (b, 0))
    weights = (ln, w_o, w_xq, w_xo, w_up, w_down)
    return pl.pallas_call(
        _post_prompt_kernel,
        grid=(batch, n_t),
        in_specs=[row(D_MODEL), row(ATTN_WIDTH), row(CONV_CH), mem, mem] + [_resident(w.shape) for w in weights],
        out_specs=row(D_MODEL),
        out_shape=jax.ShapeDtypeStruct((batch * seq, D_MODEL), F32),
        compiler_params=pltpu.CompilerParams(dimension_semantics=("arbitrary", "arbitrary"),
                                             vmem_limit_bytes=56 * MIB),
        name="post_prompt",
    )(x2d, attn, conv, mk, mv, *weights)


def _post1_sample_kernel(x_ref, ol_ref, c_ref, wuv_ref, ln_ref, wo_ref, wxq_ref, x1_ref, q2_ref):
    mix = _dot(c_ref[...], wo_ref[ATTN_WIDTH:, :])
    for h in range(MLA_HEADS):
        a_h = _dot(ol_ref[:, h * KV_LORA:(h + 1) * KV_LORA].astype(BF16), wuv_ref[h])
        mix = mix + _dot(a_h.astype(BF16), wo_ref[h * MLA_HEAD_DIM:(h + 1) * MLA_HEAD_DIM, :])
    x1, q2 = _mix_ln1_q(x_ref[...], mix, ln_ref, wxq_ref)
    x1_ref[...] = x1
    q2_ref[...] = q2 * MEM_SCALE


def _post1_sample(x2d, o_lat, conv, w_uv, ln, w_o, w_xq):
    args = (x2d, o_lat, conv, w_uv, ln, w_o, w_xq)
    n = x2d.shape[0]
    return pl.pallas_call(
        _post1_sample_kernel,
        grid=(1,),
        in_specs=[_resident(a.shape) for a in args],
        out_specs=[pl.BlockSpec((n, D_MODEL), lambda i: (0, 0))] * 2,
        out_shape=[jax.ShapeDtypeStruct((n, D_MODEL), F32)] * 2,
        compiler_params=pltpu.CompilerParams(dimension_semantics=("arbitrary",), vmem_limit_bytes=32 * MIB),
        name="post1_sample",
    )(*args)


def _cross_sample_kernel(q_ref, k_ref, v_ref, o_ref, *, bb):
    for i in range(bb):
        for h in range(MEM_HEADS):
            hs = slice(h * MEM_HEAD_DIM, (h + 1) * MEM_HEAD_DIM)
            qh = jnp.broadcast_to(q_ref[i, h:h + 1, :], (8, MEM_HEAD_DIM)).astype(BF16)
            s = _dot_nt(qh, k_ref[i, :, hs].astype(BF16))
            p = jnp.exp(s - jnp.max(s, axis=-1, keepdims=True))
            inv_l = 1.0 / jnp.sum(p, axis=-1, keepdims=True)
            o = _dot(p.astype(BF16), v_ref[i, :, hs].astype(BF16)) * inv_l
            o_ref[i, h:h + 1, :] = o[0:1, :]


def _cross_sample(q2, mem_k, mem_v, *, bb):
    nb, m_tok, _ = mem_k.shape
    q_spec = pl.BlockSpec((bb, MEM_HEADS, MEM_HEAD_DIM), lambda i: (i, 0, 0))
    kv_spec = pl.BlockSpec((bb, m_tok, D_MODEL), lambda i: (i, 0, 0))
    return pl.pallas_call(
        functools.partial(_cross_sample_kernel, bb=bb),
        grid=(nb // bb,),
        in_specs=[q_spec, kv_spec, kv_spec],
        out_specs=q_spec,
        out_shape=jax.ShapeDtypeStruct((nb, MEM_HEADS, MEM_HEAD_DIM), F32),
        compiler_params=pltpu.CompilerParams(dimension_semantics=("arbitrary",), vmem_limit_bytes=40 * MIB),
        name="cross_sample",
    )(q2, mem_k, mem_v)


def _post2_sample_kernel(x1_ref, o_ref, ln_ref, wxo_ref, wup_ref, wdown_ref, y_ref):
    o_heads = [o_ref[:, h * MEM_HEAD_DIM:(h + 1) * MEM_HEAD_DIM] for h in range(MEM_HEADS)]
    y_ref[...] = _xo_ffn(x1_ref[...], o_heads, ln_ref, wxo_ref, wup_ref, wdown_ref)


def _post2_sample(x1, o, ln, w_xo, w_up, w_down):
    args = (x1, o, ln, w_xo, w_up, w_down)
    n = x1.shape[0]
    return pl.pallas_call(
        _post2_sample_kernel,
        grid=(1,),
        in_specs=[_resident(a.shape) for a in args],
        out_specs=pl.BlockSpec((n, D_MODEL), lambda i: (0, 0)),
        out_shape=jax.ShapeDtypeStruct((n, D_MODEL), F32),
        compiler_params=pltpu.CompilerParams(dimension_semantics=("arbitrary",), vmem_limit_bytes=40 * MIB),
        name="post2_sample",
    )(*args)


TOK_PER_ROW = 4


def _decode_kernel(pt_ref, ql_ref, qbd_ref, cn_ref, krn_ref, cpool_ref, rpool_ref, o_ref,
                   cbuf, rbuf, sem, *, n_pages, page_rows):
    b = pl.program_id(0)
    nb = pl.num_programs(0)

    def page_copies(bi, slot, p):
        pg = pt_ref[bi * n_pages + p]
        rows = pl.ds(pl.multiple_of(p * page_rows, page_rows), page_rows)
        return (pltpu.make_async_copy(cpool_ref.at[pg], cbuf.at[slot, rows], sem.at[slot, 0]),
                pltpu.make_async_copy(rpool_ref.at[pg], rbuf.at[slot, rows], sem.at[slot, 1]))

    def start_all(bi, slot):
        def body(p, carry):
            for cp in page_copies(bi, slot, p):
                cp.start()
            return carry
        lax.fori_loop(0, n_pages, body, 0)

    def wait_all(bi, slot):
        def body(p, carry):
            for cp in page_copies(bi, slot, p):
                cp.wait()
            return carry
        lax.fori_loop(0, n_pages, body, 0)

    slot = b % 2

    @pl.when(b == 0)
    def _():
        start_all(0, 0)

    @pl.when(b + 1 < nb)
    def _():
        start_all(b + 1, 1 - slot)

    wait_all(b, slot)

    ql32 = ql_ref[0]
    qbd32 = qbd_ref[0]
    ql = ql32.astype(BF16)
    c_new = cn_ref[0]
    kr_new = krn_ref[0]

    s = _dot_nt(qbd32.astype(BF16), rbuf[slot].astype(BF16))
    c_slabs = [cbuf[slot, :, j * KV_LORA:(j + 1) * KV_LORA].astype(BF16) for j in range(TOK_PER_ROW)]
    s = s + jnp.concatenate([_dot_nt(ql, cj) for cj in c_slabs], axis=0)
    s_new = (jnp.sum(ql32 * c_new, axis=-1, keepdims=True)
             + jnp.sum(qbd32[0:MLA_HEADS, :] * kr_new, axis=-1, keepdims=True))

    m_rows = jnp.max(s, axis=-1, keepdims=True)
    m = s_new
    for j in range(TOK_PER_ROW):
        m = jnp.maximum(m, m_rows[j * MLA_HEADS:(j + 1) * MLA_HEADS])
    p = jnp.exp(s - jnp.concatenate([m] * TOK_PER_ROW, axis=0))
    p_new = jnp.exp(s_new - m)
    l_rows = jnp.sum(p, axis=-1, keepdims=True)
    l = p_new
    o = p_new * c_new
    for j in range(TOK_PER_ROW):
        hs = slice(j * MLA_HEADS, (j + 1) * MLA_HEADS)
        l = l + l_rows[hs]
        o = o + _dot(p[hs].astype(BF16), c_slabs[j])
    o_ref[0] = o / l


def _decode(page_table, q_lat, qbd, c_new, kr_new, cpool, rpool):
    nb, n_pages = page_table.shape
    page_rows = cpool.shape[1]
    rows = n_pages * page_rows
    grid_spec = pltpu.PrefetchScalarGridSpec(
        num_scalar_prefetch=1,
        grid=(nb,),
        in_specs=[pl.BlockSpec((1, MLA_HEADS, KV_LORA), lambda b, pt: (b, 0, 0)),
                  pl.BlockSpec((1, TOK_PER_ROW * MLA_HEADS, LANES), lambda b, pt: (b, 0, 0)),
                  pl.BlockSpec((1, 1, KV_LORA), lambda b, pt: (b, 0, 0)),
                  pl.BlockSpec((1, 1, LANES), lambda b, pt: (b, 0, 0)),
                  pl.BlockSpec(memory_space=pl.ANY),
                  pl.BlockSpec(memory_space=pl.ANY)],
        out_specs=pl.BlockSpec((1, MLA_HEADS, KV_LORA), lambda b, pt: (b, 0, 0)),
        scratch_shapes=[pltpu.VMEM((2, rows, TOK_PER_ROW * KV_LORA), F32),
                        pltpu.VMEM((2, rows, LANES), F32),
                        pltpu.SemaphoreType.DMA((2, 2))],
    )
    return pl.pallas_call(
        functools.partial(_decode_kernel, n_pages=n_pages, page_rows=page_rows),
        grid_spec=grid_spec,
        out_shape=jax.ShapeDtypeStruct((nb, MLA_HEADS, KV_LORA), F32),
        compiler_params=pltpu.CompilerParams(dimension_semantics=("arbitrary",), vmem_limit_bytes=48 * MIB),
        name="mla_sample_paged",
    )(page_table.reshape(-1), q_lat, qbd, c_new, kr_new, cpool, rpool)


def _rope_tables(pos):
    inv_freq = jnp.exp(-math.log(ROPE_BASE) * jnp.arange(HALF_ROPE, dtype=F32) / HALF_ROPE)
    ang = pos.astype(F32)[:, None] * inv_freq[None, :]
    cos, sin = jnp.cos(ang), jnp.sin(ang)
    n = pos.shape[0]
    zeros = jnp.zeros((n, LANES - HALF_ROPE), F32)
    cos_t = jnp.concatenate([cos, cos, jnp.ones((n, LANES - ROPE_DIM), F32)], axis=1)
    sin_a = jnp.concatenate([-sin, zeros], axis=1)
    sin_b = jnp.concatenate([jnp.zeros((n, HALF_ROPE), F32), sin, zeros[:, HALF_ROPE:]], axis=1)
    return jnp.concatenate([cos_t, sin_a, sin_b], axis=1)


def _pad_cols(w, left, right):
    return jnp.pad(w, [(0, 0)] * (w.ndim - 1) + [(left, right)])


def _layout_w_in(w_in):
    wq = w_in[:, :Q_COLS].reshape(D_MODEL, MLA_HEADS, MLA_HEAD_DIM + ROPE_DIM)
    wq = jnp.concatenate([wq[..., MLA_HEAD_DIM:], wq[..., :MLA_HEAD_DIM]], axis=-1)
    wq = _pad_cols(wq, 0, HEAD_BLOCK - (MLA_HEAD_DIM + ROPE_DIM)).reshape(D_MODEL, MLA_HEADS * HEAD_BLOCK)
    c0 = Q_COLS
    wc = w_in[:, c0:c0 + KV_LORA]
    wkr = _pad_cols(w_in[:, c0 + KV_LORA:c0 + KV_LORA + ROPE_DIM], 0, HEAD_BLOCK - ROPE_DIM)
    wg = w_in[:, c0 + KV_LORA + ROPE_DIM:]
    return jnp.concatenate([wq, wkr, wc, wg], axis=1).astype(BF16)


def kernel(x_prompt, x_sample, cache_ckv, cache_krope, state_conv, cache_mem_k, cache_mem_v, page_table, mem_prompt, w_in, kv_norm_g, w_uk, w_uv, conv_w, conv_b, conv_ln_g, conv_ln_b, w_o, ln1_g, ln1_b, w_xq, w_xk, w_xv, w_xo, ln2_g, ln2_b, w_up, w_down, ln3_g, ln3_b):
    assert w_in.shape[0] == DEPTH == 1
    B, S, _ = x_prompt.shape
    Bd, T, _ = x_sample.shape
    assert T == 1
    n_pool, page_size, _ = cache_ckv.shape[1:]
    n_pages = page_table.shape[1]
    past_len = n_pages * page_size
    mem_tokens = mem_prompt.shape[1]
    assert page_size % TOK_PER_ROW == 0
    l = 0

    w_all = _layout_w_in(w_in[l])
    g_kv = kv_norm_g[l][None, :]
    wuk_blk = _pad_cols(w_uk[l], ROPE_DIM, HEAD_BLOCK - ROPE_DIM - MLA_HEAD_DIM).reshape(KV_LORA, MLA_HEADS * HEAD_BLOCK)
    wukv = jnp.concatenate([wuk_blk, w_uv[l].reshape(KV_LORA, ATTN_WIDTH)], axis=1).astype(BF16)
    wq2l = jnp.pad(jnp.transpose(w_uk[l], (1, 2, 0)),
                   ((0, 0), (ROPE_DIM, HEAD_BLOCK - ROPE_DIM - MLA_HEAD_DIM), (0, 0))).astype(BF16)
    wuv_h = jnp.transpose(w_uv[l], (1, 0, 2)).astype(BF16)
    conv_params = jnp.stack([conv_b[l], conv_ln_g[l], conv_ln_b[l]])
    ln = jnp.stack([ln1_g[l], ln1_b[l], ln2_g[l], ln2_b[l], ln3_g[l], ln3_b[l],
                    jnp.zeros_like(ln1_g[l]), jnp.zeros_like(ln1_g[l])])
    wo_b, wxq_b, wxo_b = w_o[l].astype(BF16), w_xq[l].astype(BF16), w_xo[l].astype(BF16)
    wup_b, wdown_b = w_up[l].astype(BF16), w_down[l].astype(BF16)
    w_xkv = jnp.concatenate([w_xk[l], w_xv[l]], axis=1).astype(BF16)

    xp = x_prompt.reshape(B * S, D_MODEL)
    tab_p = _rope_tables(jnp.arange(S, dtype=jnp.int32))
    q_p, ckv_p, kr_p, u_p, k_p, v_p = _inproj(xp, w_all, tab_p, g_kv, wukv, tm=512, sample=False)
    attn_p = _flash(q_p, k_p, v_p, batch=B, seq=S, tq=256)
    conv_p = _conv_prompt(u_p, conv_w[l], conv_params, batch=B, seq=S, ts=512)
    mk_p, mv_p = _memkv(mem_prompt.reshape(B * mem_tokens, D_MODEL), w_xkv, tm=512)
    y_p = _post_prompt(xp, attn_p, conv_p, mk_p, mv_p, ln, wo_b, wxq_b, wxo_b, wup_b, wdown_b,
                       batch=B, seq=S, tm=256, mem_tokens=mem_tokens)

    xs = x_sample.reshape(Bd, D_MODEL)
    tab_s = _rope_tables(jnp.full((Bd,), past_len, dtype=jnp.int32))
    q_s, ckv_s, kr_s, u_s, krblk_s, qlat_s = _inproj(xs, w_all, tab_s, g_kv, wq2l, tm=Bd, sample=True)
    q_rope = q_s.reshape(Bd, MLA_HEADS, HEAD_BLOCK)[:, :, :ROPE_DIM]
    qbd = (jnp.eye(TOK_PER_ROW, dtype=F32)[None, :, None, :, None] * q_rope[:, None, :, None, :]
           ).reshape(Bd, TOK_PER_ROW * MLA_HEADS, TOK_PER_ROW * ROPE_DIM)
    page_rows = page_size // TOK_PER_ROW
    cpool = cache_ckv[l].reshape(n_pool, page_rows, TOK_PER_ROW * KV_LORA)
    rpool = cache_krope[l].reshape(n_pool, page_rows, TOK_PER_ROW * ROPE_DIM)
    o_lat = _decode(page_table, qlat_s.reshape(Bd, MLA_HEADS, KV_LORA), qbd,
                    ckv_s[:, None, :], krblk_s[:, None, :], cpool, rpool)
    conv_s, new_state = _conv_sample(state_conv[l], u_s[:, None, :], conv_w[l], conv_params, bb=16)
    x1_s, q2_s = _post1_sample(xs, o_lat.reshape(Bd, MLA_HEADS * KV_LORA), conv_s.reshape(Bd, CONV_CH),
                               wuv_h, ln, wo_b, wxq_b)
    o_s = _cross_sample(q2_s.reshape(Bd, MEM_HEADS, MEM_HEAD_DIM),
                        cache_mem_k[l].reshape(Bd, mem_tokens, D_MODEL),
                        cache_mem_v[l].reshape(Bd, mem_tokens, D_MODEL), bb=4)
    y_s = _post2_sample(x1_s, o_s.reshape(Bd, D_MODEL), ln, wxo_b, wup_b, wdown_b)

    mem_shape = (1, B, mem_tokens, MEM_HEADS, MEM_HEAD_DIM)
    return (y_p.reshape(B, S, D_MODEL),
            y_s.reshape(Bd, T, D_MODEL),
            ckv_p.reshape(1, B, S, KV_LORA),
            kr_p.reshape(1, B, S, ROPE_DIM),
            u_p.reshape(B, S, CONV_CH)[None, :, S - CONV_STATE:, :],
            mk_p.reshape(mem_shape),
            mv_p.reshape(mem_shape),
            ckv_s.reshape(1, Bd, T, KV_LORA),
            kr_s.reshape(1, Bd, T, ROPE_DIM),
            new_state[None])
```

```python
import functools
import math

import jax
import jax.numpy as jnp
from jax import lax
from jax.experimental import pallas as pl
from jax.experimental.pallas import tpu as pltpu

D_MODEL = 1024
MLA_HEADS = 8
MLA_HEAD_DIM = 64
ROPE_DIM = 32
KV_LORA = D_MODEL // 4
ATTN_WIDTH = MLA_HEADS * MLA_HEAD_DIM
CONV_CH = D_MODEL - ATTN_WIDTH
CONV_K = 31
CONV_STATE = CONV_K - 1
MEM_HEADS = 4
MEM_HEAD_DIM = D_MODEL // MEM_HEADS
D_FF = 4 * D_MODEL
ROPE_BASE = 10000.0
LN_EPS = 1e-5
DEPTH = 1
ALPHA = (2.0 * DEPTH) ** 0.25
Q_COLS = MLA_HEADS * (MLA_HEAD_DIM + ROPE_DIM)
LOG2E = math.log2(math.e)
MLA_QSCALE = (MLA_HEAD_DIM + ROPE_DIM) ** -0.5 * LOG2E
MEM_QSCALE = MEM_HEAD_DIM ** -0.5 * LOG2E

LANES = 128
HEAD_BLOCK = LANES
HALF_ROPE = ROPE_DIM // 2
NEG_BIG = -1e30
MIB = 1024 * 1024

F32 = jnp.float32
BF16 = jnp.bfloat16

QK_COLS = (MLA_HEADS + 1) * HEAD_BLOCK
IN_COLS_PADDED = QK_COLS + KV_LORA + 2 * CONV_CH


def _dot(a, b):
    return jnp.dot(a, b, preferred_element_type=F32)


def _dot_nt(a, b):
    return lax.dot_general(a, b, (((1,), (1,)), ((), ())), preferred_element_type=F32)


def _layer_norm(x, g, b):
    mu = jnp.mean(x, axis=-1, keepdims=True)
    xc = x - mu
    var = jnp.mean(xc * xc, axis=-1, keepdims=True)
    return xc * lax.rsqrt(var + LN_EPS) * g + b


def _resident(shape):
    nd = len(shape)
    return pl.BlockSpec(shape, lambda *_: (0,) * nd, pipeline_mode=pl.Buffered(1))


def _rope_block(zb, cos_t, sin_a, sin_b):
    return zb * cos_t + pltpu.roll(zb, LANES - HALF_ROPE, 1) * sin_a + pltpu.roll(zb, HALF_ROPE, 1) * sin_b


def _inproj_common(x_ref, w_ref, tab_ref, g_ref):
    x = x_ref[...].astype(BF16)
    tab = tab_ref[...]
    cos_t, sin_a, sin_b = tab[:, :LANES], tab[:, LANES:2 * LANES], tab[:, 2 * LANES:]
    zq = _dot(x, w_ref[:, :QK_COLS])
    blocks = [_rope_block(zq[:, h * HEAD_BLOCK:(h + 1) * HEAD_BLOCK], cos_t, sin_a, sin_b)
              for h in range(MLA_HEADS + 1)]
    zc = _dot(x, w_ref[:, QK_COLS:QK_COLS + KV_LORA])
    c = zc * lax.rsqrt(jnp.mean(zc * zc, axis=-1, keepdims=True) + LN_EPS) * g_ref[...]
    zg = _dot(x, w_ref[:, QK_COLS + KV_LORA:])
    u = zg[:, :CONV_CH] * jax.nn.sigmoid(zg[:, CONV_CH:])
    return blocks, c, u


V_ONES_ROW = MLA_HEAD_DIM


def _inproj_prompt_kernel(x_ref, w_ref, tab_ref, g_ref, wuk_ref, wuvt_ref,
                          q_ref, ckv_ref, kr_ref, u_ref, k_ref, vt_ref):
    blocks, c, u = _inproj_common(x_ref, w_ref, tab_ref, g_ref)
    kr = blocks[MLA_HEADS]
    for h in range(MLA_HEADS):
        q_ref[:, h * HEAD_BLOCK:(h + 1) * HEAD_BLOCK] = (blocks[h] * MLA_QSCALE).astype(BF16)
    kr_ref[...] = kr[:, :ROPE_DIM]
    ckv_ref[...] = c
    u_ref[...] = u
    cb = c.astype(BF16)
    kn = _dot(cb, wuk_ref[...])
    for h in range(MLA_HEADS):
        k_ref[:, h * HEAD_BLOCK:(h + 1) * HEAD_BLOCK] = (kn[:, h * HEAD_BLOCK:(h + 1) * HEAD_BLOCK] + kr).astype(BF16)
    vt = _dot_nt(wuvt_ref[...], cb)
    row = lax.broadcasted_iota(jnp.int32, vt.shape, 0)
    vt_ref[0] = jnp.where(row % HEAD_BLOCK == V_ONES_ROW, 1.0, vt).astype(BF16)


def _inproj_sample_kernel(x_ref, w_ref, tab_ref, g_ref, wq2l_ref,
                          q_ref, ckv_ref, kr_ref, u_ref, qlat_ref):
    blocks, c, u = _inproj_common(x_ref, w_ref, tab_ref, g_ref)
    kr = blocks[MLA_HEADS]
    for h in range(MLA_HEADS):
        qb = blocks[h] * MLA_QSCALE
        q_ref[:, h * HEAD_BLOCK:(h + 1) * HEAD_BLOCK] = qb
        qlat_ref[:, h * KV_LORA:(h + 1) * KV_LORA] = _dot(qb.astype(BF16), wq2l_ref[h])
    kr_ref[...] = kr[:, :ROPE_DIM]
    ckv_ref[...] = c
    u_ref[...] = u


def _inproj(x2d, w_all, tab, g, extra_ws, *, tm, sample):
    n = x2d.shape[0]
    n_tiles = n // tm
    n_tab_blocks = tab.shape[0] // tm
    row = lambda width: pl.BlockSpec((tm, width), lambda i: (i, 0))
    in_specs = [row(D_MODEL), _resident(w_all.shape),
                pl.BlockSpec((tm, 3 * LANES), lambda i: (i % n_tab_blocks, 0)),
                _resident(g.shape)] + [_resident(w.shape) for w in extra_ws]
    widths = [(D_MODEL, F32 if sample else BF16), (KV_LORA, F32), (ROPE_DIM, F32), (CONV_CH, F32)]
    widths.append((MLA_HEADS * KV_LORA, F32) if sample else (MLA_HEADS * HEAD_BLOCK, BF16))
    out_specs = [row(w) for w, _ in widths]
    out_shape = [jax.ShapeDtypeStruct((n, w), dt) for w, dt in widths]
    if not sample:
        out_specs.append(pl.BlockSpec((1, MLA_HEADS * HEAD_BLOCK, tm), lambda i: (i, 0, 0)))
        out_shape.append(jax.ShapeDtypeStruct((n_tiles, MLA_HEADS * HEAD_BLOCK, tm), BF16))
    return pl.pallas_call(
        _inproj_sample_kernel if sample else _inproj_prompt_kernel,
        grid=(n_tiles,),
        in_specs=in_specs,
        out_specs=out_specs,
        out_shape=out_shape,
        compiler_params=pltpu.CompilerParams(dimension_semantics=("arbitrary",), vmem_limit_bytes=48 * MIB),
        name="inproj_sample" if sample else "inproj_prompt",
    )(x2d, w_all, tab, g, *extra_ws)


CONV_HALO = 32
CONV_ROWS = 64


def _conv_tail(y, cb, g, b):
    t = _layer_norm(y + cb, g, b)
    return t * jax.nn.sigmoid(t)


def _conv_prompt_kernel(u_ref, w_ref, p_ref, o_ref, ext_ref, *, ts):
    @pl.when(pl.program_id(1) == 0)
    def _():
        ext_ref[0:CONV_HALO, :] = jnp.zeros((CONV_HALO, CONV_CH), F32)

    ext_ref[CONV_HALO:CONV_HALO + ts, :] = u_ref[...]
    cb, g, b = p_ref[0:1, :], p_ref[1:2, :], p_ref[2:3, :]
    off = CONV_HALO - CONV_STATE

    for r0 in range(0, ts, CONV_ROWS):
        acc = jnp.zeros((CONV_ROWS, CONV_CH), F32)
        for k in range(CONV_K):
            acc = acc + w_ref[k:k + 1, :] * ext_ref[r0 + k + off:r0 + k + off + CONV_ROWS, :]
        o_ref[r0:r0 + CONV_ROWS, :] = _conv_tail(acc, cb, g, b).astype(o_ref.dtype)
    ext_ref[0:CONV_HALO, :] = ext_ref[ts:ts + CONV_HALO, :]


def _conv_prompt(u2d, conv_w, conv_params, *, batch, seq, ts):
    n_s = seq // ts
    return pl.pallas_call(
        functools.partial(_conv_prompt_kernel, ts=ts),
        grid=(batch, n_s),
        in_specs=[pl.BlockSpec((ts, CONV_CH), lambda b, s: (b * n_s + s, 0)),
                  _resident(conv_w.shape), _resident(conv_params.shape)],
        out_specs=pl.BlockSpec((ts, CONV_CH), lambda b, s: (b * n_s + s, 0)),
        out_shape=jax.ShapeDtypeStruct((batch * seq, CONV_CH), BF16),
        scratch_shapes=[pltpu.VMEM((ts + CONV_HALO, CONV_CH), F32)],
        compiler_params=pltpu.CompilerParams(dimension_semantics=("arbitrary", "arbitrary"),
                                             vmem_limit_bytes=32 * MIB),
        name="conv_prompt",
    )(u2d, conv_w, conv_params)


def _conv_sample_kernel(s_ref, u_ref, w_ref, p_ref, o_ref, ns_ref):
    st = s_ref[...]
    u = u_ref[...]
    w_hist = w_ref[0:CONV_STATE, :]
    y = jnp.sum(st * w_hist[None, :, :], axis=1, keepdims=True) + u * w_ref[CONV_STATE:CONV_K, :][None]
    o_ref[...] = _conv_tail(y, p_ref[0:1, :][None], p_ref[1:2, :][None], p_ref[2:3, :][None]).astype(o_ref.dtype)
    ns_ref[:, 0:CONV_STATE - 1, :] = s_ref[:, 1:CONV_STATE, :]
    ns_ref[:, CONV_STATE - 1:CONV_STATE, :] = u


def _conv_sample(state, u, conv_w, conv_params, *, bb):
    nb = state.shape[0]
    return pl.pallas_call(
        _conv_sample_kernel,
        grid=(nb // bb,),
        in_specs=[pl.BlockSpec((bb, CONV_STATE, CONV_CH), lambda i: (i, 0, 0)),
                  pl.BlockSpec((bb, 1, CONV_CH), lambda i: (i, 0, 0)),
                  _resident(conv_w.shape), _resident(conv_params.shape)],
        out_specs=[pl.BlockSpec((bb, 1, CONV_CH), lambda i: (i, 0, 0)),
                   pl.BlockSpec((bb, CONV_STATE, CONV_CH), lambda i: (i, 0, 0))],
        out_shape=[jax.ShapeDtypeStruct((nb, 1, CONV_CH), BF16),
                   jax.ShapeDtypeStruct(state.shape, F32)],
        compiler_params=pltpu.CompilerParams(dimension_semantics=("arbitrary",), vmem_limit_bytes=32 * MIB),
        name="conv_sample",
    )(state, u, conv_w, conv_params)


def _flash_kernel(q_ref, k_ref, vt_ref, o_ref, *, t):
    i = pl.program_id(1)
    kpos = lax.broadcasted_iota(jnp.int32, (t, t), 0)
    qpos = lax.broadcasted_iota(jnp.int32, (t, t), 1)
    causal = kpos <= qpos

    def update(h, st, j, carry):
        hs = slice(h * HEAD_BLOCK, (h + 1) * HEAD_BLOCK)
        m, acc = carry
        m_new = jnp.maximum(m, jnp.max(st, axis=0, keepdims=True))
        alpha = jnp.exp2(m - m_new)
        pt = jnp.exp2(st - m_new).astype(BF16)
        return m_new, alpha * acc + _dot(vt_ref[j, hs, :], pt)

    def scores_t(h, j):
        hs = slice(h * HEAD_BLOCK, (h + 1) * HEAD_BLOCK)
        return _dot_nt(k_ref[pl.ds(pl.multiple_of(j * t, t), t), hs], q_ref[:, hs])

    def finish(carry):
        _, acc = carry
        return acc[0:MLA_HEAD_DIM, :] / acc[V_ONES_ROW:V_ONES_ROW + 1, :]

    init = (jnp.full((1, t), NEG_BIG, F32), jnp.zeros((HEAD_BLOCK, t), F32))
    for pair in range(MLA_HEADS // 2):
        heads = (2 * pair, 2 * pair + 1)
        carries = lax.fori_loop(
            0, i, lambda j, cs: tuple(update(h, scores_t(h, j), j, c) for h, c in zip(heads, cs)), (init, init))
        outs = [finish(update(h, jnp.where(causal, scores_t(h, i), NEG_BIG), i, c)) for h, c in zip(heads, carries)]
        o_ref[:, pair * LANES:(pair + 1) * LANES] = jnp.concatenate(outs, axis=0).T.astype(o_ref.dtype)


def _flash(q, k, vt, *, batch, seq, t):
    n_t = seq // t
    return pl.pallas_call(
        functools.partial(_flash_kernel, t=t),
        grid=(batch, n_t),
        in_specs=[pl.BlockSpec((t, q.shape[1]), lambda b, i: (b * n_t + i, 0)),
                  pl.BlockSpec((seq, k.shape[1]), lambda b, i: (b, 0)),
                  pl.BlockSpec((n_t, vt.shape[1], t), lambda b, i: (b, 0, 0))],
        out_specs=pl.BlockSpec((t, ATTN_WIDTH), lambda b, i: (b * n_t + i, 0)),
        out_shape=jax.ShapeDtypeStruct((batch * seq, ATTN_WIDTH), BF16),
        compiler_params=pltpu.CompilerParams(dimension_semantics=("arbitrary", "arbitrary"),
                                             vmem_limit_bytes=48 * MIB),
        name="mla_prompt_flash",
    )(q, k, vt)


def _memkv_kernel(m_ref, w_ref, k_ref, v_ref):
    kv = _dot(m_ref[...].astype(BF16), w_ref[...])
    k_ref[...] = kv[:, :D_MODEL]
    v_ref[...] = kv[:, D_MODEL:]


def _memkv(mem2d, w_xkv, *, tm):
    n = mem2d.shape[0]
    row = pl.BlockSpec((tm, D_MODEL), lambda i: (i, 0))
    return pl.pallas_call(
        _memkv_kernel,
        grid=(n // tm,),
        in_specs=[row, _resident(w_xkv.shape)],
        out_specs=[row, row],
        out_shape=[jax.ShapeDtypeStruct((n, D_MODEL), F32)] * 2,
        compiler_params=pltpu.CompilerParams(dimension_semantics=("arbitrary",), vmem_limit_bytes=40 * MIB),
        name="mem_kv",
    )(mem2d, w_xkv)


FFN_CHUNK = 1024


def _mix_ln1_q(x, mix, ln_ref, wxq_ref):
    x1 = _layer_norm(ALPHA * x + mix, ln_ref[0:1, :], ln_ref[1:2, :])
    return x1, _dot(x1.astype(BF16), wxq_ref[...])


def _xo_ffn(x1, o_heads, ln_ref, wxo_ref, wup_ref, wdown_ref):
    att = None
    for h, oh in enumerate(o_heads):
        t = _dot(oh.astype(BF16), wxo_ref[h * MEM_HEAD_DIM:(h + 1) * MEM_HEAD_DIM, :])
        att = t if att is None else att + t
    x2 = _layer_norm(ALPHA * x1 + att, ln_ref[2:3, :], ln_ref[3:4, :])
    x2b = x2.astype(BF16)
    ffn = None
    for c in range(D_FF // FFN_CHUNK):
        cs = slice(c * FFN_CHUNK, (c + 1) * FFN_CHUNK)
        hcol = jnp.maximum(_dot(x2b, wup_ref[:, cs]), 0.0)
        t = _dot((hcol * hcol).astype(BF16), wdown_ref[cs, :])
        ffn = t if ffn is None else ffn + t
    return _layer_norm(ALPHA * x2 + ffn, ln_ref[4:5, :], ln_ref[5:6, :])


def _post_prompt_kernel(x_ref, a_ref, c_ref, mk_ref, mv_ref, ln_ref,
                        wo_ref, wxq_ref, wxo_ref, wup_ref, wdown_ref, y_ref):
    mix = _dot(a_ref[...], wo_ref[0:ATTN_WIDTH, :]) + _dot(c_ref[...], wo_ref[ATTN_WIDTH:, :])
    x1, q2 = _mix_ln1_q(x_ref[...], mix, ln_ref, wxq_ref)
    o_heads = []
    for h in range(MEM_HEADS):
        hs = slice(h * MEM_HEAD_DIM, (h + 1) * MEM_HEAD_DIM)
        s = _dot_nt((q2[:, hs] * MEM_QSCALE).astype(BF16), mk_ref[:, hs].astype(BF16))
        p = jnp.exp2(s - jnp.max(s, axis=-1, keepdims=True))
        inv_l = 1.0 / jnp.sum(p, axis=-1, keepdims=True)
        o_heads.append(_dot(p.astype(BF16), mv_ref[:, hs].astype(BF16)) * inv_l)
    y_ref[...] = _xo_ffn(x1, o_heads, ln_ref, wxo_ref, wup_ref, wdown_ref)


def _post_prompt(x2d, attn, conv, mk, mv, ln, w_o, w_xq, w_xo, w_up, w_down, *, batch, seq, tm, mem_tokens):
    n_t = seq // tm
    row = lambda width: pl.BlockSpec((tm, width), lambda b, t: (b * n_t + t, 0))
    mem = pl.BlockSpec((mem_tokens, D_MODEL), lambda b, t: (b, 0))
    weights = (ln, w_o, w_xq, w_xo, w_up, w_down)
    return pl.pallas_call(
        _post_prompt_kernel,
        grid=(batch, n_t),
        in_specs=[row(D_MODEL), row(ATTN_WIDTH), row(CONV_CH), mem, mem] + [_resident(w.shape) for w in weights],
        out_specs=row(D_MODEL),
        out_shape=jax.ShapeDtypeStruct((batch * seq, D_MODEL), F32),
        compiler_params=pltpu.CompilerParams(dimension_semantics=("arbitrary", "arbitrary"),
                                             vmem_limit_bytes=56 * MIB),
        name="post_prompt",
    )(x2d, attn, conv, mk, mv, *weights)


def _post1_sample_kernel(x_ref, ol_ref, c_ref, wuv_ref, ln_ref, wo_ref, wxq_ref, x1_ref, q2_ref):
    mix = _dot(c_ref[...], wo_ref[ATTN_WIDTH:, :])
    for h in range(MLA_HEADS):
        a_h = _dot(ol_ref[:, h * KV_LORA:(h + 1) * KV_LORA].astype(BF16), wuv_ref[h])
        mix = mix + _dot(a_h.astype(BF16), wo_ref[h * MLA_HEAD_DIM:(h + 1) * MLA_HEAD_DIM, :])
    x1, q2 = _mix_ln1_q(x_ref[...], mix, ln_ref, wxq_ref)
    x1_ref[...] = x1
    q2_ref[...] = q2 * MEM_QSCALE


def _post1_sample(x2d, o_lat, conv, w_uv, ln, w_o, w_xq):
    args = (x2d, o_lat, conv, w_uv, ln, w_o, w_xq)
    n = x2d.shape[0]
    return pl.pallas_call(
        _post1_sample_kernel,
        grid=(1,),
        in_specs=[_resident(a.shape) for a in args],
        out_specs=[pl.BlockSpec((n, D_MODEL), lambda i: (0, 0))] * 2,
        out_shape=[jax.ShapeDtypeStruct((n, D_MODEL), F32)] * 2,
        compiler_params=pltpu.CompilerParams(dimension_semantics=("arbitrary",), vmem_limit_bytes=32 * MIB),
        name="post1_sample",
    )(*args)


def _cross_sample_kernel(q_ref, k_ref, v_ref, o_ref, *, bb):
    for i in range(bb):
        for h in range(MEM_HEADS):
            hs = slice(h * MEM_HEAD_DIM, (h + 1) * MEM_HEAD_DIM)
            qh = jnp.broadcast_to(q_ref[i, h:h + 1, :], (8, MEM_HEAD_DIM)).astype(BF16)
            s = _dot_nt(qh, k_ref[i, :, hs].astype(BF16))
            p = jnp.exp2(s - jnp.max(s, axis=-1, keepdims=True))
            inv_l = 1.0 / jnp.sum(p, axis=-1, keepdims=True)
            o = _dot(p.astype(BF16), v_ref[i, :, hs].astype(BF16)) * inv_l
            o_ref[i, h:h + 1, :] = o[0:1, :]


def _cross_sample(q2, mem_k, mem_v, *, bb):
    nb, m_tok, _ = mem_k.shape
    q_spec = pl.BlockSpec((bb, MEM_HEADS, MEM_HEAD_DIM), lambda i: (i, 0, 0))
    kv_spec = pl.BlockSpec((bb, m_tok, D_MODEL), lambda i: (i, 0, 0))
    return pl.pallas_call(
        functools.partial(_cross_sample_kernel, bb=bb),
        grid=(nb // bb,),
        in_specs=[q_spec, kv_spec, kv_spec],
        out_specs=q_spec,
        out_shape=jax.ShapeDtypeStruct((nb, MEM_HEADS, MEM_HEAD_DIM), F32),
        compiler_params=pltpu.CompilerParams(dimension_semantics=("arbitrary",), vmem_limit_bytes=40 * MIB),
        name="cross_sample",
    )(q2, mem_k, mem_v)


def _post2_sample_kernel(x1_ref, o_ref, ln_ref, wxo_ref, wup_ref, wdown_ref, y_ref):
    o_heads = [o_ref[:, h * MEM_HEAD_DIM:(h + 1) * MEM_HEAD_DIM] for h in range(MEM_HEADS)]
    y_ref[...] = _xo_ffn(x1_ref[...], o_heads, ln_ref, wxo_ref, wup_ref, wdown_ref)


def _post2_sample(x1, o, ln, w_xo, w_up, w_down):
    args = (x1, o, ln, w_xo, w_up, w_down)
    n = x1.shape[0]
    return pl.pallas_call(
        _post2_sample_kernel,
        grid=(1,),
        in_specs=[_resident(a.shape) for a in args],
        out_specs=pl.BlockSpec((n, D_MODEL), lambda i: (0, 0)),
        out_shape=jax.ShapeDtypeStruct((n, D_MODEL), F32),
        compiler_params=pltpu.CompilerParams(dimension_semantics=("arbitrary",), vmem_limit_bytes=40 * MIB),
        name="post2_sample",
    )(*args)


def _decode_kernel(pt_ref, ql_ref, qr_ref, cn_ref, krn_ref, cpool_ref, rpool_ref, o_ref,
                   cbuf, rbuf, sem, *, n_pages, page_size):
    b = pl.program_id(0)
    nb = pl.num_programs(0)

    def page_copies(bi, slot, p):
        pg = pt_ref[bi * n_pages + p]
        toks = pl.ds(pl.multiple_of(p * page_size, page_size), page_size)
        return (pltpu.make_async_copy(cpool_ref.at[0, pg], cbuf.at[slot, toks, :], sem.at[slot, 0]),
                pltpu.make_async_copy(rpool_ref.at[pg], rbuf.at[slot, :, toks], sem.at[slot, 1]))

    def start_all(bi, slot):
        def body(p, carry):
            for cp in page_copies(bi, slot, p):
                cp.start()
            return carry
        lax.fori_loop(0, n_pages, body, 0)

    def wait_all(bi, slot):
        def body(p, carry):
            for cp in page_copies(bi, slot, p):
                cp.wait()
            return carry
        lax.fori_loop(0, n_pages, body, 0)

    slot = b % 2

    @pl.when(b == 0)
    def _():
        start_all(0, 0)

    @pl.when(b + 1 < nb)
    def _():
        start_all(b + 1, 1 - slot)

    wait_all(b, slot)

    ql32 = ql_ref[0]
    qr32 = qr_ref[0]
    c_new = cn_ref[0]
    kr_new = krn_ref[0]

    cb = cbuf[slot].astype(BF16)
    s = _dot_nt(ql32.astype(BF16), cb) + _dot(qr32.astype(BF16), rbuf[slot].astype(BF16))
    s_new = (jnp.sum(ql32 * c_new, axis=-1, keepdims=True)
             + jnp.sum(qr32 * kr_new, axis=-1, keepdims=True))
    m = jnp.maximum(jnp.max(s, axis=-1, keepdims=True), s_new)
    p = jnp.exp2(s - m)
    p_new = jnp.exp2(s_new - m)
    l = jnp.sum(p, axis=-1, keepdims=True) + p_new
    o = _dot(p.astype(BF16), cb) + p_new * c_new
    o_ref[0] = o / l


def _decode(page_table, q_lat, q_rope, c_new, kr_new, cpool, rpool):
    nb, n_pages = page_table.shape
    page_size = cpool.shape[2]
    past = n_pages * page_size
    per_seq = lambda *tail: pl.BlockSpec((1,) + tail, lambda b, pt: (b, 0, 0))
    grid_spec = pltpu.PrefetchScalarGridSpec(
        num_scalar_prefetch=1,
        grid=(nb,),
        in_specs=[per_seq(MLA_HEADS, KV_LORA), per_seq(MLA_HEADS, ROPE_DIM), per_seq(1, KV_LORA), per_seq(1, ROPE_DIM),
                  pl.BlockSpec(memory_space=pl.ANY),
                  pl.BlockSpec(memory_space=pl.ANY)],
        out_specs=per_seq(MLA_HEADS, KV_LORA),
        scratch_shapes=[pltpu.VMEM((2, past, KV_LORA), F32),
                        pltpu.VMEM((2, ROPE_DIM, past), F32),
                        pltpu.SemaphoreType.DMA((2, 2))],
    )
    return pl.pallas_call(
        functools.partial(_decode_kernel, n_pages=n_pages, page_size=page_size),
        grid_spec=grid_spec,
        out_shape=jax.ShapeDtypeStruct((nb, MLA_HEADS, KV_LORA), F32),
        compiler_params=pltpu.CompilerParams(dimension_semantics=("arbitrary",), vmem_limit_bytes=48 * MIB),
        name="mla_sample_paged",
    )(page_table.reshape(-1), q_lat, q_rope, c_new, kr_new, cpool, rpool)


def _rope_tables(pos):
    inv_freq = jnp.exp(-math.log(ROPE_BASE) * jnp.arange(HALF_ROPE, dtype=F32) / HALF_ROPE)
    ang = pos.astype(F32)[:, None] * inv_freq[None, :]
    cos, sin = jnp.cos(ang), jnp.sin(ang)
    n = pos.shape[0]
    zeros = jnp.zeros((n, LANES - HALF_ROPE), F32)
    cos_t = jnp.concatenate([cos, cos, jnp.ones((n, LANES - ROPE_DIM), F32)], axis=1)
    sin_a = jnp.concatenate([-sin, zeros], axis=1)
    sin_b = jnp.concatenate([jnp.zeros((n, HALF_ROPE), F32), sin, zeros[:, HALF_ROPE:]], axis=1)
    return jnp.concatenate([cos_t, sin_a, sin_b], axis=1)


def _pad_cols(w, left, right):
    return jnp.pad(w, [(0, 0)] * (w.ndim - 1) + [(left, right)])


def _layout_w_in(w_in):
    wq = w_in[:, :Q_COLS].reshape(D_MODEL, MLA_HEADS, MLA_HEAD_DIM + ROPE_DIM)
    wq = jnp.concatenate([wq[..., MLA_HEAD_DIM:], wq[..., :MLA_HEAD_DIM]], axis=-1)
    wq = _pad_cols(wq, 0, HEAD_BLOCK - (MLA_HEAD_DIM + ROPE_DIM)).reshape(D_MODEL, MLA_HEADS * HEAD_BLOCK)
    c0 = Q_COLS
    wc = w_in[:, c0:c0 + KV_LORA]
    wkr = _pad_cols(w_in[:, c0 + KV_LORA:c0 + KV_LORA + ROPE_DIM], 0, HEAD_BLOCK - ROPE_DIM)
    wg = w_in[:, c0 + KV_LORA + ROPE_DIM:]
    return jnp.concatenate([wq, wkr, wc, wg], axis=1).astype(BF16)


def kernel(x_prompt, x_sample, cache_ckv, cache_krope, state_conv, cache_mem_k, cache_mem_v, page_table, mem_prompt, w_in, kv_norm_g, w_uk, w_uv, conv_w, conv_b, conv_ln_g, conv_ln_b, w_o, ln1_g, ln1_b, w_xq, w_xk, w_xv, w_xo, ln2_g, ln2_b, w_up, w_down, ln3_g, ln3_b):
    assert w_in.shape[0] == DEPTH == 1
    B, S, _ = x_prompt.shape
    Bd, T, _ = x_sample.shape
    assert T == 1
    n_pool, page_size, _ = cache_ckv.shape[1:]
    n_pages = page_table.shape[1]
    past_len = n_pages * page_size
    mem_tokens = mem_prompt.shape[1]
    l = 0

    w_all = _layout_w_in(w_in[l])
    g_kv = kv_norm_g[l][None, :]
    wuk_blk = _pad_cols(w_uk[l], ROPE_DIM, HEAD_BLOCK - ROPE_DIM - MLA_HEAD_DIM
                        ).reshape(KV_LORA, MLA_HEADS * HEAD_BLOCK).astype(BF16)
    wuv_t = jnp.pad(jnp.transpose(w_uv[l], (1, 2, 0)), ((0, 0), (0, HEAD_BLOCK - MLA_HEAD_DIM), (0, 0))
                    ).reshape(MLA_HEADS * HEAD_BLOCK, KV_LORA).astype(BF16)
    wq2l = jnp.pad(jnp.transpose(w_uk[l], (1, 2, 0)),
                   ((0, 0), (ROPE_DIM, HEAD_BLOCK - ROPE_DIM - MLA_HEAD_DIM), (0, 0))).astype(BF16)
    wuv_h = jnp.transpose(w_uv[l], (1, 0, 2)).astype(BF16)
    conv_params = jnp.stack([conv_b[l], conv_ln_g[l], conv_ln_b[l]])
    ln = jnp.stack([ln1_g[l], ln1_b[l], ln2_g[l], ln2_b[l], ln3_g[l], ln3_b[l],
                    jnp.zeros_like(ln1_g[l]), jnp.zeros_like(ln1_g[l])])
    wo_b, wxq_b, wxo_b = w_o[l].astype(BF16), w_xq[l].astype(BF16), w_xo[l].astype(BF16)
    wup_b, wdown_b = w_up[l].astype(BF16), w_down[l].astype(BF16)
    w_xkv = jnp.concatenate([w_xk[l], w_xv[l]], axis=1).astype(BF16)

    xp = x_prompt.reshape(B * S, D_MODEL)
    tab_p = _rope_tables(jnp.arange(S, dtype=jnp.int32))
    attn_tile = 512
    q_p, ckv_p, kr_p, u_p, k_p, vt_p = _inproj(xp, w_all, tab_p, g_kv, (wuk_blk, wuv_t), tm=attn_tile, sample=False)
    attn_p = _flash(q_p, k_p, vt_p, batch=B, seq=S, t=attn_tile)
    conv_p = _conv_prompt(u_p, conv_w[l], conv_params, batch=B, seq=S, ts=512)
    mk_p, mv_p = _memkv(mem_prompt.reshape(B * mem_tokens, D_MODEL), w_xkv, tm=512)
    y_p = _post_prompt(xp, attn_p, conv_p, mk_p, mv_p, ln, wo_b, wxq_b, wxo_b, wup_b, wdown_b,
                       batch=B, seq=S, tm=256, mem_tokens=mem_tokens)

    xs = x_sample.reshape(Bd, D_MODEL)
    tab_s = _rope_tables(jnp.full((Bd,), past_len, dtype=jnp.int32))
    q_s, ckv_s, kr_s, u_s, qlat_s = _inproj(xs, w_all, tab_s, g_kv, (wq2l,), tm=Bd, sample=True)
    q_rope = q_s.reshape(Bd, MLA_HEADS, HEAD_BLOCK)[:, :, :ROPE_DIM]
    rpool = jnp.swapaxes(cache_krope[l], 1, 2)
    o_lat = _decode(page_table, qlat_s.reshape(Bd, MLA_HEADS, KV_LORA), q_rope,
                    ckv_s[:, None, :], kr_s[:, None, :], cache_ckv, rpool)
    conv_s, new_state = _conv_sample(state_conv[l], u_s[:, None, :], conv_w[l], conv_params, bb=16)
    x1_s, q2_s = _post1_sample(xs, o_lat.reshape(Bd, MLA_HEADS * KV_LORA), conv_s.reshape(Bd, CONV_CH),
                               wuv_h, ln, wo_b, wxq_b)
    o_s = _cross_sample(q2_s.reshape(Bd, MEM_HEADS, MEM_HEAD_DIM),
                        cache_mem_k[l].reshape(Bd, mem_tokens, D_MODEL),
                        cache_mem_v[l].reshape(Bd, mem_tokens, D_MODEL), bb=4)
    y_s = _post2_sample(x1_s, o_s.reshape(Bd, D_MODEL), ln, wxo_b, wup_b, wdown_b)

    mem_shape = (1, B, mem_tokens, MEM_HEADS, MEM_HEAD_DIM)
    return (y_p.reshape(B, S, D_MODEL),
            y_s.reshape(Bd, T, D_MODEL),
            ckv_p.reshape(1, B, S, KV_LORA),
            kr_p.reshape(1, B, S, ROPE_DIM),
            u_p.reshape(B, S, CONV_CH)[None, :, S - CONV_STATE:, :],
            mk_p.reshape(mem_shape),
            mv_p.reshape(mem_shape),
            ckv_s.reshape(1, Bd, T, KV_LORA),
            kr_s.reshape(1, Bd, T, ROPE_DIM),
            new_state[None])
```

```python
import functools
import math

import jax
import jax.numpy as jnp
from jax import lax
from jax.experimental import pallas as pl
from jax.experimental.pallas import tpu as pltpu

D_MODEL = 1024
MLA_HEADS = 8
MLA_HEAD_DIM = 64
ROPE_DIM = 32
KV_LORA = D_MODEL // 4
ATTN_WIDTH = MLA_HEADS * MLA_HEAD_DIM
CONV_CH = D_MODEL - ATTN_WIDTH
CONV_K = 31
CONV_STATE = CONV_K - 1
MEM_HEADS = 4
MEM_HEAD_DIM = D_MODEL // MEM_HEADS
D_FF = 4 * D_MODEL
ROPE_BASE = 10000.0
LN_EPS = 1e-5
DEPTH = 1
ALPHA = (2.0 * DEPTH) ** 0.25
Q_COLS = MLA_HEADS * (MLA_HEAD_DIM + ROPE_DIM)
LOG2E = math.log2(math.e)
MLA_QSCALE = (MLA_HEAD_DIM + ROPE_DIM) ** -0.5 * LOG2E
MEM_QSCALE = MEM_HEAD_DIM ** -0.5 * LOG2E

LANES = 128
SUBLANES = 8
HEAD_BLOCK = LANES
HALF_ROPE = ROPE_DIM // 2
NEG_BIG = -1e30
MIB = 1024 * 1024

F32 = jnp.float32
BF16 = jnp.bfloat16

QK_COLS = (MLA_HEADS + 1) * HEAD_BLOCK
IN_COLS_PADDED = QK_COLS + KV_LORA + 2 * CONV_CH


def _dot(a, b):
    return jnp.dot(a, b, preferred_element_type=F32)


def _dot_nt(a, b):
    return lax.dot_general(a, b, (((1,), (1,)), ((), ())), preferred_element_type=F32)


def _layer_norm(x, g, b):
    mu = jnp.mean(x, axis=-1, keepdims=True)
    xc = x - mu
    var = jnp.mean(xc * xc, axis=-1, keepdims=True)
    return xc * lax.rsqrt(var + LN_EPS) * g + b


def _resident(shape):
    nd = len(shape)
    return pl.BlockSpec(shape, lambda *_: (0,) * nd, pipeline_mode=pl.Buffered(1))


def _rope_block(zb, cos_t, sin_a, sin_b):
    return zb * cos_t + pltpu.roll(zb, LANES - HALF_ROPE, 1) * sin_a + pltpu.roll(zb, HALF_ROPE, 1) * sin_b


def _inproj_common(x_ref, w_ref, tab_ref, g_ref):
    x = x_ref[...].astype(BF16)
    tab = tab_ref[...]
    cos_t, sin_a, sin_b = tab[:, :LANES], tab[:, LANES:2 * LANES], tab[:, 2 * LANES:]
    zq = _dot(x, w_ref[:, :QK_COLS])
    blocks = [_rope_block(zq[:, h * HEAD_BLOCK:(h + 1) * HEAD_BLOCK], cos_t, sin_a, sin_b)
              for h in range(MLA_HEADS + 1)]
    zc = _dot(x, w_ref[:, QK_COLS:QK_COLS + KV_LORA])
    c = zc * lax.rsqrt(jnp.mean(zc * zc, axis=-1, keepdims=True) + LN_EPS) * g_ref[...]
    zg = _dot(x, w_ref[:, QK_COLS + KV_LORA:])
    u = zg[:, :CONV_CH] * jax.nn.sigmoid(zg[:, CONV_CH:])
    return blocks, c, u


V_ONES_ROW = MLA_HEAD_DIM


def _inproj_prompt_kernel(x_ref, w_ref, tab_ref, g_ref, wuk_ref, wuvt_ref,
                          q_ref, ckv_ref, kr_ref, u_ref, k_ref, vt_ref):
    blocks, c, u = _inproj_common(x_ref, w_ref, tab_ref, g_ref)
    kr = blocks[MLA_HEADS]
    for h in range(MLA_HEADS):
        q_ref[:, h * HEAD_BLOCK:(h + 1) * HEAD_BLOCK] = (blocks[h] * MLA_QSCALE).astype(BF16)
    kr_ref[...] = kr[:, :ROPE_DIM]
    ckv_ref[...] = c
    u_ref[...] = u
    cb = c.astype(BF16)
    kn = _dot(cb, wuk_ref[...])
    for h in range(MLA_HEADS):
        k_ref[:, h * HEAD_BLOCK:(h + 1) * HEAD_BLOCK] = (kn[:, h * HEAD_BLOCK:(h + 1) * HEAD_BLOCK] + kr).astype(BF16)
    vt = _dot_nt(wuvt_ref[...], cb)
    row = lax.broadcasted_iota(jnp.int32, vt.shape, 0)
    vt_ref[0] = jnp.where(row % HEAD_BLOCK == V_ONES_ROW, 1.0, vt).astype(BF16)


def _inproj_sample_kernel(x_ref, w_ref, tab_ref, g_ref, wq2l_ref,
                          q_ref, ckv_ref, kr_ref, u_ref, qlat_ref):
    blocks, c, u = _inproj_common(x_ref, w_ref, tab_ref, g_ref)
    kr = blocks[MLA_HEADS]
    for h in range(MLA_HEADS):
        qb = blocks[h] * MLA_QSCALE
        q_ref[:, h * HEAD_BLOCK:(h + 1) * HEAD_BLOCK] = qb
        qlat_ref[:, h * KV_LORA:(h + 1) * KV_LORA] = _dot(qb.astype(BF16), wq2l_ref[h])
    kr_ref[...] = kr[:, :ROPE_DIM]
    ckv_ref[...] = c
    u_ref[...] = u


def _inproj(x2d, w_all, tab, g, extra_ws, *, tm, sample):
    n = x2d.shape[0]
    n_tiles = n // tm
    n_tab_blocks = tab.shape[0] // tm
    row = lambda width: pl.BlockSpec((tm, width), lambda i: (i, 0))
    in_specs = [row(D_MODEL), _resident(w_all.shape),
                pl.BlockSpec((tm, 3 * LANES), lambda i: (i % n_tab_blocks, 0)),
                _resident(g.shape)] + [_resident(w.shape) for w in extra_ws]
    widths = [(D_MODEL, F32 if sample else BF16), (KV_LORA, F32), (ROPE_DIM, F32), (CONV_CH, F32)]
    widths.append((MLA_HEADS * KV_LORA, F32) if sample else (MLA_HEADS * HEAD_BLOCK, BF16))
    out_specs = [row(w) for w, _ in widths]
    out_shape = [jax.ShapeDtypeStruct((n, w), dt) for w, dt in widths]
    if not sample:
        out_specs.append(pl.BlockSpec((1, MLA_HEADS * HEAD_BLOCK, tm), lambda i: (i, 0, 0)))
        out_shape.append(jax.ShapeDtypeStruct((n_tiles, MLA_HEADS * HEAD_BLOCK, tm), BF16))
    return pl.pallas_call(
        _inproj_sample_kernel if sample else _inproj_prompt_kernel,
        grid=(n_tiles,),
        in_specs=in_specs,
        out_specs=out_specs,
        out_shape=out_shape,
        compiler_params=pltpu.CompilerParams(dimension_semantics=("arbitrary",), vmem_limit_bytes=48 * MIB),
        name="inproj_sample" if sample else "inproj_prompt",
    )(x2d, w_all, tab, g, *extra_ws)


CONV_HALO = 32
CONV_ROWS = 64


def _conv_tail(y, cb, g, b):
    t = _layer_norm(y + cb, g, b)
    return t * jax.nn.sigmoid(t)


def _conv_prompt_kernel(u_ref, w_ref, p_ref, o_ref, ext_ref, sh_ref, *, ts):
    @pl.when(pl.program_id(1) == 0)
    def _():
        ext_ref[0:CONV_HALO, :] = jnp.zeros((CONV_HALO, CONV_CH), F32)

    ext_ref[CONV_HALO:CONV_HALO + ts, :] = u_ref[...]
    cb, g, b = p_ref[0:1, :], p_ref[1:2, :], p_ref[2:3, :]
    off = CONV_HALO - CONV_STATE
    for r in range(1, SUBLANES):
        sh_ref[r - 1] = ext_ref[r:r + sh_ref.shape[1], :]

    for r0 in range(0, ts, CONV_ROWS):
        acc = jnp.zeros((CONV_ROWS, CONV_CH), F32)
        for k in range(CONV_K):
            a, r = divmod(k + off, SUBLANES)
            lo = r0 + SUBLANES * a
            src = ext_ref[lo:lo + CONV_ROWS, :] if r == 0 else sh_ref[r - 1, lo:lo + CONV_ROWS, :]
            acc = acc + w_ref[k:k + 1, :] * src
        o_ref[r0:r0 + CONV_ROWS, :] = _conv_tail(acc, cb, g, b).astype(o_ref.dtype)
    ext_ref[0:CONV_HALO, :] = ext_ref[ts:ts + CONV_HALO, :]


def _conv_prompt(u2d, conv_w, conv_params, *, batch, seq, ts):
    n_s = seq // ts
    return pl.pallas_call(
        functools.partial(_conv_prompt_kernel, ts=ts),
        grid=(batch, n_s),
        in_specs=[pl.BlockSpec((ts, CONV_CH), lambda b, s: (b * n_s + s, 0)),
                  _resident(conv_w.shape), _resident(conv_params.shape)],
        out_specs=pl.BlockSpec((ts, CONV_CH), lambda b, s: (b * n_s + s, 0)),
        out_shape=jax.ShapeDtypeStruct((batch * seq, CONV_CH), BF16),
        scratch_shapes=[pltpu.VMEM((ts + CONV_HALO, CONV_CH), F32),
                        pltpu.VMEM((SUBLANES - 1, ts + CONV_HALO - SUBLANES, CONV_CH), F32)],
        compiler_params=pltpu.CompilerParams(dimension_semantics=("arbitrary", "arbitrary"),
                                             vmem_limit_bytes=32 * MIB),
        name="conv_prompt",
    )(u2d, conv_w, conv_params)


def _conv_sample_kernel(s_ref, u_ref, w_ref, p_ref, o_ref, ns_ref):
    st = s_ref[...]
    u = u_ref[...]
    w_hist = w_ref[0:CONV_STATE, :]
    y = jnp.sum(st * w_hist[None, :, :], axis=1, keepdims=True) + u * w_ref[CONV_STATE:CONV_K, :][None]
    o_ref[...] = _conv_tail(y, p_ref[0:1, :][None], p_ref[1:2, :][None], p_ref[2:3, :][None]).astype(o_ref.dtype)
    ns_ref[:, 0:CONV_STATE - 1, :] = s_ref[:, 1:CONV_STATE, :]
    ns_ref[:, CONV_STATE - 1:CONV_STATE, :] = u


def _conv_sample(state, u, conv_w, conv_params, *, bb):
    nb = state.shape[0]
    return pl.pallas_call(
        _conv_sample_kernel,
        grid=(nb // bb,),
        in_specs=[pl.BlockSpec((bb, CONV_STATE, CONV_CH), lambda i: (i, 0, 0)),
                  pl.BlockSpec((bb, 1, CONV_CH), lambda i: (i, 0, 0)),
                  _resident(conv_w.shape), _resident(conv_params.shape)],
        out_specs=[pl.BlockSpec((bb, 1, CONV_CH), lambda i: (i, 0, 0)),
                   pl.BlockSpec((bb, CONV_STATE, CONV_CH), lambda i: (i, 0, 0))],
        out_shape=[jax.ShapeDtypeStruct((nb, 1, CONV_CH), BF16),
                   jax.ShapeDtypeStruct(state.shape, F32)],
        compiler_params=pltpu.CompilerParams(dimension_semantics=("arbitrary",), vmem_limit_bytes=32 * MIB),
        name="conv_sample",
    )(state, u, conv_w, conv_params)


FLASH_HEAD_GROUP = 4


def _flash_kernel(q_ref, k_ref, vt_ref, o_ref, *state, t):
    i = pl.program_id(1)
    kpos = lax.broadcasted_iota(jnp.int32, (t, t), 0)
    qpos = lax.broadcasted_iota(jnp.int32, (t, t), 1)
    causal = kpos <= qpos

    m_refs, acc_refs = state[:MLA_HEADS], state[MLA_HEADS:]
    for h in range(MLA_HEADS):
        m_refs[h][...] = jnp.full((1, t), NEG_BIG, F32)
        acc_refs[h][...] = jnp.zeros((HEAD_BLOCK, t), F32)

    def kv_tile(j, masked):
        rows = pl.ds(pl.multiple_of(j * t, t), t)
        for g in range(0, MLA_HEADS, FLASH_HEAD_GROUP):
            heads = range(g, g + FLASH_HEAD_GROUP)
            hs = {h: slice(h * HEAD_BLOCK, (h + 1) * HEAD_BLOCK) for h in heads}
            sts = {h: _dot_nt(k_ref[rows, hs[h]], q_ref[:, hs[h]]) for h in heads}
            pts, alphas = {}, {}
            for h in heads:
                st = jnp.where(causal, sts[h], NEG_BIG) if masked else sts[h]
                m_old = m_refs[h][...]
                m_new = jnp.maximum(m_old, jnp.max(st, axis=0, keepdims=True))
                m_refs[h][...] = m_new
                alphas[h] = jnp.exp2(m_old - m_new)
                pts[h] = jnp.exp2(st - m_new).astype(BF16)
            for h in heads:
                acc_refs[h][...] = alphas[h] * acc_refs[h][...] + _dot(vt_ref[j, hs[h], :], pts[h])

    def off_diag(j, carry):
        kv_tile(j, False)
        return carry

    lax.fori_loop(0, i, off_diag, 0)
    kv_tile(i, True)

    def finish(h):
        acc = acc_refs[h][...]
        return acc[0:MLA_HEAD_DIM, :] / acc[V_ONES_ROW:V_ONES_ROW + 1, :]

    for pair in range(MLA_HEADS // 2):
        o_t = jnp.concatenate([finish(2 * pair), finish(2 * pair + 1)], axis=0)
        o_ref[:, pair * LANES:(pair + 1) * LANES] = o_t.T.astype(o_ref.dtype)


def _flash(q, k, vt, *, batch, seq, t):
    n_t = seq // t
    return pl.pallas_call(
        functools.partial(_flash_kernel, t=t),
        grid=(batch, n_t),
        in_specs=[pl.BlockSpec((t, q.shape[1]), lambda b, i: (b * n_t + i, 0)),
                  pl.BlockSpec((seq, k.shape[1]), lambda b, i: (b, 0)),
                  pl.BlockSpec((n_t, vt.shape[1], t), lambda b, i: (b, 0, 0))],
        out_specs=pl.BlockSpec((t, ATTN_WIDTH), lambda b, i: (b * n_t + i, 0)),
        out_shape=jax.ShapeDtypeStruct((batch * seq, ATTN_WIDTH), BF16),
        scratch_shapes=[pltpu.VMEM((1, t), F32)] * MLA_HEADS + [pltpu.VMEM((HEAD_BLOCK, t), F32)] * MLA_HEADS,
        compiler_params=pltpu.CompilerParams(dimension_semantics=("arbitrary", "arbitrary"),
                                             vmem_limit_bytes=48 * MIB),
        name="mla_prompt_flash",
    )(q, k, vt)


def _memkv_kernel(m_ref, w_ref, k_ref, v_ref):
    kv = _dot(m_ref[...].astype(BF16), w_ref[...])
    k_ref[...] = kv[:, :D_MODEL]
    v_ref[...] = kv[:, D_MODEL:]


def _memkv(mem2d, w_xkv, *, tm):
    n = mem2d.shape[0]
    row = pl.BlockSpec((tm, D_MODEL), lambda i: (i, 0))
    return pl.pallas_call(
        _memkv_kernel,
        grid=(n // tm,),
        in_specs=[row, _resident(w_xkv.shape)],
        out_specs=[row, row],
        out_shape=[jax.ShapeDtypeStruct((n, D_MODEL), F32)] * 2,
        compiler_params=pltpu.CompilerParams(dimension_semantics=("arbitrary",), vmem_limit_bytes=40 * MIB),
        name="mem_kv",
    )(mem2d, w_xkv)


FFN_CHUNK = 1024


def _mix_ln1_q(x, mix, ln_ref, wxq_ref):
    x1 = _layer_norm(ALPHA * x + mix, ln_ref[0:1, :], ln_ref[1:2, :])
    return x1, _dot(x1.astype(BF16), wxq_ref[...])


def _xo_ffn(x1, o_heads, ln_ref, wxo_ref, wup_ref, wdown_ref):
    att = None
    for h, oh in enumerate(o_heads):
        t = _dot(oh.astype(BF16), wxo_ref[h * MEM_HEAD_DIM:(h + 1) * MEM_HEAD_DIM, :])
        att = t if att is None else att + t
    x2 = _layer_norm(ALPHA * x1 + att, ln_ref[2:3, :], ln_ref[3:4, :])
    x2b = x2.astype(BF16)
    ffn = None
    for c in range(D_FF // FFN_CHUNK):
        cs = slice(c * FFN_CHUNK, (c + 1) * FFN_CHUNK)
        hcol = jnp.maximum(_dot(x2b, wup_ref[:, cs]), 0.0)
        t = _dot((hcol * hcol).astype(BF16), wdown_ref[cs, :])
        ffn = t if ffn is None else ffn + t
    return _layer_norm(ALPHA * x2 + ffn, ln_ref[4:5, :], ln_ref[5:6, :])


def _post_prompt_kernel(x_ref, a_ref, c_ref, mk_ref, mv_ref, ln_ref,
                        wo_ref, wxq_ref, wxo_ref, wup_ref, wdown_ref, y_ref):
    mix = _dot(a_ref[...], wo_ref[0:ATTN_WIDTH, :]) + _dot(c_ref[...], wo_ref[ATTN_WIDTH:, :])
    x1, q2 = _mix_ln1_q(x_ref[...], mix, ln_ref, wxq_ref)
    o_heads = []
    for h in range(MEM_HEADS):
        hs = slice(h * MEM_HEAD_DIM, (h + 1) * MEM_HEAD_DIM)
        s = _dot_nt((q2[:, hs] * MEM_QSCALE).astype(BF16), mk_ref[:, hs].astype(BF16))
        p = jnp.exp2(s - jnp.max(s, axis=-1, keepdims=True))
        inv_l = 1.0 / jnp.sum(p, axis=-1, keepdims=True)
        o_heads.append(_dot(p.astype(BF16), mv_ref[:, hs].astype(BF16)) * inv_l)
    y_ref[...] = _xo_ffn(x1, o_heads, ln_ref, wxo_ref, wup_ref, wdown_ref)


def _post_prompt(x2d, attn, conv, mk, mv, ln, w_o, w_xq, w_xo, w_up, w_down, *, batch, seq, tm, mem_tokens):
    n_t = seq // tm
    row = lambda width: pl.BlockSpec((tm, width), lambda b, t: (b * n_t + t, 0))
    mem = pl.BlockSpec((mem_tokens, D_MODEL), lambda b, t: (b, 0))
    weights = (ln, w_o, w_xq, w_xo, w_up, w_down)
    return pl.pallas_call(
        _post_prompt_kernel,
        grid=(batch, n_t),
        in_specs=[row(D_MODEL), row(ATTN_WIDTH), row(CONV_CH), mem, mem] + [_resident(w.shape) for w in weights],
        out_specs=row(D_MODEL),
        out_shape=jax.ShapeDtypeStruct((batch * seq, D_MODEL), F32),
        compiler_params=pltpu.CompilerParams(dimension_semantics=("arbitrary", "arbitrary"),
                                             vmem_limit_bytes=56 * MIB),
        name="post_prompt",
    )(x2d, attn, conv, mk, mv, *weights)


def _post1_sample_kernel(x_ref, ol_ref, c_ref, wuv_ref, ln_ref, wo_ref, wxq_ref, x1_ref, q2_ref):
    mix = _dot(c_ref[...], wo_ref[ATTN_WIDTH:, :])
    for h in range(MLA_HEADS):
        a_h = _dot(ol_ref[:, h * KV_LORA:(h + 1) * KV_LORA].astype(BF16), wuv_ref[h])
        mix = mix + _dot(a_h.astype(BF16), wo_ref[h * MLA_HEAD_DIM:(h + 1) * MLA_HEAD_DIM, :])
    x1, q2 = _mix_ln1_q(x_ref[...], mix, ln_ref, wxq_ref)
    x1_ref[...] = x1
    q2_ref[...] = q2 * MEM_QSCALE


def _post1_sample(x2d, o_lat, conv, w_uv, ln, w_o, w_xq):
    args = (x2d, o_lat, conv, w_uv, ln, w_o, w_xq)
    n = x2d.shape[0]
    return pl.pallas_call(
        _post1_sample_kernel,
        grid=(1,),
        in_specs=[_resident(a.shape) for a in args],
        out_specs=[pl.BlockSpec((n, D_MODEL), lambda i: (0, 0))] * 2,
        out_shape=[jax.ShapeDtypeStruct((n, D_MODEL), F32)] * 2,
        compiler_params=pltpu.CompilerParams(dimension_semantics=("arbitrary",), vmem_limit_bytes=32 * MIB),
        name="post1_sample",
    )(*args)


def _cross_sample_kernel(q_ref, k_ref, v_ref, o_ref, *, bb):
    for i in range(bb):
        q = q_ref[i]
        s = jnp.sum(k_ref[i] * q[None], axis=-1, keepdims=True)
        p = jnp.exp2(s - jnp.max(s, axis=0, keepdims=True))
        l = jnp.sum(p, axis=0)
        o_ref[i] = jnp.sum(p * v_ref[i], axis=0) / l


def _cross_sample(q2, mem_k, mem_v, *, bb):
    nb, m_tok = mem_k.shape[:2]
    q_spec = pl.BlockSpec((bb, MEM_HEADS, MEM_HEAD_DIM), lambda i: (i, 0, 0))
    kv_spec = pl.BlockSpec((bb, m_tok, MEM_HEADS, MEM_HEAD_DIM), lambda i: (i, 0, 0, 0))
    return pl.pallas_call(
        functools.partial(_cross_sample_kernel, bb=bb),
        grid=(nb // bb,),
        in_specs=[q_spec, kv_spec, kv_spec],
        out_specs=q_spec,
        out_shape=jax.ShapeDtypeStruct((nb, MEM_HEADS, MEM_HEAD_DIM), F32),
        compiler_params=pltpu.CompilerParams(dimension_semantics=("arbitrary",), vmem_limit_bytes=40 * MIB),
        name="cross_sample",
    )(q2, mem_k, mem_v)


def _post2_sample_kernel(x1_ref, o_ref, ln_ref, wxo_ref, wup_ref, wdown_ref, y_ref):
    o_heads = [o_ref[:, h * MEM_HEAD_DIM:(h + 1) * MEM_HEAD_DIM] for h in range(MEM_HEADS)]
    y_ref[...] = _xo_ffn(x1_ref[...], o_heads, ln_ref, wxo_ref, wup_ref, wdown_ref)


def _post2_sample(x1, o, ln, w_xo, w_up, w_down):
    args = (x1, o, ln, w_xo, w_up, w_down)
    n = x1.shape[0]
    return pl.pallas_call(
        _post2_sample_kernel,
        grid=(1,),
        in_specs=[_resident(a.shape) for a in args],
        out_specs=pl.BlockSpec((n, D_MODEL), lambda i: (0, 0)),
        out_shape=jax.ShapeDtypeStruct((n, D_MODEL), F32),
        compiler_params=pltpu.CompilerParams(dimension_semantics=("arbitrary",), vmem_limit_bytes=40 * MIB),
        name="post2_sample",
    )(*args)


DECODE_CHUNK_PAGES = 16
DECODE_ISSUE_UNROLL = 8
def _decode_kernel(pt_ref, ql_ref, qr_ref, cn_ref, krn_ref, cpool_ref, rpool_ref, o_ref,
                   cbuf, rbuf, sem, *, n_pages, page_size):
    b = pl.program_id(0)
    nb = pl.num_programs(0)

    def start_all(bi, slot):
        def body(p, carry):
            pg = pt_ref[bi * n_pages + p]
            pltpu.make_async_copy(cpool_ref.at[0, pg], cbuf.at[slot, p], sem.at[slot, 0]).start()
            pltpu.make_async_copy(rpool_ref.at[pg], rbuf.at[slot, p], sem.at[slot, 1]).start()
            return carry
        lax.fori_loop(0, n_pages, body, 0, unroll=DECODE_ISSUE_UNROLL)

    def wait_all(slot):
        pltpu.make_async_copy(cpool_ref.at[0, pl.ds(0, n_pages)], cbuf.at[slot], sem.at[slot, 0]).wait()
        pltpu.make_async_copy(rpool_ref.at[pl.ds(0, n_pages)], rbuf.at[slot], sem.at[slot, 1]).wait()

    slot = b % 2

    @pl.when(b == 0)
    def _():
        start_all(0, 0)

    @pl.when(b + 1 < nb)
    def _():
        start_all(b + 1, 1 - slot)

    wait_all(slot)

    ql32 = ql_ref[0]
    qr32 = qr_ref[0]
    c_new = cn_ref[0]
    kr_new = krn_ref[0]
    ql, qr = ql32.astype(BF16), qr32.astype(BF16)

    n_chunks = n_pages // DECODE_CHUNK_PAGES
    c_chunks, s_chunks = [], []
    for c in range(n_chunks):
        pages = range(c * DECODE_CHUNK_PAGES, (c + 1) * DECODE_CHUNK_PAGES)
        cb = cbuf[slot, pages.start:pages.stop].reshape(DECODE_CHUNK_PAGES * page_size, KV_LORA).astype(BF16)
        rb = jnp.concatenate([rbuf[slot, p] for p in pages], axis=1).astype(BF16)
        c_chunks.append(cb)
        s_chunks.append(_dot_nt(ql, cb) + _dot(qr, rb))
    s_new = (jnp.sum(ql32 * c_new, axis=-1, keepdims=True)
             + jnp.sum(qr32 * kr_new, axis=-1, keepdims=True))
    m = s_new
    for s in s_chunks:
        m = jnp.maximum(m, jnp.max(s, axis=-1, keepdims=True))
    p_new = jnp.exp2(s_new - m)
    l = p_new
    o = p_new * c_new
    for s, cb in zip(s_chunks, c_chunks):
        p = jnp.exp2(s - m)
        l = l + jnp.sum(p, axis=-1, keepdims=True)
        o = o + _dot(p.astype(BF16), cb)
    o_ref[0] = o / l


def _decode(page_table, q_lat, q_rope, c_new, kr_new, cpool, rpool):
    nb, n_pages = page_table.shape
    page_size = cpool.shape[2]
    assert n_pages % DECODE_CHUNK_PAGES == 0 and n_pages % DECODE_ISSUE_UNROLL == 0
    per_seq = lambda *tail: pl.BlockSpec((1,) + tail, lambda b, pt: (b, 0, 0))
    grid_spec = pltpu.PrefetchScalarGridSpec(
        num_scalar_prefetch=1,
        grid=(nb,),
        in_specs=[per_seq(MLA_HEADS, KV_LORA), per_seq(MLA_HEADS, ROPE_DIM), per_seq(1, KV_LORA), per_seq(1, ROPE_DIM),
                  pl.BlockSpec(memory_space=pl.ANY),
                  pl.BlockSpec(memory_space=pl.ANY)],
        out_specs=per_seq(MLA_HEADS, KV_LORA),
        scratch_shapes=[pltpu.VMEM((2, n_pages, page_size, KV_LORA), F32),
                        pltpu.VMEM((2, n_pages, ROPE_DIM, page_size), F32),
                        pltpu.SemaphoreType.DMA((2, 2))],
    )
    return pl.pallas_call(
        functools.partial(_decode_kernel, n_pages=n_pages, page_size=page_size),
        grid_spec=grid_spec,
        out_shape=jax.ShapeDtypeStruct((nb, MLA_HEADS, KV_LORA), F32),
        compiler_params=pltpu.CompilerParams(dimension_semantics=("arbitrary",), vmem_limit_bytes=48 * MIB),
        name="mla_sample_paged",
    )(page_table.reshape(-1), q_lat, q_rope, c_new, kr_new, cpool, rpool)


def _rope_tables(pos):
    inv_freq = jnp.exp(-math.log(ROPE_BASE) * jnp.arange(HALF_ROPE, dtype=F32) / HALF_ROPE)
    ang = pos.astype(F32)[:, None] * inv_freq[None, :]
    cos, sin = jnp.cos(ang), jnp.sin(ang)
    n = pos.shape[0]
    zeros = jnp.zeros((n, LANES - HALF_ROPE), F32)
    cos_t = jnp.concatenate([cos, cos, jnp.ones((n, LANES - ROPE_DIM), F32)], axis=1)
    sin_a = jnp.concatenate([-sin, zeros], axis=1)
    sin_b = jnp.concatenate([jnp.zeros((n, HALF_ROPE), F32), sin, zeros[:, HALF_ROPE:]], axis=1)
    return jnp.concatenate([cos_t, sin_a, sin_b], axis=1)


def _pad_cols(w, left, right):
    return jnp.pad(w, [(0, 0)] * (w.ndim - 1) + [(left, right)])


def _layout_w_in(w_in):
    wq = w_in[:, :Q_COLS].reshape(D_MODEL, MLA_HEADS, MLA_HEAD_DIM + ROPE_DIM)
    wq = jnp.concatenate([wq[..., MLA_HEAD_DIM:], wq[..., :MLA_HEAD_DIM]], axis=-1)
    wq = _pad_cols(wq, 0, HEAD_BLOCK - (MLA_HEAD_DIM + ROPE_DIM)).reshape(D_MODEL, MLA_HEADS * HEAD_BLOCK)
    c0 = Q_COLS
    wc = w_in[:, c0:c0 + KV_LORA]
    wkr = _pad_cols(w_in[:, c0 + KV_LORA:c0 + KV_LORA + ROPE_DIM], 0, HEAD_BLOCK - ROPE_DIM)
    wg = w_in[:, c0 + KV_LORA + ROPE_DIM:]
    return jnp.concatenate([wq, wkr, wc, wg], axis=1).astype(BF16)


def kernel(x_prompt, x_sample, cache_ckv, cache_krope, state_conv, cache_mem_k, cache_mem_v, page_table, mem_prompt, w_in, kv_norm_g, w_uk, w_uv, conv_w, conv_b, conv_ln_g, conv_ln_b, w_o, ln1_g, ln1_b, w_xq, w_xk, w_xv, w_xo, ln2_g, ln2_b, w_up, w_down, ln3_g, ln3_b):
    assert w_in.shape[0] == DEPTH == 1
    B, S, _ = x_prompt.shape
    Bd, T, _ = x_sample.shape
    assert T == 1
    n_pool, page_size, _ = cache_ckv.shape[1:]
    n_pages = page_table.shape[1]
    past_len = n_pages * page_size
    mem_tokens = mem_prompt.shape[1]
    l = 0

    w_all = _layout_w_in(w_in[l])
    g_kv = kv_norm_g[l][None, :]
    wuk_blk = _pad_cols(w_uk[l], ROPE_DIM, HEAD_BLOCK - ROPE_DIM - MLA_HEAD_DIM
                        ).reshape(KV_LORA, MLA_HEADS * HEAD_BLOCK).astype(BF16)
    wuv_t = jnp.pad(jnp.transpose(w_uv[l], (1, 2, 0)), ((0, 0), (0, HEAD_BLOCK - MLA_HEAD_DIM), (0, 0))
                    ).reshape(MLA_HEADS * HEAD_BLOCK, KV_LORA).astype(BF16)
    wq2l = jnp.pad(jnp.transpose(w_uk[l], (1, 2, 0)),
                   ((0, 0), (ROPE_DIM, HEAD_BLOCK - ROPE_DIM - MLA_HEAD_DIM), (0, 0))).astype(BF16)
    wuv_h = jnp.transpose(w_uv[l], (1, 0, 2)).astype(BF16)
    conv_params = jnp.stack([conv_b[l], conv_ln_g[l], conv_ln_b[l]])
    ln = jnp.stack([ln1_g[l], ln1_b[l], ln2_g[l], ln2_b[l], ln3_g[l], ln3_b[l],
                    jnp.zeros_like(ln1_g[l]), jnp.zeros_like(ln1_g[l])])
    wo_b, wxq_b, wxo_b = w_o[l].astype(BF16), w_xq[l].astype(BF16), w_xo[l].astype(BF16)
    wup_b, wdown_b = w_up[l].astype(BF16), w_down[l].astype(BF16)
    w_xkv = jnp.concatenate([w_xk[l], w_xv[l]], axis=1).astype(BF16)

    xp = x_prompt.reshape(B * S, D_MODEL)
    tab_p = _rope_tables(jnp.arange(S, dtype=jnp.int32))
    attn_tile = 512
    q_p, ckv_p, kr_p, u_p, k_p, vt_p = _inproj(xp, w_all, tab_p, g_kv, (wuk_blk, wuv_t), tm=attn_tile, sample=False)
    attn_p = _flash(q_p, k_p, vt_p, batch=B, seq=S, t=attn_tile)
    conv_p = _conv_prompt(u_p, conv_w[l], conv_params, batch=B, seq=S, ts=512)
    mk_p, mv_p = _memkv(mem_prompt.reshape(B * mem_tokens, D_MODEL), w_xkv, tm=512)
    y_p = _post_prompt(xp, attn_p, conv_p, mk_p, mv_p, ln, wo_b, wxq_b, wxo_b, wup_b, wdown_b,
                       batch=B, seq=S, tm=512, mem_tokens=mem_tokens)

    xs = x_sample.reshape(Bd, D_MODEL)
    tab_s = _rope_tables(jnp.full((Bd,), past_len, dtype=jnp.int32))
    q_s, ckv_s, kr_s, u_s, qlat_s = _inproj(xs, w_all, tab_s, g_kv, (wq2l,), tm=Bd, sample=True)
    q_rope = q_s.reshape(Bd, MLA_HEADS, HEAD_BLOCK)[:, :, :ROPE_DIM]
    rpool = jnp.swapaxes(cache_krope[l], 1, 2)
    o_lat = _decode(page_table, qlat_s.reshape(Bd, MLA_HEADS, KV_LORA), q_rope,
                    ckv_s[:, None, :], kr_s[:, None, :], cache_ckv, rpool)
    conv_s, new_state = _conv_sample(state_conv[l], u_s[:, None, :], conv_w[l], conv_params, bb=16)
    x1_s, q2_s = _post1_sample(xs, o_lat.reshape(Bd, MLA_HEADS * KV_LORA), conv_s.reshape(Bd, CONV_CH),
                               wuv_h, ln, wo_b, wxq_b)
    o_s = _cross_sample(q2_s.reshape(Bd, MEM_HEADS, MEM_HEAD_DIM), cache_mem_k[l], cache_mem_v[l], bb=4)
    y_s = _post2_sample(x1_s, o_s.reshape(Bd, D_MODEL), ln, wxo_b, wup_b, wdown_b)

    mem_shape = (1, B, mem_tokens, MEM_HEADS, MEM_HEAD_DIM)
    return (y_p.reshape(B, S, D_MODEL),
            y_s.reshape(Bd, T, D_MODEL),
            ckv_p.reshape(1, B, S, KV_LORA),
            kr_p.reshape(1, B, S, ROPE_DIM),
            u_p.reshape(B, S, CONV_CH)[None, :, S - CONV_STATE:, :],
            mk_p.reshape(mem_shape),
            mv_p.reshape(mem_shape),
            ckv_s.reshape(1, Bd, T, KV_LORA),
            kr_s.reshape(1, Bd, T, ROPE_DIM),
            new_state[None])
```

```python
import functools
import math

import jax
import jax.numpy as jnp
from jax import lax
from jax.experimental import pallas as pl
from jax.experimental.pallas import tpu as pltpu

D_MODEL = 1024
MLA_HEADS = 8
MLA_HEAD_DIM = 64
ROPE_DIM = 32
KV_LORA = D_MODEL // 4
ATTN_WIDTH = MLA_HEADS * MLA_HEAD_DIM
CONV_CH = D_MODEL - ATTN_WIDTH
CONV_K = 31
CONV_STATE = CONV_K - 1
MEM_HEADS = 4
MEM_HEAD_DIM = D_MODEL // MEM_HEADS
D_FF = 4 * D_MODEL
ROPE_BASE = 10000.0
LN_EPS = 1e-5
DEPTH = 1
ALPHA = (2.0 * DEPTH) ** 0.25
Q_COLS = MLA_HEADS * (MLA_HEAD_DIM + ROPE_DIM)
LOG2E = math.log2(math.e)
MLA_QSCALE = (MLA_HEAD_DIM + ROPE_DIM) ** -0.5 * LOG2E
MEM_QSCALE = MEM_HEAD_DIM ** -0.5 * LOG2E

LANES = 128
SUBLANES = 8
BF16_SUBLANES = 16
HEAD_BLOCK = LANES
HALF_ROPE = ROPE_DIM // 2
NEG_BIG = -1e30
MIB = 1024 * 1024

F32 = jnp.float32
BF16 = jnp.bfloat16

QK_COLS = (MLA_HEADS + 1) * HEAD_BLOCK
IN_COLS_PADDED = QK_COLS + KV_LORA + 2 * CONV_CH


def _dot(a, b):
    return jnp.dot(a, b, preferred_element_type=F32)


def _dot_nt(a, b):
    return lax.dot_general(a, b, (((1,), (1,)), ((), ())), preferred_element_type=F32)


def _layer_norm(x, g, b):
    mu = jnp.mean(x, axis=-1, keepdims=True)
    xc = x - mu
    var = jnp.mean(xc * xc, axis=-1, keepdims=True)
    return xc * lax.rsqrt(var + LN_EPS) * g + b


def _resident(shape):
    nd = len(shape)
    return pl.BlockSpec(shape, lambda *_: (0,) * nd, pipeline_mode=pl.Buffered(1))


def _rope_block(zb, cos_t, sin_a, sin_b):
    return zb * cos_t + pltpu.roll(zb, LANES - HALF_ROPE, 1) * sin_a + pltpu.roll(zb, HALF_ROPE, 1) * sin_b


def _inproj_common(x_ref, w_ref, tab_ref, g_ref):
    x = x_ref[...].astype(BF16)
    tab = tab_ref[...]
    cos_t, sin_a, sin_b = tab[:, :LANES], tab[:, LANES:2 * LANES], tab[:, 2 * LANES:]
    zq = _dot(x, w_ref[:, :QK_COLS])
    blocks = [_rope_block(zq[:, h * HEAD_BLOCK:(h + 1) * HEAD_BLOCK], cos_t, sin_a, sin_b)
              for h in range(MLA_HEADS + 1)]
    zc = _dot(x, w_ref[:, QK_COLS:QK_COLS + KV_LORA])
    c = zc * lax.rsqrt(jnp.mean(zc * zc, axis=-1, keepdims=True) + LN_EPS) * g_ref[...]
    zg = _dot(x, w_ref[:, QK_COLS + KV_LORA:])
    u = zg[:, :CONV_CH] * jax.nn.sigmoid(zg[:, CONV_CH:])
    return blocks, c, u


V_ONES_ROW = MLA_HEAD_DIM
V_BLOCK = -(-(MLA_HEAD_DIM + 1) // BF16_SUBLANES) * BF16_SUBLANES


def _inproj_prompt_kernel(x_ref, w_ref, tab_ref, g_ref, wuk_ref, wuvt_ref,
                          q_ref, ckv_ref, kr_ref, u_ref, k_ref, vt_ref):
    blocks, c, u = _inproj_common(x_ref, w_ref, tab_ref, g_ref)
    kr = blocks[MLA_HEADS]
    for h in range(MLA_HEADS):
        q_ref[:, h * HEAD_BLOCK:(h + 1) * HEAD_BLOCK] = (blocks[h] * MLA_QSCALE).astype(BF16)
    kr_ref[...] = kr[:, :ROPE_DIM]
    ckv_ref[...] = c
    u_ref[...] = u
    cb = c.astype(BF16)
    kn = _dot(cb, wuk_ref[...])
    for h in range(MLA_HEADS):
        k_ref[:, h * HEAD_BLOCK:(h + 1) * HEAD_BLOCK] = (kn[:, h * HEAD_BLOCK:(h + 1) * HEAD_BLOCK] + kr).astype(BF16)
    vt = _dot_nt(wuvt_ref[...], cb)
    row = lax.broadcasted_iota(jnp.int32, vt.shape, 0)
    vt_ref[0] = jnp.where(row % V_BLOCK == V_ONES_ROW, 1.0, vt).astype(BF16)


def _inproj_sample_kernel(x_ref, w_ref, tab_ref, g_ref, wq2l_ref,
                          q_ref, ckv_ref, kr_ref, u_ref, qlat_ref):
    blocks, c, u = _inproj_common(x_ref, w_ref, tab_ref, g_ref)
    kr = blocks[MLA_HEADS]
    for h in range(MLA_HEADS):
        qb = blocks[h] * MLA_QSCALE
        q_ref[:, h * HEAD_BLOCK:(h + 1) * HEAD_BLOCK] = qb
        qlat_ref[:, h * KV_LORA:(h + 1) * KV_LORA] = _dot(qb.astype(BF16), wq2l_ref[h])
    kr_ref[...] = kr[:, :ROPE_DIM]
    ckv_ref[...] = c
    u_ref[...] = u


def _inproj(x2d, w_all, tab, g, extra_ws, *, tm, sample):
    n = x2d.shape[0]
    n_tiles = n // tm
    n_tab_blocks = tab.shape[0] // tm
    row = lambda width: pl.BlockSpec((tm, width), lambda i: (i, 0))
    in_specs = [row(D_MODEL), _resident(w_all.shape),
                pl.BlockSpec((tm, 3 * LANES), lambda i: (i % n_tab_blocks, 0)),
                _resident(g.shape)] + [_resident(w.shape) for w in extra_ws]
    widths = [(D_MODEL, F32 if sample else BF16), (KV_LORA, F32), (ROPE_DIM, F32), (CONV_CH, F32)]
    widths.append((MLA_HEADS * KV_LORA, F32) if sample else (MLA_HEADS * HEAD_BLOCK, BF16))
    out_specs = [row(w) for w, _ in widths]
    out_shape = [jax.ShapeDtypeStruct((n, w), dt) for w, dt in widths]
    if not sample:
        out_specs.append(pl.BlockSpec((1, MLA_HEADS * V_BLOCK, tm), lambda i: (i, 0, 0)))
        out_shape.append(jax.ShapeDtypeStruct((n_tiles, MLA_HEADS * V_BLOCK, tm), BF16))
    return pl.pallas_call(
        _inproj_sample_kernel if sample else _inproj_prompt_kernel,
        grid=(n_tiles,),
        in_specs=in_specs,
        out_specs=out_specs,
        out_shape=out_shape,
        compiler_params=pltpu.CompilerParams(dimension_semantics=("arbitrary",), vmem_limit_bytes=48 * MIB),
        name="inproj_sample" if sample else "inproj_prompt",
    )(x2d, w_all, tab, g, *extra_ws)


CONV_HALO = 32
CONV_ROWS = 64


def _conv_tail(y, cb, g, b):
    t = _layer_norm(y + cb, g, b)
    return t * jax.nn.sigmoid(t)


def _conv_prompt_kernel(u_ref, w_ref, p_ref, o_ref, ext_ref, sh_ref, *, ts):
    @pl.when(pl.program_id(1) == 0)
    def _():
        ext_ref[0:CONV_HALO, :] = jnp.zeros((CONV_HALO, CONV_CH), F32)

    ext_ref[CONV_HALO:CONV_HALO + ts, :] = u_ref[...]
    cb, g, b = p_ref[0:1, :], p_ref[1:2, :], p_ref[2:3, :]
    off = CONV_HALO - CONV_STATE
    for r in range(1, SUBLANES):
        sh_ref[r - 1] = ext_ref[r:r + sh_ref.shape[1], :]

    for r0 in range(0, ts, CONV_ROWS):
        acc = jnp.zeros((CONV_ROWS, CONV_CH), F32)
        for k in range(CONV_K):
            a, r = divmod(k + off, SUBLANES)
            lo = r0 + SUBLANES * a
            src = ext_ref[lo:lo + CONV_ROWS, :] if r == 0 else sh_ref[r - 1, lo:lo + CONV_ROWS, :]
            acc = acc + w_ref[k:k + 1, :] * src
        o_ref[r0:r0 + CONV_ROWS, :] = _conv_tail(acc, cb, g, b).astype(o_ref.dtype)
    ext_ref[0:CONV_HALO, :] = ext_ref[ts:ts + CONV_HALO, :]


def _conv_prompt(u2d, conv_w, conv_params, *, batch, seq, ts):
    n_s = seq // ts
    return pl.pallas_call(
        functools.partial(_conv_prompt_kernel, ts=ts),
        grid=(batch, n_s),
        in_specs=[pl.BlockSpec((ts, CONV_CH), lambda b, s: (b * n_s + s, 0)),
                  _resident(conv_w.shape), _resident(conv_params.shape)],
        out_specs=pl.BlockSpec((ts, CONV_CH), lambda b, s: (b * n_s + s, 0)),
        out_shape=jax.ShapeDtypeStruct((batch * seq, CONV_CH), BF16),
        scratch_shapes=[pltpu.VMEM((ts + CONV_HALO, CONV_CH), F32),
                        pltpu.VMEM((SUBLANES - 1, ts + CONV_HALO - SUBLANES, CONV_CH), F32)],
        compiler_params=pltpu.CompilerParams(dimension_semantics=("arbitrary", "arbitrary"),
                                             vmem_limit_bytes=32 * MIB),
        name="conv_prompt",
    )(u2d, conv_w, conv_params)


def _conv_sample_kernel(s_ref, u_ref, w_ref, p_ref, o_ref, ns_ref):
    st = s_ref[...]
    u = u_ref[...]
    w_hist = w_ref[0:CONV_STATE, :]
    y = jnp.sum(st * w_hist[None, :, :], axis=1, keepdims=True) + u * w_ref[CONV_STATE:CONV_K, :][None]
    o_ref[...] = _conv_tail(y, p_ref[0:1, :][None], p_ref[1:2, :][None], p_ref[2:3, :][None]).astype(o_ref.dtype)
    ns_ref[:, 0:CONV_STATE - 1, :] = s_ref[:, 1:CONV_STATE, :]
    ns_ref[:, CONV_STATE - 1:CONV_STATE, :] = u


def _conv_sample(state, u, conv_w, conv_params, *, bb):
    nb = state.shape[0]
    return pl.pallas_call(
        _conv_sample_kernel,
        grid=(nb // bb,),
        in_specs=[pl.BlockSpec((bb, CONV_STATE, CONV_CH), lambda i: (i, 0, 0)),
                  pl.BlockSpec((bb, 1, CONV_CH), lambda i: (i, 0, 0)),
                  _resident(conv_w.shape), _resident(conv_params.shape)],
        out_specs=[pl.BlockSpec((bb, 1, CONV_CH), lambda i: (i, 0, 0)),
                   pl.BlockSpec((bb, CONV_STATE, CONV_CH), lambda i: (i, 0, 0))],
        out_shape=[jax.ShapeDtypeStruct((nb, 1, CONV_CH), BF16),
                   jax.ShapeDtypeStruct(state.shape, F32)],
        compiler_params=pltpu.CompilerParams(dimension_semantics=("arbitrary",), vmem_limit_bytes=32 * MIB),
        name="conv_sample",
    )(state, u, conv_w, conv_params)


FLASH_HEAD_GROUP = 4


def _flash_kernel(q_ref, k_ref, vt_ref, o_ref, *state, t):
    i = pl.program_id(1)
    kpos = lax.broadcasted_iota(jnp.int32, (t, t), 0)
    qpos = lax.broadcasted_iota(jnp.int32, (t, t), 1)
    causal = kpos <= qpos

    m_refs, acc_refs = state[:MLA_HEADS], state[MLA_HEADS:]
    for h in range(MLA_HEADS):
        m_refs[h][...] = jnp.full((1, t), NEG_BIG, F32)
        acc_refs[h][...] = jnp.zeros((V_BLOCK, t), F32)

    def kv_tile(j, masked):
        rows = pl.ds(pl.multiple_of(j * t, t), t)
        for g in range(0, MLA_HEADS, FLASH_HEAD_GROUP):
            heads = range(g, g + FLASH_HEAD_GROUP)
            hs = {h: slice(h * HEAD_BLOCK, (h + 1) * HEAD_BLOCK) for h in heads}
            vs = {h: slice(h * V_BLOCK, (h + 1) * V_BLOCK) for h in heads}
            sts = {h: _dot_nt(k_ref[rows, hs[h]], q_ref[:, hs[h]]) for h in heads}
            pts, alphas = {}, {}
            for h in heads:
                st = jnp.where(causal, sts[h], NEG_BIG) if masked else sts[h]
                m_old = m_refs[h][...]
                m_new = jnp.maximum(m_old, jnp.max(st, axis=0, keepdims=True))
                m_refs[h][...] = m_new
                alphas[h] = jnp.exp2(m_old - m_new)
                pts[h] = jnp.exp2(st - m_new).astype(BF16)
            for h in heads:
                acc_refs[h][...] = alphas[h] * acc_refs[h][...] + _dot(vt_ref[j, vs[h], :], pts[h])

    def off_diag(j, carry):
        kv_tile(j, False)
        return carry

    lax.fori_loop(0, i, off_diag, 0)
    kv_tile(i, True)

    def finish(h):
        acc = acc_refs[h][...]
        return acc[0:MLA_HEAD_DIM, :] / acc[V_ONES_ROW:V_ONES_ROW + 1, :]

    for pair in range(MLA_HEADS // 2):
        o_t = jnp.concatenate([finish(2 * pair), finish(2 * pair + 1)], axis=0)
        o_ref[:, pair * LANES:(pair + 1) * LANES] = o_t.T.astype(o_ref.dtype)


def _flash(q, k, vt, *, batch, seq, t):
    n_t = seq // t
    return pl.pallas_call(
        functools.partial(_flash_kernel, t=t),
        grid=(batch, n_t),
        in_specs=[pl.BlockSpec((t, q.shape[1]), lambda b, i: (b * n_t + i, 0)),
                  pl.BlockSpec((seq, k.shape[1]), lambda b, i: (b, 0)),
                  pl.BlockSpec((n_t, vt.shape[1], t), lambda b, i: (b, 0, 0))],
        out_specs=pl.BlockSpec((t, ATTN_WIDTH), lambda b, i: (b * n_t + i, 0)),
        out_shape=jax.ShapeDtypeStruct((batch * seq, ATTN_WIDTH), BF16),
        scratch_shapes=[pltpu.VMEM((1, t), F32)] * MLA_HEADS + [pltpu.VMEM((V_BLOCK, t), F32)] * MLA_HEADS,
        compiler_params=pltpu.CompilerParams(dimension_semantics=("arbitrary", "arbitrary"),
                                             vmem_limit_bytes=48 * MIB),
        name="mla_prompt_flash",
    )(q, k, vt)


def _memkv_kernel(m_ref, w_ref, k_ref, v_ref):
    kv = _dot(m_ref[...].astype(BF16), w_ref[...])
    k_ref[...] = kv[:, :D_MODEL]
    v_ref[...] = kv[:, D_MODEL:]


def _memkv(mem2d, w_xkv, *, tm):
    n = mem2d.shape[0]
    row = pl.BlockSpec((tm, D_MODEL), lambda i: (i, 0))
    return pl.pallas_call(
        _memkv_kernel,
        grid=(n // tm,),
        in_specs=[row, _resident(w_xkv.shape)],
        out_specs=[row, row],
        out_shape=[jax.ShapeDtypeStruct((n, D_MODEL), F32)] * 2,
        compiler_params=pltpu.CompilerParams(dimension_semantics=("arbitrary",), vmem_limit_bytes=40 * MIB),
        name="mem_kv",
    )(mem2d, w_xkv)


FFN_CHUNK = 1024


def _mix_ln1_q(xs, mixes, ln_ref, wxq_ref):
    x1s = [_layer_norm(ALPHA * x + mix, ln_ref[0:1, :], ln_ref[1:2, :]) for x, mix in zip(xs, mixes)]
    return x1s, [_dot(x1.astype(BF16), wxq_ref[...]) for x1 in x1s]


def _xo_ffn(x1s, o_heads_per_slab, ln_ref, wxo_ref, wup_ref, wdown_ref):
    atts = []
    for o_heads in o_heads_per_slab:
        att = None
        for h, oh in enumerate(o_heads):
            t = _dot(oh.astype(BF16), wxo_ref[h * MEM_HEAD_DIM:(h + 1) * MEM_HEAD_DIM, :])
            att = t if att is None else att + t
        atts.append(att)
    x2s = [_layer_norm(ALPHA * x1 + att, ln_ref[2:3, :], ln_ref[3:4, :]) for x1, att in zip(x1s, atts)]
    x2bs = [x2.astype(BF16) for x2 in x2s]
    ffns = [None] * len(x2s)
    for c in range(D_FF // FFN_CHUNK):
        cs = slice(c * FFN_CHUNK, (c + 1) * FFN_CHUNK)
        hcols = [jnp.maximum(_dot(x2b, wup_ref[:, cs]), 0.0) for x2b in x2bs]
        for n, hcol in enumerate(hcols):
            t = _dot((hcol * hcol).astype(BF16), wdown_ref[cs, :])
            ffns[n] = t if ffns[n] is None else ffns[n] + t
    return [_layer_norm(ALPHA * x2 + ffn, ln_ref[4:5, :], ln_ref[5:6, :]) for x2, ffn in zip(x2s, ffns)]


POST_SLABS = 2


def _post_prompt_kernel(x_ref, a_ref, c_ref, mk_ref, mv_ref, ln_ref,
                        wo_ref, wxq_ref, wxo_ref, wup_ref, wdown_ref, y_ref):
    rows_per_slab = x_ref.shape[0] // POST_SLABS
    slabs = [slice(n * rows_per_slab, (n + 1) * rows_per_slab) for n in range(POST_SLABS)]
    mixes = [_dot(a_ref[r, :], wo_ref[0:ATTN_WIDTH, :]) + _dot(c_ref[r, :], wo_ref[ATTN_WIDTH:, :]) for r in slabs]
    x1s, q2s = _mix_ln1_q([x_ref[r, :] for r in slabs], mixes, ln_ref, wxq_ref)
    o_heads_per_slab = [[] for _ in slabs]
    for h in range(MEM_HEADS):
        hs = slice(h * MEM_HEAD_DIM, (h + 1) * MEM_HEAD_DIM)
        mk_h, mv_h = mk_ref[:, hs].astype(BF16), mv_ref[:, hs].astype(BF16)
        ss = [_dot_nt((q2[:, hs] * MEM_QSCALE).astype(BF16), mk_h) for q2 in q2s]
        ps = [jnp.exp2(s - jnp.max(s, axis=-1, keepdims=True)) for s in ss]
        for n, p in enumerate(ps):
            inv_l = 1.0 / jnp.sum(p, axis=-1, keepdims=True)
            o_heads_per_slab[n].append(_dot(p.astype(BF16), mv_h) * inv_l)
    ys = _xo_ffn(x1s, o_heads_per_slab, ln_ref, wxo_ref, wup_ref, wdown_ref)
    for r, y in zip(slabs, ys):
        y_ref[r, :] = y


def _post_prompt(x2d, attn, conv, mk, mv, ln, w_o, w_xq, w_xo, w_up, w_down, *, batch, seq, tm, mem_tokens):
    n_t = seq // tm
    row = lambda width: pl.BlockSpec((tm, width), lambda b, t: (b * n_t + t, 0))
    mem = pl.BlockSpec((mem_tokens, D_MODEL), lambda b, t: (b, 0))
    weights = (ln, w_o, w_xq, w_xo, w_up, w_down)
    return pl.pallas_call(
        _post_prompt_kernel,
        grid=(batch, n_t),
        in_specs=[row(D_MODEL), row(ATTN_WIDTH), row(CONV_CH), mem, mem] + [_resident(w.shape) for w in weights],
        out_specs=row(D_MODEL),
        out_shape=jax.ShapeDtypeStruct((batch * seq, D_MODEL), F32),
        compiler_params=pltpu.CompilerParams(dimension_semantics=("arbitrary", "arbitrary"),
                                             vmem_limit_bytes=56 * MIB),
        name="post_prompt",
    )(x2d, attn, conv, mk, mv, *weights)


def _post1_sample_kernel(x_ref, ol_ref, c_ref, wuv_ref, ln_ref, wo_ref, wxq_ref, x1_ref, q2_ref):
    mix = _dot(c_ref[...], wo_ref[ATTN_WIDTH:, :])
    for h in range(MLA_HEADS):
        a_h = _dot(ol_ref[:, h * KV_LORA:(h + 1) * KV_LORA].astype(BF16), wuv_ref[h])
        mix = mix + _dot(a_h.astype(BF16), wo_ref[h * MLA_HEAD_DIM:(h + 1) * MLA_HEAD_DIM, :])
    (x1,), (q2,) = _mix_ln1_q([x_ref[...]], [mix], ln_ref, wxq_ref)
    x1_ref[...] = x1
    q2_ref[...] = q2 * MEM_QSCALE


def _post1_sample(x2d, o_lat, conv, w_uv, ln, w_o, w_xq):
    args = (x2d, o_lat, conv, w_uv, ln, w_o, w_xq)
    n = x2d.shape[0]
    return pl.pallas_call(
        _post1_sample_kernel,
        grid=(1,),
        in_specs=[_resident(a.shape) for a in args],
        out_specs=[pl.BlockSpec((n, D_MODEL), lambda i: (0, 0))] * 2,
        out_shape=[jax.ShapeDtypeStruct((n, D_MODEL), F32)] * 2,
        compiler_params=pltpu.CompilerParams(dimension_semantics=("arbitrary",), vmem_limit_bytes=32 * MIB),
        name="post1_sample",
    )(*args)


def _cross_sample_kernel(q_ref, k_ref, v_ref, o_ref, *, bb):
    for i in range(bb):
        q = q_ref[i]
        s = jnp.sum(k_ref[i] * q[None], axis=-1, keepdims=True)
        p = jnp.exp2(s - jnp.max(s, axis=0, keepdims=True))
        l = jnp.sum(p, axis=0)
        o_ref[i] = jnp.sum(p * v_ref[i], axis=0) / l


def _cross_sample(q2, mem_k, mem_v, *, bb):
    nb, m_tok = mem_k.shape[:2]
    q_spec = pl.BlockSpec((bb, MEM_HEADS, MEM_HEAD_DIM), lambda i: (i, 0, 0))
    kv_spec = pl.BlockSpec((bb, m_tok, MEM_HEADS, MEM_HEAD_DIM), lambda i: (i, 0, 0, 0))
    return pl.pallas_call(
        functools.partial(_cross_sample_kernel, bb=bb),
        grid=(nb // bb,),
        in_specs=[q_spec, kv_spec, kv_spec],
        out_specs=q_spec,
        out_shape=jax.ShapeDtypeStruct((nb, MEM_HEADS, MEM_HEAD_DIM), F32),
        compiler_params=pltpu.CompilerParams(dimension_semantics=("arbitrary",), vmem_limit_bytes=40 * MIB),
        name="cross_sample",
    )(q2, mem_k, mem_v)


def _post2_sample_kernel(x1_ref, o_ref, ln_ref, wxo_ref, wup_ref, wdown_ref, y_ref):
    o_heads = [o_ref[:, h * MEM_HEAD_DIM:(h + 1) * MEM_HEAD_DIM] for h in range(MEM_HEADS)]
    (y_ref[...],) = _xo_ffn([x1_ref[...]], [o_heads], ln_ref, wxo_ref, wup_ref, wdown_ref)


def _post2_sample(x1, o, ln, w_xo, w_up, w_down):
    args = (x1, o, ln, w_xo, w_up, w_down)
    n = x1.shape[0]
    return pl.pallas_call(
        _post2_sample_kernel,
        grid=(1,),
        in_specs=[_resident(a.shape) for a in args],
        out_specs=pl.BlockSpec((n, D_MODEL), lambda i: (0, 0)),
        out_shape=jax.ShapeDtypeStruct((n, D_MODEL), F32),
        compiler_params=pltpu.CompilerParams(dimension_semantics=("arbitrary",), vmem_limit_bytes=40 * MIB),
        name="post2_sample",
    )(*args)


DECODE_CHUNK_PAGES = 8
DECODE_ISSUE_UNROLL = 8


def _decode_kernel(pt_ref, ql_ref, qr_ref, cn_ref, krn_ref, cpool_ref, rpool_ref, o_ref,
                   cbuf, rbuf, sem, *, n_pages, page_size):
    b = pl.program_id(0)
    nb = pl.num_programs(0)

    def start_all(bi, slot):
        def body(p, carry):
            pg = pt_ref[bi * n_pages + p]
            pltpu.make_async_copy(cpool_ref.at[0, pg], cbuf.at[slot, p], sem.at[slot, 0]).start()
            pltpu.make_async_copy(rpool_ref.at[pg], rbuf.at[slot, p], sem.at[slot, 1]).start()
            return carry
        lax.fori_loop(0, n_pages, body, 0, unroll=DECODE_ISSUE_UNROLL)

    def wait_all(slot):
        pltpu.make_async_copy(cpool_ref.at[0, pl.ds(0, n_pages)], cbuf.at[slot], sem.at[slot, 0]).wait()
        pltpu.make_async_copy(rpool_ref.at[pl.ds(0, n_pages)], rbuf.at[slot], sem.at[slot, 1]).wait()

    slot = b % 2

    @pl.when(b == 0)
    def _():
        start_all(0, 0)

    @pl.when(b + 1 < nb)
    def _():
        start_all(b + 1, 1 - slot)

    wait_all(slot)

    ql32 = ql_ref[0]
    qr32 = qr_ref[0]
    c_new = cn_ref[0]
    kr_new = krn_ref[0]
    ql, qr = ql32.astype(BF16), qr32.astype(BF16)

    def chunk(c):
        pages = range(c * DECODE_CHUNK_PAGES, (c + 1) * DECODE_CHUNK_PAGES)
        cb = cbuf[slot, pages.start:pages.stop].reshape(DECODE_CHUNK_PAGES * page_size, KV_LORA).astype(BF16)
        rb = jnp.concatenate([rbuf[slot, p] for p in pages], axis=1).astype(BF16)
        return cb, _dot_nt(ql, cb) + _dot(qr, rb)

    m = (jnp.sum(ql32 * c_new, axis=-1, keepdims=True)
         + jnp.sum(qr32 * kr_new, axis=-1, keepdims=True))
    l = jnp.ones_like(m)
    o = jnp.broadcast_to(c_new, (MLA_HEADS, KV_LORA))
    n_chunks = n_pages // DECODE_CHUNK_PAGES
    nxt = chunk(0)
    for c in range(n_chunks):
        cb, s = nxt
        if c + 1 < n_chunks:
            nxt = chunk(c + 1)
        m_new = jnp.maximum(m, jnp.max(s, axis=-1, keepdims=True))
        alpha = jnp.exp2(m - m_new)
        p = jnp.exp2(s - m_new)
        l = alpha * l + jnp.sum(p, axis=-1, keepdims=True)
        o = alpha * o + _dot(p.astype(BF16), cb)
        m = m_new
    o_ref[0] = o / l


def _decode(page_table, q_lat, q_rope, c_new, kr_new, cpool, rpool):
    nb, n_pages = page_table.shape
    page_size = cpool.shape[2]
    assert n_pages % DECODE_CHUNK_PAGES == 0 and n_pages % DECODE_ISSUE_UNROLL == 0
    per_seq = lambda *tail: pl.BlockSpec((1,) + tail, lambda b, pt: (b, 0, 0))
    grid_spec = pltpu.PrefetchScalarGridSpec(
        num_scalar_prefetch=1,
        grid=(nb,),
        in_specs=[per_seq(MLA_HEADS, KV_LORA), per_seq(MLA_HEADS, ROPE_DIM), per_seq(1, KV_LORA), per_seq(1, ROPE_DIM),
                  pl.BlockSpec(memory_space=pl.ANY),
                  pl.BlockSpec(memory_space=pl.ANY)],
        out_specs=per_seq(MLA_HEADS, KV_LORA),
        scratch_shapes=[pltpu.VMEM((2, n_pages, page_size, KV_LORA), F32),
                        pltpu.VMEM((2, n_pages, ROPE_DIM, page_size), F32),
                        pltpu.SemaphoreType.DMA((2, 2))],
    )
    return pl.pallas_call(
        functools.partial(_decode_kernel, n_pages=n_pages, page_size=page_size),
        grid_spec=grid_spec,
        out_shape=jax.ShapeDtypeStruct((nb, MLA_HEADS, KV_LORA), F32),
        compiler_params=pltpu.CompilerParams(dimension_semantics=("arbitrary",), vmem_limit_bytes=48 * MIB),
        name="mla_sample_paged",
    )(page_table.reshape(-1), q_lat, q_rope, c_new, kr_new, cpool, rpool)


def _rope_tables(pos):
    inv_freq = jnp.exp(-math.log(ROPE_BASE) * jnp.arange(HALF_ROPE, dtype=F32) / HALF_ROPE)
    ang = pos.astype(F32)[:, None] * inv_freq[None, :]
    cos, sin = jnp.cos(ang), jnp.sin(ang)
    n = pos.shape[0]
    zeros = jnp.zeros((n, LANES - HALF_ROPE), F32)
    cos_t = jnp.concatenate([cos, cos, jnp.ones((n, LANES - ROPE_DIM), F32)], axis=1)
    sin_a = jnp.concatenate([-sin, zeros], axis=1)
    sin_b = jnp.concatenate([jnp.zeros((n, HALF_ROPE), F32), sin, zeros[:, HALF_ROPE:]], axis=1)
    return jnp.concatenate([cos_t, sin_a, sin_b], axis=1)


def _pad_cols(w, left, right):
    return jnp.pad(w, [(0, 0)] * (w.ndim - 1) + [(left, right)])


def _layout_w_in(w_in):
    wq = w_in[:, :Q_COLS].reshape(D_MODEL, MLA_HEADS, MLA_HEAD_DIM + ROPE_DIM)
    wq = jnp.concatenate([wq[..., MLA_HEAD_DIM:], wq[..., :MLA_HEAD_DIM]], axis=-1)
    wq = _pad_cols(wq, 0, HEAD_BLOCK - (MLA_HEAD_DIM + ROPE_DIM)).reshape(D_MODEL, MLA_HEADS * HEAD_BLOCK)
    c0 = Q_COLS
    wc = w_in[:, c0:c0 + KV_LORA]
    wkr = _pad_cols(w_in[:, c0 + KV_LORA:c0 + KV_LORA + ROPE_DIM], 0, HEAD_BLOCK - ROPE_DIM)
    wg = w_in[:, c0 + KV_LORA + ROPE_DIM:]
    return jnp.concatenate([wq, wkr, wc, wg], axis=1).astype(BF16)


def kernel(x_prompt, x_sample, cache_ckv, cache_krope, state_conv, cache_mem_k, cache_mem_v, page_table, mem_prompt, w_in, kv_norm_g, w_uk, w_uv, conv_w, conv_b, conv_ln_g, conv_ln_b, w_o, ln1_g, ln1_b, w_xq, w_xk, w_xv, w_xo, ln2_g, ln2_b, w_up, w_down, ln3_g, ln3_b):
    assert w_in.shape[0] == DEPTH == 1
    B, S, _ = x_prompt.shape
    Bd, T, _ = x_sample.shape
    assert T == 1
    n_pool, page_size, _ = cache_ckv.shape[1:]
    n_pages = page_table.shape[1]
    past_len = n_pages * page_size
    mem_tokens = mem_prompt.shape[1]
    l = 0

    w_all = _layout_w_in(w_in[l])
    g_kv = kv_norm_g[l][None, :]
    wuk_blk = _pad_cols(w_uk[l], ROPE_DIM, HEAD_BLOCK - ROPE_DIM - MLA_HEAD_DIM
                        ).reshape(KV_LORA, MLA_HEADS * HEAD_BLOCK).astype(BF16)
    wuv_t = jnp.pad(jnp.transpose(w_uv[l], (1, 2, 0)), ((0, 0), (0, V_BLOCK - MLA_HEAD_DIM), (0, 0))
                    ).reshape(MLA_HEADS * V_BLOCK, KV_LORA).astype(BF16)
    wq2l = jnp.pad(jnp.transpose(w_uk[l], (1, 2, 0)),
                   ((0, 0), (ROPE_DIM, HEAD_BLOCK - ROPE_DIM - MLA_HEAD_DIM), (0, 0))).astype(BF16)
    wuv_h = jnp.transpose(w_uv[l], (1, 0, 2)).astype(BF16)
    conv_params = jnp.stack([conv_b[l], conv_ln_g[l], conv_ln_b[l]])
    ln = jnp.stack([ln1_g[l], ln1_b[l], ln2_g[l], ln2_b[l], ln3_g[l], ln3_b[l],
                    jnp.zeros_like(ln1_g[l]), jnp.zeros_like(ln1_g[l])])
    wo_b, wxq_b, wxo_b = w_o[l].astype(BF16), w_xq[l].astype(BF16), w_xo[l].astype(BF16)
    wup_b, wdown_b = w_up[l].astype(BF16), w_down[l].astype(BF16)
    w_xkv = jnp.concatenate([w_xk[l], w_xv[l]], axis=1).astype(BF16)

    xp = x_prompt.reshape(B * S, D_MODEL)
    tab_p = _rope_tables(jnp.arange(S, dtype=jnp.int32))
    attn_tile = 512
    q_p, ckv_p, kr_p, u_p, k_p, vt_p = _inproj(xp, w_all, tab_p, g_kv, (wuk_blk, wuv_t), tm=attn_tile, sample=False)
    attn_p = _flash(q_p, k_p, vt_p, batch=B, seq=S, t=attn_tile)
    conv_p = _conv_prompt(u_p, conv_w[l], conv_params, batch=B, seq=S, ts=512)
    mk_p, mv_p = _memkv(mem_prompt.reshape(B * mem_tokens, D_MODEL), w_xkv, tm=512)
    y_p = _post_prompt(xp, attn_p, conv_p, mk_p, mv_p, ln, wo_b, wxq_b, wxo_b, wup_b, wdown_b,
                       batch=B, seq=S, tm=512, mem_tokens=mem_tokens)

    xs = x_sample.reshape(Bd, D_MODEL)
    tab_s = _rope_tables(jnp.full((Bd,), past_len, dtype=jnp.int32))
    q_s, ckv_s, kr_s, u_s, qlat_s = _inproj(xs, w_all, tab_s, g_kv, (wq2l,), tm=Bd, sample=True)
    q_rope = q_s.reshape(Bd, MLA_HEADS, HEAD_BLOCK)[:, :, :ROPE_DIM]
    rpool = jnp.swapaxes(cache_krope[l], 1, 2)
    o_lat = _decode(page_table, qlat_s.reshape(Bd, MLA_HEADS, KV_LORA), q_rope,
                    ckv_s[:, None, :], kr_s[:, None, :], cache_ckv, rpool)
    conv_s, new_state = _conv_sample(state_conv[l], u_s[:, None, :], conv_w[l], conv_params, bb=16)
    x1_s, q2_s = _post1_sample(xs, o_lat.reshape(Bd, MLA_HEADS * KV_LORA), conv_s.reshape(Bd, CONV_CH),
                               wuv_h, ln, wo_b, wxq_b)
    o_s = _cross_sample(q2_s.reshape(Bd, MEM_HEADS, MEM_HEAD_DIM), cache_mem_k[l], cache_mem_v[l], bb=4)
    y_s = _post2_sample(x1_s, o_s.reshape(Bd, D_MODEL), ln, wxo_b, wup_b, wdown_b)

    mem_shape = (1, B, mem_tokens, MEM_HEADS, MEM_HEAD_DIM)
    return (y_p.reshape(B, S, D_MODEL),
            y_s.reshape(Bd, T, D_MODEL),
            ckv_p.reshape(1, B, S, KV_LORA),
            kr_p.reshape(1, B, S, ROPE_DIM),
            u_p.reshape(B, S, CONV_CH)[None, :, S - CONV_STATE:, :],
            mk_p.reshape(mem_shape),
            mv_p.reshape(mem_shape),
            ckv_s.reshape(1, Bd, T, KV_LORA),
            kr_s.reshape(1, Bd, T, ROPE_DIM),
            new_state[None])
```

```python
import functools
import math

import jax
import jax.numpy as jnp
from jax import lax
from jax.experimental import pallas as pl
from jax.experimental.pallas import tpu as pltpu

D_MODEL = 1024
MLA_HEADS = 8
MLA_HEAD_DIM = 64
ROPE_DIM = 32
KV_LORA = D_MODEL // 4
ATTN_WIDTH = MLA_HEADS * MLA_HEAD_DIM
CONV_CH = D_MODEL - ATTN_WIDTH
CONV_K = 31
CONV_STATE = CONV_K - 1
MEM_HEADS = 4
MEM_HEAD_DIM = D_MODEL // MEM_HEADS
D_FF = 4 * D_MODEL
ROPE_BASE = 10000.0
LN_EPS = 1e-5
DEPTH = 1
ALPHA = (2.0 * DEPTH) ** 0.25
Q_COLS = MLA_HEADS * (MLA_HEAD_DIM + ROPE_DIM)
LOG2E = math.log2(math.e)
MLA_QSCALE = (MLA_HEAD_DIM + ROPE_DIM) ** -0.5 * LOG2E
MEM_QSCALE = MEM_HEAD_DIM ** -0.5 * LOG2E

LANES = 128
SUBLANES = 8
BF16_SUBLANES = 16
HEAD_BLOCK = LANES
HALF_ROPE = ROPE_DIM // 2
NEG_BIG = -1e30
MIB = 1024 * 1024

F32 = jnp.float32
BF16 = jnp.bfloat16

QK_COLS = (MLA_HEADS + 1) * HEAD_BLOCK
IN_COLS_PADDED = QK_COLS + KV_LORA + 2 * CONV_CH


def _dot(a, b):
    return jnp.dot(a, b, preferred_element_type=F32)


def _dot_nt(a, b):
    return lax.dot_general(a, b, (((1,), (1,)), ((), ())), preferred_element_type=F32)


def _layer_norm(x, g, b):
    mu = jnp.mean(x, axis=-1, keepdims=True)
    xc = x - mu
    var = jnp.mean(xc * xc, axis=-1, keepdims=True)
    return xc * lax.rsqrt(var + LN_EPS) * g + b


def _resident(shape):
    nd = len(shape)
    return pl.BlockSpec(shape, lambda *_: (0,) * nd, pipeline_mode=pl.Buffered(1))


def _rope_block(zb, cos_t, sin_a, sin_b):
    return zb * cos_t + pltpu.roll(zb, LANES - HALF_ROPE, 1) * sin_a + pltpu.roll(zb, HALF_ROPE, 1) * sin_b


def _inproj_common(x_ref, w_ref, tab_ref, g_ref):
    x = x_ref[...].astype(BF16)
    tab = tab_ref[...]
    cos_t, sin_a, sin_b = tab[:, :LANES], tab[:, LANES:2 * LANES], tab[:, 2 * LANES:]
    zq = _dot(x, w_ref[:, :QK_COLS])
    blocks = [_rope_block(zq[:, h * HEAD_BLOCK:(h + 1) * HEAD_BLOCK], cos_t, sin_a, sin_b)
              for h in range(MLA_HEADS + 1)]
    zc = _dot(x, w_ref[:, QK_COLS:QK_COLS + KV_LORA])
    c = zc * lax.rsqrt(jnp.mean(zc * zc, axis=-1, keepdims=True) + LN_EPS) * g_ref[...]
    zg = _dot(x, w_ref[:, QK_COLS + KV_LORA:])
    u = zg[:, :CONV_CH] * jax.nn.sigmoid(zg[:, CONV_CH:])
    return blocks, c, u


V_ONES_ROW = MLA_HEAD_DIM
V_BLOCK = -(-(MLA_HEAD_DIM + 1) // BF16_SUBLANES) * BF16_SUBLANES


def _inproj_prompt_kernel(x_ref, w_ref, tab_ref, g_ref, wuk_ref, wuvt_ref,
                          q_ref, ckv_ref, kr_ref, u_ref, k_ref, vt_ref):
    blocks, c, u = _inproj_common(x_ref, w_ref, tab_ref, g_ref)
    kr = blocks[MLA_HEADS]
    for h in range(MLA_HEADS):
        q_ref[:, h * HEAD_BLOCK:(h + 1) * HEAD_BLOCK] = (blocks[h] * MLA_QSCALE).astype(BF16)
    kr_ref[0] = kr.T[0:ROPE_DIM, :]
    ckv_ref[...] = c
    u_ref[...] = u
    cb = c.astype(BF16)
    kn = _dot(cb, wuk_ref[...])
    for h in range(MLA_HEADS):
        k_ref[:, h * HEAD_BLOCK:(h + 1) * HEAD_BLOCK] = (kn[:, h * HEAD_BLOCK:(h + 1) * HEAD_BLOCK] + kr).astype(BF16)
    vt = _dot_nt(wuvt_ref[...], cb)
    row = lax.broadcasted_iota(jnp.int32, vt.shape, 0)
    vt_ref[0] = jnp.where(row % V_BLOCK == V_ONES_ROW, 1.0, vt).astype(BF16)


def _inproj_sample_kernel(x_ref, w_ref, tab_ref, g_ref, wq2l_ref,
                          q_ref, ckv_ref, kr_ref, u_ref, qlat_ref):
    blocks, c, u = _inproj_common(x_ref, w_ref, tab_ref, g_ref)
    kr = blocks[MLA_HEADS]
    for h in range(MLA_HEADS):
        qb = blocks[h] * MLA_QSCALE
        q_ref[:, h * HEAD_BLOCK:(h + 1) * HEAD_BLOCK] = qb
        qlat_ref[:, h * KV_LORA:(h + 1) * KV_LORA] = _dot(qb.astype(BF16), wq2l_ref[h])
    kr_ref[...] = kr[:, :ROPE_DIM]
    ckv_ref[...] = c
    u_ref[...] = u


def _inproj(x2d, w_all, tab, g, extra_ws, *, tm, sample):
    n = x2d.shape[0]
    n_tiles = n // tm
    n_tab_blocks = tab.shape[0] // tm
    row = lambda width: pl.BlockSpec((tm, width), lambda i: (i, 0))
    in_specs = [row(D_MODEL), _resident(w_all.shape),
                pl.BlockSpec((tm, 3 * LANES), lambda i: (i % n_tab_blocks, 0)),
                _resident(g.shape)] + [_resident(w.shape) for w in extra_ws]
    widths = [(D_MODEL, F32 if sample else BF16), (KV_LORA, F32), (ROPE_DIM, F32), (CONV_CH, F32)]
    widths.append((MLA_HEADS * KV_LORA, F32) if sample else (MLA_HEADS * HEAD_BLOCK, BF16))
    out_specs = [row(w) for w, _ in widths]
    out_shape = [jax.ShapeDtypeStruct((n, w), dt) for w, dt in widths]
    if not sample:
        out_specs.append(pl.BlockSpec((1, MLA_HEADS * V_BLOCK, tm), lambda i: (i, 0, 0)))
        out_shape.append(jax.ShapeDtypeStruct((n_tiles, MLA_HEADS * V_BLOCK, tm), BF16))
        out_specs[2] = pl.BlockSpec((1, ROPE_DIM, tm), lambda i: (i // n_tab_blocks, 0, i % n_tab_blocks))
        out_shape[2] = jax.ShapeDtypeStruct((n_tiles // n_tab_blocks, ROPE_DIM, tab.shape[0]), F32)
    return pl.pallas_call(
        _inproj_sample_kernel if sample else _inproj_prompt_kernel,
        grid=(n_tiles,),
        in_specs=in_specs,
        out_specs=out_specs,
        out_shape=out_shape,
        compiler_params=pltpu.CompilerParams(dimension_semantics=("arbitrary",), vmem_limit_bytes=48 * MIB),
        name="inproj_sample" if sample else "inproj_prompt",
    )(x2d, w_all, tab, g, *extra_ws)


CONV_HALO = 32
CONV_ROWS = 64


def _conv_tail(y, cb, g, b):
    t = _layer_norm(y + cb, g, b)
    return t * jax.nn.sigmoid(t)


def _conv_prompt_kernel(u_ref, w_ref, p_ref, o_ref, ext_ref, sh_ref, *, ts):
    @pl.when(pl.program_id(1) == 0)
    def _():
        ext_ref[0:CONV_HALO, :] = jnp.zeros((CONV_HALO, CONV_CH), F32)

    ext_ref[CONV_HALO:CONV_HALO + ts, :] = u_ref[...]
    cb, g, b = p_ref[0:1, :], p_ref[1:2, :], p_ref[2:3, :]
    off = CONV_HALO - CONV_STATE
    for r in range(1, SUBLANES):
        sh_ref[r - 1] = ext_ref[r:r + sh_ref.shape[1], :]

    for r0 in range(0, ts, CONV_ROWS):
        acc = jnp.zeros((CONV_ROWS, CONV_CH), F32)
        for k in range(CONV_K):
            a, r = divmod(k + off, SUBLANES)
            lo = r0 + SUBLANES * a
            src = ext_ref[lo:lo + CONV_ROWS, :] if r == 0 else sh_ref[r - 1, lo:lo + CONV_ROWS, :]
            acc = acc + w_ref[k:k + 1, :] * src
        o_ref[r0:r0 + CONV_ROWS, :] = _conv_tail(acc, cb, g, b).astype(o_ref.dtype)
    ext_ref[0:CONV_HALO, :] = ext_ref[ts:ts + CONV_HALO, :]


def _conv_prompt(u2d, conv_w, conv_params, *, batch, seq, ts):
    n_s = seq // ts
    return pl.pallas_call(
        functools.partial(_conv_prompt_kernel, ts=ts),
        grid=(batch, n_s),
        in_specs=[pl.BlockSpec((ts, CONV_CH), lambda b, s: (b * n_s + s, 0)),
                  _resident(conv_w.shape), _resident(conv_params.shape)],
        out_specs=pl.BlockSpec((ts, CONV_CH), lambda b, s: (b * n_s + s, 0)),
        out_shape=jax.ShapeDtypeStruct((batch * seq, CONV_CH), BF16),
        scratch_shapes=[pltpu.VMEM((ts + CONV_HALO, CONV_CH), F32),
                        pltpu.VMEM((SUBLANES - 1, ts + CONV_HALO - SUBLANES, CONV_CH), F32)],
        compiler_params=pltpu.CompilerParams(dimension_semantics=("arbitrary", "arbitrary"),
                                             vmem_limit_bytes=32 * MIB),
        name="conv_prompt",
    )(u2d, conv_w, conv_params)


def _conv_sample_kernel(s_ref, u_ref, w_ref, p_ref, o_ref, ns_ref):
    st = s_ref[...]
    u = u_ref[...]
    w_hist = w_ref[0:CONV_STATE, :]
    y = jnp.sum(st * w_hist[None, :, :], axis=1, keepdims=True) + u * w_ref[CONV_STATE:CONV_K, :][None]
    o_ref[...] = _conv_tail(y, p_ref[0:1, :][None], p_ref[1:2, :][None], p_ref[2:3, :][None]).astype(o_ref.dtype)
    ns_ref[:, 0:CONV_STATE - 1, :] = s_ref[:, 1:CONV_STATE, :]
    ns_ref[:, CONV_STATE - 1:CONV_STATE, :] = u


def _conv_sample(state, u, conv_w, conv_params, *, bb):
    nb = state.shape[0]
    return pl.pallas_call(
        _conv_sample_kernel,
        grid=(nb // bb,),
        in_specs=[pl.BlockSpec((bb, CONV_STATE, CONV_CH), lambda i: (i, 0, 0)),
                  pl.BlockSpec((bb, 1, CONV_CH), lambda i: (i, 0, 0)),
                  _resident(conv_w.shape), _resident(conv_params.shape)],
        out_specs=[pl.BlockSpec((bb, 1, CONV_CH), lambda i: (i, 0, 0)),
                   pl.BlockSpec((bb, CONV_STATE, CONV_CH), lambda i: (i, 0, 0))],
        out_shape=[jax.ShapeDtypeStruct((nb, 1, CONV_CH), BF16),
                   jax.ShapeDtypeStruct(state.shape, F32)],
        compiler_params=pltpu.CompilerParams(dimension_semantics=("arbitrary",), vmem_limit_bytes=32 * MIB),
        name="conv_sample",
    )(state, u, conv_w, conv_params)


FLASH_HEAD_GROUP = 4


def _flash_kernel(q_ref, k_ref, vt_ref, o_ref, *state, t):
    i = pl.program_id(1)
    kpos = lax.broadcasted_iota(jnp.int32, (t, t), 0)
    qpos = lax.broadcasted_iota(jnp.int32, (t, t), 1)
    causal = kpos <= qpos

    m_refs, acc_refs = state[:MLA_HEADS], state[MLA_HEADS:]
    for h in range(MLA_HEADS):
        m_refs[h][...] = jnp.full((1, t), NEG_BIG, F32)
        acc_refs[h][...] = jnp.zeros((V_BLOCK, t), F32)

    def kv_tile(j, masked):
        rows = pl.ds(pl.multiple_of(j * t, t), t)
        for g in range(0, MLA_HEADS, FLASH_HEAD_GROUP):
            heads = range(g, g + FLASH_HEAD_GROUP)
            hs = {h: slice(h * HEAD_BLOCK, (h + 1) * HEAD_BLOCK) for h in heads}
            vs = {h: slice(h * V_BLOCK, (h + 1) * V_BLOCK) for h in heads}
            sts = {h: _dot_nt(k_ref[rows, hs[h]], q_ref[:, hs[h]]) for h in heads}
            pts, alphas = {}, {}
            for h in heads:
                st = jnp.where(causal, sts[h], NEG_BIG) if masked else sts[h]
                m_old = m_refs[h][...]
                m_new = jnp.maximum(m_old, jnp.max(st, axis=0, keepdims=True))
                m_refs[h][...] = m_new
                alphas[h] = jnp.exp2(m_old - m_new)
                pts[h] = jnp.exp2(st - m_new).astype(BF16)
            for h in heads:
                acc_refs[h][...] = alphas[h] * acc_refs[h][...] + _dot(vt_ref[j, vs[h], :], pts[h])

    def off_diag(j, carry):
        kv_tile(j, False)
        return carry

    lax.fori_loop(0, i, off_diag, 0)
    kv_tile(i, True)

    def finish(h):
        acc = acc_refs[h][...]
        return acc[0:MLA_HEAD_DIM, :] / acc[V_ONES_ROW:V_ONES_ROW + 1, :]

    for pair in range(MLA_HEADS // 2):
        o_t = jnp.concatenate([finish(2 * pair), finish(2 * pair + 1)], axis=0)
        o_ref[:, pair * LANES:(pair + 1) * LANES] = o_t.T.astype(o_ref.dtype)


def _flash(q, k, vt, *, batch, seq, t):
    n_t = seq // t
    return pl.pallas_call(
        functools.partial(_flash_kernel, t=t),
        grid=(batch, n_t),
        in_specs=[pl.BlockSpec((t, q.shape[1]), lambda b, i: (b * n_t + i, 0)),
                  pl.BlockSpec((seq, k.shape[1]), lambda b, i: (b, 0)),
                  pl.BlockSpec((n_t, vt.shape[1], t), lambda b, i: (b, 0, 0))],
        out_specs=pl.BlockSpec((t, ATTN_WIDTH), lambda b, i: (b * n_t + i, 0)),
        out_shape=jax.ShapeDtypeStruct((batch * seq, ATTN_WIDTH), BF16),
        scratch_shapes=[pltpu.VMEM((1, t), F32)] * MLA_HEADS + [pltpu.VMEM((V_BLOCK, t), F32)] * MLA_HEADS,
        compiler_params=pltpu.CompilerParams(dimension_semantics=("arbitrary", "arbitrary"),
                                             vmem_limit_bytes=48 * MIB),
        name="mla_prompt_flash",
    )(q, k, vt)


def _memkv_kernel(m_ref, w_ref, k_ref, v_ref, kvb_ref):
    kv = _dot(m_ref[...].astype(BF16), w_ref[...])
    kvb_ref[...] = kv.astype(BF16)
    for h in range(MEM_HEADS):
        k_ref[:, h, :] = kv[:, h * MEM_HEAD_DIM:(h + 1) * MEM_HEAD_DIM]
        v_ref[:, h, :] = kv[:, D_MODEL + h * MEM_HEAD_DIM:D_MODEL + (h + 1) * MEM_HEAD_DIM]


def _memkv(mem2d, w_xkv, *, tm):
    n = mem2d.shape[0]
    row = pl.BlockSpec((tm, D_MODEL), lambda i: (i, 0))
    cache = pl.BlockSpec((tm, MEM_HEADS, MEM_HEAD_DIM), lambda i: (i, 0, 0))
    return pl.pallas_call(
        _memkv_kernel,
        grid=(n // tm,),
        in_specs=[row, _resident(w_xkv.shape)],
        out_specs=[cache, cache, pl.BlockSpec((tm, 2 * D_MODEL), lambda i: (i, 0))],
        out_shape=[jax.ShapeDtypeStruct((n, MEM_HEADS, MEM_HEAD_DIM), F32)] * 2
        + [jax.ShapeDtypeStruct((n, 2 * D_MODEL), BF16)],
        compiler_params=pltpu.CompilerParams(dimension_semantics=("arbitrary",), vmem_limit_bytes=40 * MIB),
        name="mem_kv",
    )(mem2d, w_xkv)


FFN_CHUNK = 1024


def _mix_ln1_q(xs, mixes, ln_ref, wxq_ref):
    x1s = [_layer_norm(ALPHA * x + mix, ln_ref[0:1, :], ln_ref[1:2, :]) for x, mix in zip(xs, mixes)]
    return x1s, [_dot(x1.astype(BF16), wxq_ref[...]) for x1 in x1s]


def _xo_ffn(x1s, o_heads_per_slab, ln_ref, wxo_ref, wup_ref, wdown_ref):
    atts = []
    for o_heads in o_heads_per_slab:
        att = None
        for h, oh in enumerate(o_heads):
            t = _dot(oh.astype(BF16), wxo_ref[h * MEM_HEAD_DIM:(h + 1) * MEM_HEAD_DIM, :])
            att = t if att is None else att + t
        atts.append(att)
    x2s = [_layer_norm(ALPHA * x1 + att, ln_ref[2:3, :], ln_ref[3:4, :]) for x1, att in zip(x1s, atts)]
    x2bs = [x2.astype(BF16) for x2 in x2s]
    ffns = [None] * len(x2s)
    for c in range(D_FF // FFN_CHUNK):
        cs = slice(c * FFN_CHUNK, (c + 1) * FFN_CHUNK)
        hcols = [jnp.maximum(_dot(x2b, wup_ref[:, cs]), 0.0) for x2b in x2bs]
        for n, hcol in enumerate(hcols):
            t = _dot((hcol * hcol).astype(BF16), wdown_ref[cs, :])
            ffns[n] = t if ffns[n] is None else ffns[n] + t
    return [_layer_norm(ALPHA * x2 + ffn, ln_ref[4:5, :], ln_ref[5:6, :]) for x2, ffn in zip(x2s, ffns)]


POST_SLABS = 2


def _post_prompt_kernel(x_ref, a_ref, c_ref, mkv_ref, ln_ref,
                        wo_ref, wxq_ref, wxo_ref, wup_ref, wdown_ref, y_ref):
    rows_per_slab = x_ref.shape[0] // POST_SLABS
    slabs = [slice(n * rows_per_slab, (n + 1) * rows_per_slab) for n in range(POST_SLABS)]
    mixes = [_dot(a_ref[r, :], wo_ref[0:ATTN_WIDTH, :]) + _dot(c_ref[r, :], wo_ref[ATTN_WIDTH:, :]) for r in slabs]
    x1s, q2s = _mix_ln1_q([x_ref[r, :] for r in slabs], mixes, ln_ref, wxq_ref)
    o_heads_per_slab = [[] for _ in slabs]
    for h in range(MEM_HEADS):
        hs = slice(h * MEM_HEAD_DIM, (h + 1) * MEM_HEAD_DIM)
        mk_h = mkv_ref[:, hs]
        mv_h = mkv_ref[:, D_MODEL + h * MEM_HEAD_DIM:D_MODEL + (h + 1) * MEM_HEAD_DIM]
        ss = [_dot_nt((q2[:, hs] * MEM_QSCALE).astype(BF16), mk_h) for q2 in q2s]
        ps = [jnp.exp2(s - jnp.max(s, axis=-1, keepdims=True)) for s in ss]
        for n, p in enumerate(ps):
            inv_l = 1.0 / jnp.sum(p, axis=-1, keepdims=True)
            o_heads_per_slab[n].append(_dot(p.astype(BF16), mv_h) * inv_l)
    ys = _xo_ffn(x1s, o_heads_per_slab, ln_ref, wxo_ref, wup_ref, wdown_ref)
    for r, y in zip(slabs, ys):
        y_ref[r, :] = y


def _post_prompt(x2d, attn, conv, mkv, ln, w_o, w_xq, w_xo, w_up, w_down, *, batch, seq, tm, mem_tokens):
    n_t = seq // tm
    row = lambda width: pl.BlockSpec((tm, width), lambda b, t: (b * n_t + t, 0))
    mem = pl.BlockSpec((mem_tokens, 2 * D_MODEL), lambda b, t: (b, 0))
    weights = (ln, w_o, w_xq, w_xo, w_up, w_down)
    return pl.pallas_call(
        _post_prompt_kernel,
        grid=(batch, n_t),
        in_specs=[row(D_MODEL), row(ATTN_WIDTH), row(CONV_CH), mem] + [_resident(w.shape) for w in weights],
        out_specs=row(D_MODEL),
        out_shape=jax.ShapeDtypeStruct((batch * seq, D_MODEL), F32),
        compiler_params=pltpu.CompilerParams(dimension_semantics=("arbitrary", "arbitrary"),
                                             vmem_limit_bytes=56 * MIB),
        name="post_prompt",
    )(x2d, attn, conv, mkv, *weights)


def _post1_sample_kernel(x_ref, ol_ref, c_ref, wuv_ref, ln_ref, wo_ref, wxq_ref, x1_ref, q2_ref):
    mix = _dot(c_ref[...], wo_ref[ATTN_WIDTH:, :])
    for h in range(MLA_HEADS):
        a_h = _dot(ol_ref[:, h * KV_LORA:(h + 1) * KV_LORA].astype(BF16), wuv_ref[h])
        mix = mix + _dot(a_h.astype(BF16), wo_ref[h * MLA_HEAD_DIM:(h + 1) * MLA_HEAD_DIM, :])
    (x1,), (q2,) = _mix_ln1_q([x_ref[...]], [mix], ln_ref, wxq_ref)
    x1_ref[...] = x1
    q2_ref[...] = q2 * MEM_QSCALE


def _post1_sample(x2d, o_lat, conv, w_uv, ln, w_o, w_xq):
    args = (x2d, o_lat, conv, w_uv, ln, w_o, w_xq)
    n = x2d.shape[0]
    return pl.pallas_call(
        _post1_sample_kernel,
        grid=(1,),
        in_specs=[_resident(a.shape) for a in args],
        out_specs=[pl.BlockSpec((n, D_MODEL), lambda i: (0, 0))] * 2,
        out_shape=[jax.ShapeDtypeStruct((n, D_MODEL), F32)] * 2,
        compiler_params=pltpu.CompilerParams(dimension_semantics=("arbitrary",), vmem_limit_bytes=32 * MIB),
        name="post1_sample",
    )(*args)


def _cross_sample_kernel(q_ref, k_ref, v_ref, o_ref, *, bb):
    for i in range(bb):
        q = q_ref[i]
        s = jnp.sum(k_ref[i] * q[None], axis=-1, keepdims=True)
        p = jnp.exp2(s - jnp.max(s, axis=0, keepdims=True))
        l = jnp.sum(p, axis=0)
        o_ref[i] = jnp.sum(p * v_ref[i], axis=0) / l


def _cross_sample(q2, mem_k, mem_v, *, bb):
    nb, m_tok = mem_k.shape[:2]
    q_spec = pl.BlockSpec((bb, MEM_HEADS, MEM_HEAD_DIM), lambda i: (i, 0, 0))
    kv_spec = pl.BlockSpec((bb, m_tok, MEM_HEADS, MEM_HEAD_DIM), lambda i: (i, 0, 0, 0))
    return pl.pallas_call(
        functools.partial(_cross_sample_kernel, bb=bb),
        grid=(nb // bb,),
        in_specs=[q_spec, kv_spec, kv_spec],
        out_specs=q_spec,
        out_shape=jax.ShapeDtypeStruct((nb, MEM_HEADS, MEM_HEAD_DIM), F32),
        compiler_params=pltpu.CompilerParams(dimension_semantics=("arbitrary",), vmem_limit_bytes=40 * MIB),
        name="cross_sample",
    )(q2, mem_k, mem_v)


def _post2_sample_kernel(x1_ref, o_ref, ln_ref, wxo_ref, wup_ref, wdown_ref, y_ref):
    o_heads = [o_ref[:, h * MEM_HEAD_DIM:(h + 1) * MEM_HEAD_DIM] for h in range(MEM_HEADS)]
    (y_ref[...],) = _xo_ffn([x1_ref[...]], [o_heads], ln_ref, wxo_ref, wup_ref, wdown_ref)


def _post2_sample(x1, o, ln, w_xo, w_up, w_down):
    args = (x1, o, ln, w_xo, w_up, w_down)
    n = x1.shape[0]
    return pl.pallas_call(
        _post2_sample_kernel,
        grid=(1,),
        in_specs=[_resident(a.shape) for a in args],
        out_specs=pl.BlockSpec((n, D_MODEL), lambda i: (0, 0)),
        out_shape=jax.ShapeDtypeStruct((n, D_MODEL), F32),
        compiler_params=pltpu.CompilerParams(dimension_semantics=("arbitrary",), vmem_limit_bytes=40 * MIB),
        name="post2_sample",
    )(*args)


DECODE_CHUNK_PAGES = 8
DECODE_ISSUE_UNROLL = 8


def _decode_kernel(pt_ref, ql_ref, qr_ref, cn_ref, krn_ref, cpool_ref, rpool_ref, o_ref,
                   cbuf, rbuf, sem, *, n_pages, page_size):
    b = pl.program_id(0)
    nb = pl.num_programs(0)

    def start_all(bi, slot):
        def body(p, carry):
            pg = pt_ref[bi * n_pages + p]
            pltpu.make_async_copy(cpool_ref.at[0, pg], cbuf.at[slot, p], sem.at[slot, 0]).start()
            pltpu.make_async_copy(rpool_ref.at[pg], rbuf.at[slot, p], sem.at[slot, 1]).start()
            return carry
        lax.fori_loop(0, n_pages, body, 0, unroll=DECODE_ISSUE_UNROLL)

    def wait_all(slot):
        pltpu.make_async_copy(cpool_ref.at[0, pl.ds(0, n_pages)], cbuf.at[slot], sem.at[slot, 0]).wait()
        pltpu.make_async_copy(rpool_ref.at[pl.ds(0, n_pages)], rbuf.at[slot], sem.at[slot, 1]).wait()

    slot = b % 2

    @pl.when(b == 0)
    def _():
        start_all(0, 0)

    @pl.when(b + 1 < nb)
    def _():
        start_all(b + 1, 1 - slot)

    wait_all(slot)

    ql32 = ql_ref[0]
    qr32 = qr_ref[0]
    c_new = cn_ref[0]
    kr_new = krn_ref[0]
    ql, qr = ql32.astype(BF16), qr32.astype(BF16)

    def chunk(c):
        pages = range(c * DECODE_CHUNK_PAGES, (c + 1) * DECODE_CHUNK_PAGES)
        cb = cbuf[slot, pages.start:pages.stop].reshape(DECODE_CHUNK_PAGES * page_size, KV_LORA).astype(BF16)
        rb = jnp.concatenate([rbuf[slot, p] for p in pages], axis=1).astype(BF16)
        return cb, _dot_nt(ql, cb) + _dot(qr, rb)

    m = (jnp.sum(ql32 * c_new, axis=-1, keepdims=True)
         + jnp.sum(qr32 * kr_new, axis=-1, keepdims=True))
    l = jnp.ones_like(m)
    o = jnp.broadcast_to(c_new, (MLA_HEADS, KV_LORA))
    n_chunks = n_pages // DECODE_CHUNK_PAGES
    nxt = chunk(0)
    for c in range(n_chunks):
        cb, s = nxt
        if c + 1 < n_chunks:
            nxt = chunk(c + 1)
        m_new = jnp.maximum(m, jnp.max(s, axis=-1, keepdims=True))
        alpha = jnp.exp2(m - m_new)
        p = jnp.exp2(s - m_new)
        l = alpha * l + jnp.sum(p, axis=-1, keepdims=True)
        o = alpha * o + _dot(p.astype(BF16), cb)
        m = m_new
    o_ref[0] = o / l


def _decode(page_table, q_lat, q_rope, c_new, kr_new, cpool, rpool):
    nb, n_pages = page_table.shape
    page_size = cpool.shape[2]
    assert n_pages % DECODE_CHUNK_PAGES == 0 and n_pages % DECODE_ISSUE_UNROLL == 0
    per_seq = lambda *tail: pl.BlockSpec((1,) + tail, lambda b, pt: (b, 0, 0))
    grid_spec = pltpu.PrefetchScalarGridSpec(
        num_scalar_prefetch=1,
        grid=(nb,),
        in_specs=[per_seq(MLA_HEADS, KV_LORA), per_seq(MLA_HEADS, ROPE_DIM), per_seq(1, KV_LORA), per_seq(1, ROPE_DIM),
                  pl.BlockSpec(memory_space=pl.ANY),
                  pl.BlockSpec(memory_space=pl.ANY)],
        out_specs=per_seq(MLA_HEADS, KV_LORA),
        scratch_shapes=[pltpu.VMEM((2, n_pages, page_size, KV_LORA), F32),
                        pltpu.VMEM((2, n_pages, ROPE_DIM, page_size), F32),
                        pltpu.SemaphoreType.DMA((2, 2))],
    )
    return pl.pallas_call(
        functools.partial(_decode_kernel, n_pages=n_pages, page_size=page_size),
        grid_spec=grid_spec,
        out_shape=jax.ShapeDtypeStruct((nb, MLA_HEADS, KV_LORA), F32),
        compiler_params=pltpu.CompilerParams(dimension_semantics=("arbitrary",), vmem_limit_bytes=48 * MIB),
        name="mla_sample_paged",
    )(page_table.reshape(-1), q_lat, q_rope, c_new, kr_new, cpool, rpool)


def _rope_tables(pos):
    inv_freq = jnp.exp(-math.log(ROPE_BASE) * jnp.arange(HALF_ROPE, dtype=F32) / HALF_ROPE)
    ang = pos.astype(F32)[:, None] * inv_freq[None, :]
    cos, sin = jnp.cos(ang), jnp.sin(ang)
    n = pos.shape[0]
    zeros = jnp.zeros((n, LANES - HALF_ROPE), F32)
    cos_t = jnp.concatenate([cos, cos, jnp.ones((n, LANES - ROPE_DIM), F32)], axis=1)
    sin_a = jnp.concatenate([-sin, zeros], axis=1)
    sin_b = jnp.concatenate([jnp.zeros((n, HALF_ROPE), F32), sin, zeros[:, HALF_ROPE:]], axis=1)
    return jnp.concatenate([cos_t, sin_a, sin_b], axis=1)


def _pad_cols(w, left, right):
    return jnp.pad(w, [(0, 0)] * (w.ndim - 1) + [(left, right)])


def _layout_w_in(w_in):
    wq = w_in[:, :Q_COLS].reshape(D_MODEL, MLA_HEADS, MLA_HEAD_DIM + ROPE_DIM)
    wq = jnp.concatenate([wq[..., MLA_HEAD_DIM:], wq[..., :MLA_HEAD_DIM]], axis=-1)
    wq = _pad_cols(wq, 0, HEAD_BLOCK - (MLA_HEAD_DIM + ROPE_DIM)).reshape(D_MODEL, MLA_HEADS * HEAD_BLOCK)
    c0 = Q_COLS
    wc = w_in[:, c0:c0 + KV_LORA]
    wkr = _pad_cols(w_in[:, c0 + KV_LORA:c0 + KV_LORA + ROPE_DIM], 0, HEAD_BLOCK - ROPE_DIM)
    wg = w_in[:, c0 + KV_LORA + ROPE_DIM:]
    return jnp.concatenate([wq, wkr, wc, wg], axis=1).astype(BF16)


def kernel(x_prompt, x_sample, cache_ckv, cache_krope, state_conv, cache_mem_k, cache_mem_v, page_table, mem_prompt, w_in, kv_norm_g, w_uk, w_uv, conv_w, conv_b, conv_ln_g, conv_ln_b, w_o, ln1_g, ln1_b, w_xq, w_xk, w_xv, w_xo, ln2_g, ln2_b, w_up, w_down, ln3_g, ln3_b):
    assert w_in.shape[0] == DEPTH == 1
    B, S, _ = x_prompt.shape
    Bd, T, _ = x_sample.shape
    assert T == 1
    n_pool, page_size, _ = cache_ckv.shape[1:]
    n_pages = page_table.shape[1]
    past_len = n_pages * page_size
    mem_tokens = mem_prompt.shape[1]
    l = 0

    w_all = _layout_w_in(w_in[l])
    g_kv = kv_norm_g[l][None, :]
    wuk_blk = _pad_cols(w_uk[l], ROPE_DIM, HEAD_BLOCK - ROPE_DIM - MLA_HEAD_DIM
                        ).reshape(KV_LORA, MLA_HEADS * HEAD_BLOCK).astype(BF16)
    wuv_t = jnp.pad(jnp.transpose(w_uv[l], (1, 2, 0)), ((0, 0), (0, V_BLOCK - MLA_HEAD_DIM), (0, 0))
                    ).reshape(MLA_HEADS * V_BLOCK, KV_LORA).astype(BF16)
    wq2l = jnp.pad(jnp.transpose(w_uk[l], (1, 2, 0)),
                   ((0, 0), (ROPE_DIM, HEAD_BLOCK - ROPE_DIM - MLA_HEAD_DIM), (0, 0))).astype(BF16)
    wuv_h = jnp.transpose(w_uv[l], (1, 0, 2)).astype(BF16)
    conv_params = jnp.stack([conv_b[l], conv_ln_g[l], conv_ln_b[l]])
    ln = jnp.stack([ln1_g[l], ln1_b[l], ln2_g[l], ln2_b[l], ln3_g[l], ln3_b[l],
                    jnp.zeros_like(ln1_g[l]), jnp.zeros_like(ln1_g[l])])
    wo_b, wxq_b, wxo_b = w_o[l].astype(BF16), w_xq[l].astype(BF16), w_xo[l].astype(BF16)
    wup_b, wdown_b = w_up[l].astype(BF16), w_down[l].astype(BF16)
    w_xkv = jnp.concatenate([w_xk[l], w_xv[l]], axis=1).astype(BF16)

    xp = x_prompt.reshape(B * S, D_MODEL)
    tab_p = _rope_tables(jnp.arange(S, dtype=jnp.int32))
    attn_tile = 512
    q_p, ckv_p, kr_p, u_p, k_p, vt_p = _inproj(xp, w_all, tab_p, g_kv, (wuk_blk, wuv_t), tm=attn_tile, sample=False)
    attn_p = _flash(q_p, k_p, vt_p, batch=B, seq=S, t=attn_tile)
    conv_p = _conv_prompt(u_p, conv_w[l], conv_params, batch=B, seq=S, ts=512)
    mk_p, mv_p, mkv_p = _memkv(mem_prompt.reshape(B * mem_tokens, D_MODEL), w_xkv, tm=512)
    y_p = _post_prompt(xp, attn_p, conv_p, mkv_p, ln, wo_b, wxq_b, wxo_b, wup_b, wdown_b,
                       batch=B, seq=S, tm=512, mem_tokens=mem_tokens)

    xs = x_sample.reshape(Bd, D_MODEL)
    tab_s = _rope_tables(jnp.full((Bd,), past_len, dtype=jnp.int32))
    q_s, ckv_s, kr_s, u_s, qlat_s = _inproj(xs, w_all, tab_s, g_kv, (wq2l,), tm=Bd, sample=True)
    q_rope = q_s.reshape(Bd, MLA_HEADS, HEAD_BLOCK)[:, :, :ROPE_DIM]
    rpool = jnp.swapaxes(cache_krope[l], 1, 2)
    o_lat = _decode(page_table, qlat_s.reshape(Bd, MLA_HEADS, KV_LORA), q_rope,
                    ckv_s[:, None, :], kr_s[:, None, :], cache_ckv, rpool)
    conv_s, new_state = _conv_sample(state_conv[l], u_s[:, None, :], conv_w[l], conv_params, bb=16)
    x1_s, q2_s = _post1_sample(xs, o_lat.reshape(Bd, MLA_HEADS * KV_LORA), conv_s.reshape(Bd, CONV_CH),
                               wuv_h, ln, wo_b, wxq_b)
    o_s = _cross_sample(q2_s.reshape(Bd, MEM_HEADS, MEM_HEAD_DIM), cache_mem_k[l], cache_mem_v[l], bb=4)
    y_s = _post2_sample(x1_s, o_s.reshape(Bd, D_MODEL), ln, wxo_b, wup_b, wdown_b)

    mem_shape = (1, B, mem_tokens, MEM_HEADS, MEM_HEAD_DIM)
    return (y_p.reshape(B, S, D_MODEL),
            y_s.reshape(Bd, T, D_MODEL),
            ckv_p.reshape(1, B, S, KV_LORA),
            jnp.swapaxes(kr_p, 1, 2)[None],
            u_p.reshape(B, S, CONV_CH)[None, :, S - CONV_STATE:, :],
            mk_p.reshape(mem_shape),
            mv_p.reshape(mem_shape),
            ckv_s.reshape(1, Bd, T, KV_LORA),
            kr_s.reshape(1, Bd, T, ROPE_DIM),
            new_state[None])
```

```python
import functools
import math

import jax
import jax.numpy as jnp
from jax import lax
from jax.experimental import pallas as pl
from jax.experimental.pallas import tpu as pltpu

D_MODEL = 1024
MLA_HEADS = 8
MLA_HEAD_DIM = 64
ROPE_DIM = 32
KV_LORA = D_MODEL // 4
ATTN_WIDTH = MLA_HEADS * MLA_HEAD_DIM
CONV_CH = D_MODEL - ATTN_WIDTH
CONV_K = 31
CONV_STATE = CONV_K - 1
MEM_HEADS = 4
MEM_HEAD_DIM = D_MODEL // MEM_HEADS
D_FF = 4 * D_MODEL
ROPE_BASE = 10000.0
LN_EPS = 1e-5
DEPTH = 1
ALPHA = (2.0 * DEPTH) ** 0.25
Q_COLS = MLA_HEADS * (MLA_HEAD_DIM + ROPE_DIM)
LOG2E = math.log2(math.e)
MLA_QSCALE = (MLA_HEAD_DIM + ROPE_DIM) ** -0.5 * LOG2E
MEM_QSCALE = MEM_HEAD_DIM ** -0.5 * LOG2E

LANES = 128
SUBLANES = 8
BF16_SUBLANES = 16
HEAD_BLOCK = LANES
HALF_ROPE = ROPE_DIM // 2
NEG_BIG = -1e30
MIB = 1024 * 1024

F32 = jnp.float32
BF16 = jnp.bfloat16

QK_COLS = (MLA_HEADS + 1) * HEAD_BLOCK
IN_COLS_PADDED = QK_COLS + KV_LORA + 2 * CONV_CH


def _dot(a, b):
    return jnp.dot(a, b, preferred_element_type=F32)


def _dot_nt(a, b):
    return lax.dot_general(a, b, (((1,), (1,)), ((), ())), preferred_element_type=F32)


def _layer_norm(x, g, b):
    mu = jnp.mean(x, axis=-1, keepdims=True)
    xc = x - mu
    var = jnp.mean(xc * xc, axis=-1, keepdims=True)
    return xc * lax.rsqrt(var + LN_EPS) * g + b


def _resident(shape):
    nd = len(shape)
    return pl.BlockSpec(shape, lambda *_: (0,) * nd, pipeline_mode=pl.Buffered(1))


def _rope_block(zb, cos_t, sin_a, sin_b):
    return zb * cos_t + pltpu.roll(zb, LANES - HALF_ROPE, 1) * sin_a + pltpu.roll(zb, HALF_ROPE, 1) * sin_b


def _inproj_common(x_ref, w_ref, tab_ref, g_ref):
    x = x_ref[...].astype(BF16)
    tab = tab_ref[...]
    cos_t, sin_a, sin_b = tab[:, :LANES], tab[:, LANES:2 * LANES], tab[:, 2 * LANES:]
    zq = _dot(x, w_ref[:, :QK_COLS])
    blocks = [_rope_block(zq[:, h * HEAD_BLOCK:(h + 1) * HEAD_BLOCK], cos_t, sin_a, sin_b)
              for h in range(MLA_HEADS + 1)]
    zc = _dot(x, w_ref[:, QK_COLS:QK_COLS + KV_LORA])
    c = zc * lax.rsqrt(jnp.mean(zc * zc, axis=-1, keepdims=True) + LN_EPS) * g_ref[...]
    zg = _dot(x, w_ref[:, QK_COLS + KV_LORA:])
    u = zg[:, :CONV_CH] * jax.nn.sigmoid(zg[:, CONV_CH:])
    return blocks, c, u


V_ONES_ROW = MLA_HEAD_DIM
V_BLOCK = -(-(MLA_HEAD_DIM + 1) // BF16_SUBLANES) * BF16_SUBLANES


def _inproj_prompt_kernel(x_ref, w_ref, tab_ref, g_ref, wuk_ref, wuvt_ref,
                          q_ref, ckv_ref, kr_ref, u_ref, k_ref, vt_ref):
    blocks, c, u = _inproj_common(x_ref, w_ref, tab_ref, g_ref)
    kr = blocks[MLA_HEADS]
    for h in range(MLA_HEADS):
        q_ref[:, h * HEAD_BLOCK:(h + 1) * HEAD_BLOCK] = (blocks[h] * MLA_QSCALE).astype(BF16)
    kr_ref[0] = kr.T[0:ROPE_DIM, :]
    ckv_ref[...] = c
    u_ref[...] = u
    cb = c.astype(BF16)
    kn = _dot(cb, wuk_ref[...])
    for h in range(MLA_HEADS):
        k_ref[:, h * HEAD_BLOCK:(h + 1) * HEAD_BLOCK] = (kn[:, h * HEAD_BLOCK:(h + 1) * HEAD_BLOCK] + kr).astype(BF16)
    vt = _dot_nt(wuvt_ref[...], cb)
    row = lax.broadcasted_iota(jnp.int32, vt.shape, 0)
    vt_ref[0] = jnp.where(row % V_BLOCK == V_ONES_ROW, 1.0, vt).astype(BF16)


def _inproj_sample_kernel(x_ref, w_ref, tab_ref, g_ref, wq2l_ref,
                          q_ref, ckv_ref, kr_ref, u_ref, qlat_ref):
    blocks, c, u = _inproj_common(x_ref, w_ref, tab_ref, g_ref)
    kr = blocks[MLA_HEADS]
    for h in range(MLA_HEADS):
        qb = blocks[h] * MLA_QSCALE
        q_ref[:, h * HEAD_BLOCK:(h + 1) * HEAD_BLOCK] = qb
        qlat_ref[:, h * KV_LORA:(h + 1) * KV_LORA] = _dot(qb.astype(BF16), wq2l_ref[h])
    kr_ref[...] = kr[:, :ROPE_DIM]
    ckv_ref[...] = c
    u_ref[...] = u


def _inproj(x2d, w_all, tab, g, extra_ws, *, tm, sample):
    n = x2d.shape[0]
    n_tiles = n // tm
    n_tab_blocks = tab.shape[0] // tm
    row = lambda width: pl.BlockSpec((tm, width), lambda i: (i, 0))
    in_specs = [row(D_MODEL), _resident(w_all.shape),
                pl.BlockSpec((tm, 3 * LANES), lambda i: (i % n_tab_blocks, 0)),
                _resident(g.shape)] + [_resident(w.shape) for w in extra_ws]
    widths = [(D_MODEL, F32 if sample else BF16), (KV_LORA, F32), (ROPE_DIM, F32), (CONV_CH, F32)]
    widths.append((MLA_HEADS * KV_LORA, F32) if sample else (MLA_HEADS * HEAD_BLOCK, BF16))
    out_specs = [row(w) for w, _ in widths]
    out_shape = [jax.ShapeDtypeStruct((n, w), dt) for w, dt in widths]
    if not sample:
        out_specs.append(pl.BlockSpec((1, MLA_HEADS * V_BLOCK, tm), lambda i: (i, 0, 0)))
        out_shape.append(jax.ShapeDtypeStruct((n_tiles, MLA_HEADS * V_BLOCK, tm), BF16))
        out_specs[2] = pl.BlockSpec((1, ROPE_DIM, tm), lambda i: (i // n_tab_blocks, 0, i % n_tab_blocks))
        out_shape[2] = jax.ShapeDtypeStruct((n_tiles // n_tab_blocks, ROPE_DIM, tab.shape[0]), F32)
    return pl.pallas_call(
        _inproj_sample_kernel if sample else _inproj_prompt_kernel,
        grid=(n_tiles,),
        in_specs=in_specs,
        out_specs=out_specs,
        out_shape=out_shape,
        compiler_params=pltpu.CompilerParams(dimension_semantics=("arbitrary",), vmem_limit_bytes=48 * MIB),
        name="inproj_sample" if sample else "inproj_prompt",
    )(x2d, w_all, tab, g, *extra_ws)


CONV_HALO = 32
CONV_ROWS = 64


def _conv_tail(y, cb, g, b):
    t = _layer_norm(y + cb, g, b)
    return t * jax.nn.sigmoid(t)


def _conv_prompt_kernel(u_ref, w_ref, p_ref, o_ref, ext_ref, sh_ref, *, ts):
    @pl.when(pl.program_id(1) == 0)
    def _():
        ext_ref[0:CONV_HALO, :] = jnp.zeros((CONV_HALO, CONV_CH), F32)

    ext_ref[CONV_HALO:CONV_HALO + ts, :] = u_ref[...]
    cb, g, b = p_ref[0:1, :], p_ref[1:2, :], p_ref[2:3, :]
    off = CONV_HALO - CONV_STATE
    for r in range(1, SUBLANES):
        sh_ref[r - 1] = ext_ref[r:r + sh_ref.shape[1], :]

    for r0 in range(0, ts, CONV_ROWS):
        acc = jnp.zeros((CONV_ROWS, CONV_CH), F32)
        for k in range(CONV_K):
            a, r = divmod(k + off, SUBLANES)
            lo = r0 + SUBLANES * a
            src = ext_ref[lo:lo + CONV_ROWS, :] if r == 0 else sh_ref[r - 1, lo:lo + CONV_ROWS, :]
            acc = acc + w_ref[k:k + 1, :] * src
        o_ref[r0:r0 + CONV_ROWS, :] = _conv_tail(acc, cb, g, b).astype(o_ref.dtype)
    ext_ref[0:CONV_HALO, :] = ext_ref[ts:ts + CONV_HALO, :]


def _conv_prompt(u2d, conv_w, conv_params, *, batch, seq, ts):
    n_s = seq // ts
    return pl.pallas_call(
        functools.partial(_conv_prompt_kernel, ts=ts),
        grid=(batch, n_s),
        in_specs=[pl.BlockSpec((ts, CONV_CH), lambda b, s: (b * n_s + s, 0)),
                  _resident(conv_w.shape), _resident(conv_params.shape)],
        out_specs=pl.BlockSpec((ts, CONV_CH), lambda b, s: (b * n_s + s, 0)),
        out_shape=jax.ShapeDtypeStruct((batch * seq, CONV_CH), BF16),
        scratch_shapes=[pltpu.VMEM((ts + CONV_HALO, CONV_CH), F32),
                        pltpu.VMEM((SUBLANES - 1, ts + CONV_HALO - SUBLANES, CONV_CH), F32)],
        compiler_params=pltpu.CompilerParams(dimension_semantics=("arbitrary", "arbitrary"),
                                             vmem_limit_bytes=32 * MIB),
        name="conv_prompt",
    )(u2d, conv_w, conv_params)


def _conv_sample_kernel(s_ref, u_ref, w_ref, p_ref, o_ref, ns_ref):
    st = s_ref[...]
    u = u_ref[...]
    w_hist = w_ref[0:CONV_STATE, :]
    y = jnp.sum(st * w_hist[None, :, :], axis=1, keepdims=True) + u * w_ref[CONV_STATE:CONV_K, :][None]
    o_ref[...] = _conv_tail(y, p_ref[0:1, :][None], p_ref[1:2, :][None], p_ref[2:3, :][None]).astype(o_ref.dtype)
    ns_ref[:, 0:CONV_STATE - 1, :] = s_ref[:, 1:CONV_STATE, :]
    ns_ref[:, CONV_STATE - 1:CONV_STATE, :] = u


def _conv_sample(state, u, conv_w, conv_params, *, bb):
    nb = state.shape[0]
    return pl.pallas_call(
        _conv_sample_kernel,
        grid=(nb // bb,),
        in_specs=[pl.BlockSpec((bb, CONV_STATE, CONV_CH), lambda i: (i, 0, 0)),
                  pl.BlockSpec((bb, 1, CONV_CH), lambda i: (i, 0, 0)),
                  _resident(conv_w.shape), _resident(conv_params.shape)],
        out_specs=[pl.BlockSpec((bb, 1, CONV_CH), lambda i: (i, 0, 0)),
                   pl.BlockSpec((bb, CONV_STATE, CONV_CH), lambda i: (i, 0, 0))],
        out_shape=[jax.ShapeDtypeStruct((nb, 1, CONV_CH), BF16),
                   jax.ShapeDtypeStruct(state.shape, F32)],
        compiler_params=pltpu.CompilerParams(dimension_semantics=("arbitrary",), vmem_limit_bytes=32 * MIB),
        name="conv_sample",
    )(state, u, conv_w, conv_params)


FLASH_HEAD_GROUP = 4


def _flash_kernel(q_ref, k_ref, vt_ref, o_ref, *state, t):
    i = pl.program_id(1)
    kpos = lax.broadcasted_iota(jnp.int32, (t, t), 0)
    qpos = lax.broadcasted_iota(jnp.int32, (t, t), 1)
    causal = kpos <= qpos

    m_refs, acc_refs = state[:MLA_HEADS], state[MLA_HEADS:]
    for h in range(MLA_HEADS):
        m_refs[h][...] = jnp.full((1, t), NEG_BIG, F32)
        acc_refs[h][...] = jnp.zeros((V_BLOCK, t), F32)

    def kv_tile(j, masked):
        rows = pl.ds(pl.multiple_of(j * t, t), t)
        for g in range(0, MLA_HEADS, FLASH_HEAD_GROUP):
            heads = range(g, g + FLASH_HEAD_GROUP)
            hs = {h: slice(h * HEAD_BLOCK, (h + 1) * HEAD_BLOCK) for h in heads}
            vs = {h: slice(h * V_BLOCK, (h + 1) * V_BLOCK) for h in heads}
            sts = {h: _dot_nt(k_ref[rows, hs[h]], q_ref[:, hs[h]]) for h in heads}
            pts, alphas = {}, {}
            for h in heads:
                st = jnp.where(causal, sts[h], NEG_BIG) if masked else sts[h]
                m_old = m_refs[h][...]
                m_new = jnp.maximum(m_old, jnp.max(st, axis=0, keepdims=True))
                m_refs[h][...] = m_new
                alphas[h] = jnp.exp2(m_old - m_new)
                pts[h] = jnp.exp2(st - m_new).astype(BF16)
            for h in heads:
                acc_refs[h][...] = alphas[h] * acc_refs[h][...] + _dot(vt_ref[j, vs[h], :], pts[h])

    def off_diag(j, carry):
        kv_tile(j, False)
        return carry

    lax.fori_loop(0, i, off_diag, 0)
    kv_tile(i, True)

    def finish(h):
        acc = acc_refs[h][...]
        return acc[0:MLA_HEAD_DIM, :] / acc[V_ONES_ROW:V_ONES_ROW + 1, :]

    for pair in range(MLA_HEADS // 2):
        o_t = jnp.concatenate([finish(2 * pair), finish(2 * pair + 1)], axis=0)
        o_ref[:, pair * LANES:(pair + 1) * LANES] = o_t.T.astype(o_ref.dtype)


def _flash(q, k, vt, *, batch, seq, t):
    n_t = seq // t
    return pl.pallas_call(
        functools.partial(_flash_kernel, t=t),
        grid=(batch, n_t),
        in_specs=[pl.BlockSpec((t, q.shape[1]), lambda b, i: (b * n_t + i, 0)),
                  pl.BlockSpec((seq, k.shape[1]), lambda b, i: (b, 0)),
                  pl.BlockSpec((n_t, vt.shape[1], t), lambda b, i: (b, 0, 0))],
        out_specs=pl.BlockSpec((t, ATTN_WIDTH), lambda b, i: (b * n_t + i, 0)),
        out_shape=jax.ShapeDtypeStruct((batch * seq, ATTN_WIDTH), BF16),
        scratch_shapes=[pltpu.VMEM((1, t), F32)] * MLA_HEADS + [pltpu.VMEM((V_BLOCK, t), F32)] * MLA_HEADS,
        compiler_params=pltpu.CompilerParams(dimension_semantics=("arbitrary", "arbitrary"),
                                             vmem_limit_bytes=48 * MIB),
        name="mla_prompt_flash",
    )(q, k, vt)


def _memkv_kernel(m_ref, w_ref, k_ref, v_ref, kvb_ref):
    kv = _dot(m_ref[...].astype(BF16), w_ref[...])
    kvb_ref[...] = kv.astype(BF16)
    for h in range(MEM_HEADS):
        k_ref[:, h, :] = kv[:, h * MEM_HEAD_DIM:(h + 1) * MEM_HEAD_DIM]
        v_ref[:, h, :] = kv[:, D_MODEL + h * MEM_HEAD_DIM:D_MODEL + (h + 1) * MEM_HEAD_DIM]


def _memkv(mem2d, w_xkv, *, tm):
    n = mem2d.shape[0]
    row = pl.BlockSpec((tm, D_MODEL), lambda i: (i, 0))
    cache = pl.BlockSpec((tm, MEM_HEADS, MEM_HEAD_DIM), lambda i: (i, 0, 0))
    return pl.pallas_call(
        _memkv_kernel,
        grid=(n // tm,),
        in_specs=[row, _resident(w_xkv.shape)],
        out_specs=[cache, cache, pl.BlockSpec((tm, 2 * D_MODEL), lambda i: (i, 0))],
        out_shape=[jax.ShapeDtypeStruct((n, MEM_HEADS, MEM_HEAD_DIM), F32)] * 2
        + [jax.ShapeDtypeStruct((n, 2 * D_MODEL), BF16)],
        compiler_params=pltpu.CompilerParams(dimension_semantics=("arbitrary",), vmem_limit_bytes=40 * MIB),
        name="mem_kv",
    )(mem2d, w_xkv)


FFN_CHUNK = 1024


def _mix_ln1_q(xs, mixes, ln_ref, wxq_ref):
    x1s = [_layer_norm(ALPHA * x + mix, ln_ref[0:1, :], ln_ref[1:2, :]) for x, mix in zip(xs, mixes)]
    return x1s, [_dot(x1.astype(BF16), wxq_ref[...]) for x1 in x1s]


def _xo_ffn(x1s, o_heads_per_slab, ln_ref, wxo_ref, wup_ref, wdown_ref):
    atts = []
    for o_heads in o_heads_per_slab:
        att = None
        for h, oh in enumerate(o_heads):
            t = _dot(oh.astype(BF16), wxo_ref[h * MEM_HEAD_DIM:(h + 1) * MEM_HEAD_DIM, :])
            att = t if att is None else att + t
        atts.append(att)
    x2s = [_layer_norm(ALPHA * x1 + att, ln_ref[2:3, :], ln_ref[3:4, :]) for x1, att in zip(x1s, atts)]
    x2bs = [x2.astype(BF16) for x2 in x2s]
    ffns = [None] * len(x2s)
    for c in range(D_FF // FFN_CHUNK):
        cs = slice(c * FFN_CHUNK, (c + 1) * FFN_CHUNK)
        hcols = [jnp.maximum(_dot(x2b, wup_ref[:, cs]), 0.0) for x2b in x2bs]
        for n, hcol in enumerate(hcols):
            t = _dot((hcol * hcol).astype(BF16), wdown_ref[cs, :])
            ffns[n] = t if ffns[n] is None else ffns[n] + t
    return [_layer_norm(ALPHA * x2 + ffn, ln_ref[4:5, :], ln_ref[5:6, :]) for x2, ffn in zip(x2s, ffns)]


POST_SLABS = 2


def _post_prompt_kernel(x_ref, a_ref, c_ref, mkv_ref, ln_ref,
                        wo_ref, wxq_ref, wxo_ref, wup_ref, wdown_ref, y_ref):
    rows_per_slab = x_ref.shape[0] // POST_SLABS
    slabs = [slice(n * rows_per_slab, (n + 1) * rows_per_slab) for n in range(POST_SLABS)]
    head_cols = [slice(h * MEM_HEAD_DIM, (h + 1) * MEM_HEAD_DIM) for h in range(MEM_HEADS)]

    def mix(n):
        r = slabs[n]
        return _dot(a_ref[r, :], wo_ref[0:ATTN_WIDTH, :]) + _dot(c_ref[r, :], wo_ref[ATTN_WIDTH:, :])

    def ln1(n, mix_n):
        return _layer_norm(ALPHA * x_ref[slabs[n], :] + mix_n, ln_ref[0:1, :], ln_ref[1:2, :])

    def xq(x1):
        return _dot(x1.astype(BF16), wxq_ref[...]) * MEM_QSCALE

    def scores(q2):
        return [_dot_nt(q2[:, hs].astype(BF16), mkv_ref[:, hs]) for hs in head_cols]

    def softmax(ss):
        ps = [jnp.exp2(s - jnp.max(s, axis=-1, keepdims=True)) for s in ss]
        return ps, [1.0 / jnp.sum(p, axis=-1, keepdims=True) for p in ps]

    def values(ps, inv_ls):
        return [_dot(p.astype(BF16), mkv_ref[:, D_MODEL + h * MEM_HEAD_DIM:D_MODEL + (h + 1) * MEM_HEAD_DIM]) * inv_l
                for h, (p, inv_l) in enumerate(zip(ps, inv_ls))]

    def xo(o_heads):
        att = None
        for h, oh in enumerate(o_heads):
            t = _dot(oh.astype(BF16), wxo_ref[head_cols[h], :])
            att = t if att is None else att + t
        return att

    def ln2(x1, att):
        return _layer_norm(ALPHA * x1 + att, ln_ref[2:3, :], ln_ref[3:4, :])

    mix_a, mix_b = mix(0), mix(1)
    x1_a = ln1(0, mix_a)
    q2_a = xq(x1_a)
    x1_b = ln1(1, mix_b)
    s_a = scores(q2_a)
    q2_b = xq(x1_b)
    p_a = softmax(s_a)
    o_a = values(*p_a)
    s_b = scores(q2_b)
    att_a = xo(o_a)
    p_b = softmax(s_b)
    o_b = values(*p_b)
    x2_a = ln2(x1_a, att_a)
    att_b = xo(o_b)
    x2_b = ln2(x1_b, att_b)

    x2s = [x2_a, x2_b]
    x2bs = [x2.astype(BF16) for x2 in x2s]
    ffns = [None] * POST_SLABS
    for c in range(D_FF // FFN_CHUNK):
        cs = slice(c * FFN_CHUNK, (c + 1) * FFN_CHUNK)
        hcols = [jnp.maximum(_dot(x2b, wup_ref[:, cs]), 0.0) for x2b in x2bs]
        for n, hcol in enumerate(hcols):
            t = _dot((hcol * hcol).astype(BF16), wdown_ref[cs, :])
            ffns[n] = t if ffns[n] is None else ffns[n] + t
    for n in range(POST_SLABS):
        y_ref[slabs[n], :] = _layer_norm(ALPHA * x2s[n] + ffns[n], ln_ref[4:5, :], ln_ref[5:6, :])


def _post_prompt(x2d, attn, conv, mkv, ln, w_o, w_xq, w_xo, w_up, w_down, *, batch, seq, tm, mem_tokens):
    n_t = seq // tm
    row = lambda width: pl.BlockSpec((tm, width), lambda b, t: (b * n_t + t, 0))
    mem = pl.BlockSpec((mem_tokens, 2 * D_MODEL), lambda b, t: (b, 0))
    weights = (ln, w_o, w_xq, w_xo, w_up, w_down)
    return pl.pallas_call(
        _post_prompt_kernel,
        grid=(batch, n_t),
        in_specs=[row(D_MODEL), row(ATTN_WIDTH), row(CONV_CH), mem] + [_resident(w.shape) for w in weights],
        out_specs=row(D_MODEL),
        out_shape=jax.ShapeDtypeStruct((batch * seq, D_MODEL), F32),
        compiler_params=pltpu.CompilerParams(dimension_semantics=("arbitrary", "arbitrary"),
                                             vmem_limit_bytes=56 * MIB),
        name="post_prompt",
    )(x2d, attn, conv, mkv, *weights)


def _post1_sample_kernel(x_ref, ol_ref, c_ref, wuv_ref, ln_ref, wo_ref, wxq_ref, x1_ref, q2_ref):
    mix = _dot(c_ref[...], wo_ref[ATTN_WIDTH:, :])
    for h in range(MLA_HEADS):
        a_h = _dot(ol_ref[:, h * KV_LORA:(h + 1) * KV_LORA].astype(BF16), wuv_ref[h])
        mix = mix + _dot(a_h.astype(BF16), wo_ref[h * MLA_HEAD_DIM:(h + 1) * MLA_HEAD_DIM, :])
    (x1,), (q2,) = _mix_ln1_q([x_ref[...]], [mix], ln_ref, wxq_ref)
    x1_ref[...] = x1
    q2_ref[...] = q2 * MEM_QSCALE


def _post1_sample(x2d, o_lat, conv, w_uv, ln, w_o, w_xq):
    args = (x2d, o_lat, conv, w_uv, ln, w_o, w_xq)
    n = x2d.shape[0]
    return pl.pallas_call(
        _post1_sample_kernel,
        grid=(1,),
        in_specs=[_resident(a.shape) for a in args],
        out_specs=[pl.BlockSpec((n, D_MODEL), lambda i: (0, 0))] * 2,
        out_shape=[jax.ShapeDtypeStruct((n, D_MODEL), F32)] * 2,
        compiler_params=pltpu.CompilerParams(dimension_semantics=("arbitrary",), vmem_limit_bytes=32 * MIB),
        name="post1_sample",
    )(*args)


def _cross_sample_kernel(q_ref, k_ref, v_ref, o_ref, *, bb):
    pairs = k_ref.shape[1] // 2
    for i in range(bb):
        q = q_ref[i]
        q2 = jnp.concatenate([q, q], axis=0)[None]
        k = k_ref[i].reshape(pairs, 2 * MEM_HEADS, MEM_HEAD_DIM)
        v = v_ref[i].reshape(pairs, 2 * MEM_HEADS, MEM_HEAD_DIM)
        s = jnp.sum(k * q2, axis=-1, keepdims=True)
        m = jnp.max(s, axis=0, keepdims=True)
        m = jnp.maximum(m[:, :MEM_HEADS], m[:, MEM_HEADS:])
        p = jnp.exp2(s - jnp.concatenate([m, m], axis=1))
        l = jnp.sum(p, axis=0)
        o = jnp.sum(p * v, axis=0)
        o_ref[i] = (o[:MEM_HEADS] + o[MEM_HEADS:]) / (l[:MEM_HEADS] + l[MEM_HEADS:])


def _cross_sample(q2, mem_k, mem_v, *, bb):
    nb, m_tok = mem_k.shape[:2]
    q_spec = pl.BlockSpec((bb, MEM_HEADS, MEM_HEAD_DIM), lambda i: (i, 0, 0))
    kv_spec = pl.BlockSpec((bb, m_tok, MEM_HEADS, MEM_HEAD_DIM), lambda i: (i, 0, 0, 0))
    return pl.pallas_call(
        functools.partial(_cross_sample_kernel, bb=bb),
        grid=(nb // bb,),
        in_specs=[q_spec, kv_spec, kv_spec],
        out_specs=q_spec,
        out_shape=jax.ShapeDtypeStruct((nb, MEM_HEADS, MEM_HEAD_DIM), F32),
        compiler_params=pltpu.CompilerParams(dimension_semantics=("arbitrary",), vmem_limit_bytes=40 * MIB),
        name="cross_sample",
    )(q2, mem_k, mem_v)


def _post2_sample_kernel(x1_ref, o_ref, ln_ref, wxo_ref, wup_ref, wdown_ref, y_ref):
    o_heads = [o_ref[:, h * MEM_HEAD_DIM:(h + 1) * MEM_HEAD_DIM] for h in range(MEM_HEADS)]
    (y_ref[...],) = _xo_ffn([x1_ref[...]], [o_heads], ln_ref, wxo_ref, wup_ref, wdown_ref)


def _post2_sample(x1, o, ln, w_xo, w_up, w_down):
    args = (x1, o, ln, w_xo, w_up, w_down)
    n = x1.shape[0]
    return pl.pallas_call(
        _post2_sample_kernel,
        grid=(1,),
        in_specs=[_resident(a.shape) for a in args],
        out_specs=pl.BlockSpec((n, D_MODEL), lambda i: (0, 0)),
        out_shape=jax.ShapeDtypeStruct((n, D_MODEL), F32),
        compiler_params=pltpu.CompilerParams(dimension_semantics=("arbitrary",), vmem_limit_bytes=40 * MIB),
        name="post2_sample",
    )(*args)


DECODE_CHUNK_PAGES = 8
DECODE_ISSUE_UNROLL = 8


def _decode_kernel(pt_ref, ql_ref, qr_ref, cn_ref, krn_ref, cpool_ref, rpool_ref, o_ref,
                   cbuf, rbuf, sem, *, n_pages, page_size):
    b = pl.program_id(0)
    nb = pl.num_programs(0)

    def start_all(bi, slot):
        def body(p, carry):
            pg = pt_ref[bi * n_pages + p]
            pltpu.make_async_copy(cpool_ref.at[0, pg], cbuf.at[slot, p], sem.at[slot, 0]).start()
            pltpu.make_async_copy(rpool_ref.at[pg], rbuf.at[slot, p], sem.at[slot, 1]).start()
            return carry
        lax.fori_loop(0, n_pages, body, 0, unroll=DECODE_ISSUE_UNROLL)

    def wait_all(slot):
        pltpu.make_async_copy(cpool_ref.at[0, pl.ds(0, n_pages)], cbuf.at[slot], sem.at[slot, 0]).wait()
        pltpu.make_async_copy(rpool_ref.at[pl.ds(0, n_pages)], rbuf.at[slot], sem.at[slot, 1]).wait()

    slot = b % 2

    @pl.when(b == 0)
    def _():
        start_all(0, 0)

    @pl.when(b + 1 < nb)
    def _():
        start_all(b + 1, 1 - slot)

    wait_all(slot)

    ql32 = ql_ref[0]
    qr32 = qr_ref[0]
    c_new = cn_ref[0]
    kr_new = krn_ref[0]
    ql, qr = ql32.astype(BF16), qr32.astype(BF16)

    def chunk(c):
        pages = range(c * DECODE_CHUNK_PAGES, (c + 1) * DECODE_CHUNK_PAGES)
        cb = cbuf[slot, pages.start:pages.stop].reshape(DECODE_CHUNK_PAGES * page_size, KV_LORA).astype(BF16)
        rb = jnp.concatenate([rbuf[slot, p] for p in pages], axis=1).astype(BF16)
        return cb, _dot_nt(ql, cb) + _dot(qr, rb)

    m = (jnp.sum(ql32 * c_new, axis=-1, keepdims=True)
         + jnp.sum(qr32 * kr_new, axis=-1, keepdims=True))
    l = jnp.ones_like(m)
    o = jnp.broadcast_to(c_new, (MLA_HEADS, KV_LORA))
    n_chunks = n_pages // DECODE_CHUNK_PAGES
    nxt = chunk(0)
    for c in range(n_chunks):
        cb, s = nxt
        if c + 1 < n_chunks:
            nxt = chunk(c + 1)
        m_new = jnp.maximum(m, jnp.max(s, axis=-1, keepdims=True))
        alpha = jnp.exp2(m - m_new)
        p = jnp.exp2(s - m_new)
        l = alpha * l + jnp.sum(p, axis=-1, keepdims=True)
        o = alpha * o + _dot(p.astype(BF16), cb)
        m = m_new
    o_ref[0] = o / l


def _decode(page_table, q_lat, q_rope, c_new, kr_new, cpool, rpool):
    nb, n_pages = page_table.shape
    page_size = cpool.shape[2]
    assert n_pages % DECODE_CHUNK_PAGES == 0 and n_pages % DECODE_ISSUE_UNROLL == 0
    per_seq = lambda *tail: pl.BlockSpec((1,) + tail, lambda b, pt: (b, 0, 0))
    grid_spec = pltpu.PrefetchScalarGridSpec(
        num_scalar_prefetch=1,
        grid=(nb,),
        in_specs=[per_seq(MLA_HEADS, KV_LORA), per_seq(MLA_HEADS, ROPE_DIM), per_seq(1, KV_LORA), per_seq(1, ROPE_DIM),
                  pl.BlockSpec(memory_space=pl.ANY),
                  pl.BlockSpec(memory_space=pl.ANY)],
        out_specs=per_seq(MLA_HEADS, KV_LORA),
        scratch_shapes=[pltpu.VMEM((2, n_pages, page_size, KV_LORA), F32),
                        pltpu.VMEM((2, n_pages, ROPE_DIM, page_size), F32),
                        pltpu.SemaphoreType.DMA((2, 2))],
    )
    return pl.pallas_call(
        functools.partial(_decode_kernel, n_pages=n_pages, page_size=page_size),
        grid_spec=grid_spec,
        out_shape=jax.ShapeDtypeStruct((nb, MLA_HEADS, KV_LORA), F32),
        compiler_params=pltpu.CompilerParams(dimension_semantics=("arbitrary",), vmem_limit_bytes=48 * MIB),
        name="mla_sample_paged",
    )(page_table.reshape(-1), q_lat, q_rope, c_new, kr_new, cpool, rpool)


def _rope_tables(pos):
    inv_freq = jnp.exp(-math.log(ROPE_BASE) * jnp.arange(HALF_ROPE, dtype=F32) / HALF_ROPE)
    ang = pos.astype(F32)[:, None] * inv_freq[None, :]
    cos, sin = jnp.cos(ang), jnp.sin(ang)
    n = pos.shape[0]
    zeros = jnp.zeros((n, LANES - HALF_ROPE), F32)
    cos_t = jnp.concatenate([cos, cos, jnp.ones((n, LANES - ROPE_DIM), F32)], axis=1)
    sin_a = jnp.concatenate([-sin, zeros], axis=1)
    sin_b = jnp.concatenate([jnp.zeros((n, HALF_ROPE), F32), sin, zeros[:, HALF_ROPE:]], axis=1)
    return jnp.concatenate([cos_t, sin_a, sin_b], axis=1)


def _pad_cols(w, left, right):
    return jnp.pad(w, [(0, 0)] * (w.ndim - 1) + [(left, right)])


def _layout_w_in(w_in):
    wq = w_in[:, :Q_COLS].reshape(D_MODEL, MLA_HEADS, MLA_HEAD_DIM + ROPE_DIM)
    wq = jnp.concatenate([wq[..., MLA_HEAD_DIM:], wq[..., :MLA_HEAD_DIM]], axis=-1)
    wq = _pad_cols(wq, 0, HEAD_BLOCK - (MLA_HEAD_DIM + ROPE_DIM)).reshape(D_MODEL, MLA_HEADS * HEAD_BLOCK)
    c0 = Q_COLS
    wc = w_in[:, c0:c0 + KV_LORA]
    wkr = _pad_cols(w_in[:, c0 + KV_LORA:c0 + KV_LORA + ROPE_DIM], 0, HEAD_BLOCK - ROPE_DIM)
    wg = w_in[:, c0 + KV_LORA + ROPE_DIM:]
    return jnp.concatenate([wq, wkr, wc, wg], axis=1).astype(BF16)


def kernel(x_prompt, x_sample, cache_ckv, cache_krope, state_conv, cache_mem_k, cache_mem_v, page_table, mem_prompt, w_in, kv_norm_g, w_uk, w_uv, conv_w, conv_b, conv_ln_g, conv_ln_b, w_o, ln1_g, ln1_b, w_xq, w_xk, w_xv, w_xo, ln2_g, ln2_b, w_up, w_down, ln3_g, ln3_b):
    assert w_in.shape[0] == DEPTH == 1
    B, S, _ = x_prompt.shape
    Bd, T, _ = x_sample.shape
    assert T == 1
    n_pool, page_size, _ = cache_ckv.shape[1:]
    n_pages = page_table.shape[1]
    past_len = n_pages * page_size
    mem_tokens = mem_prompt.shape[1]
    l = 0

    w_all = _layout_w_in(w_in[l])
    g_kv = kv_norm_g[l][None, :]
    wuk_blk = _pad_cols(w_uk[l], ROPE_DIM, HEAD_BLOCK - ROPE_DIM - MLA_HEAD_DIM
                        ).reshape(KV_LORA, MLA_HEADS * HEAD_BLOCK).astype(BF16)
    wuv_t = jnp.pad(jnp.transpose(w_uv[l], (1, 2, 0)), ((0, 0), (0, V_BLOCK - MLA_HEAD_DIM), (0, 0))
                    ).reshape(MLA_HEADS * V_BLOCK, KV_LORA).astype(BF16)
    wq2l = jnp.pad(jnp.transpose(w_uk[l], (1, 2, 0)),
                   ((0, 0), (ROPE_DIM, HEAD_BLOCK - ROPE_DIM - MLA_HEAD_DIM), (0, 0))).astype(BF16)
    wuv_h = jnp.transpose(w_uv[l], (1, 0, 2)).astype(BF16)
    conv_params = jnp.stack([conv_b[l], conv_ln_g[l], conv_ln_b[l]])
    ln = jnp.stack([ln1_g[l], ln1_b[l], ln2_g[l], ln2_b[l], ln3_g[l], ln3_b[l],
                    jnp.zeros_like(ln1_g[l]), jnp.zeros_like(ln1_g[l])])
    wo_b, wxq_b, wxo_b = w_o[l].astype(BF16), w_xq[l].astype(BF16), w_xo[l].astype(BF16)
    wup_b, wdown_b = w_up[l].astype(BF16), w_down[l].astype(BF16)
    w_xkv = jnp.concatenate([w_xk[l], w_xv[l]], axis=1).astype(BF16)

    xp = x_prompt.reshape(B * S, D_MODEL)
    tab_p = _rope_tables(jnp.arange(S, dtype=jnp.int32))
    attn_tile = 512
    q_p, ckv_p, kr_p, u_p, k_p, vt_p = _inproj(xp, w_all, tab_p, g_kv, (wuk_blk, wuv_t), tm=attn_tile, sample=False)
    attn_p = _flash(q_p, k_p, vt_p, batch=B, seq=S, t=attn_tile)
    conv_p = _conv_prompt(u_p, conv_w[l], conv_params, batch=B, seq=S, ts=512)
    mk_p, mv_p, mkv_p = _memkv(mem_prompt.reshape(B * mem_tokens, D_MODEL), w_xkv, tm=512)
    y_p = _post_prompt(xp, attn_p, conv_p, mkv_p, ln, wo_b, wxq_b, wxo_b, wup_b, wdown_b,
                       batch=B, seq=S, tm=512, mem_tokens=mem_tokens)

    xs = x_sample.reshape(Bd, D_MODEL)
    tab_s = _rope_tables(jnp.full((Bd,), past_len, dtype=jnp.int32))
    q_s, ckv_s, kr_s, u_s, qlat_s = _inproj(xs, w_all, tab_s, g_kv, (wq2l,), tm=Bd, sample=True)
    q_rope = q_s.reshape(Bd, MLA_HEADS, HEAD_BLOCK)[:, :, :ROPE_DIM]
    rpool = jnp.swapaxes(cache_krope[l], 1, 2)
    o_lat = _decode(page_table, qlat_s.reshape(Bd, MLA_HEADS, KV_LORA), q_rope,
                    ckv_s[:, None, :], kr_s[:, None, :], cache_ckv, rpool)
    conv_s, new_state = _conv_sample(state_conv[l], u_s[:, None, :], conv_w[l], conv_params, bb=16)
    x1_s, q2_s = _post1_sample(xs, o_lat.reshape(Bd, MLA_HEADS * KV_LORA), conv_s.reshape(Bd, CONV_CH),
                               wuv_h, ln, wo_b, wxq_b)
    o_s = _cross_sample(q2_s.reshape(Bd, MEM_HEADS, MEM_HEAD_DIM), cache_mem_k[l], cache_mem_v[l], bb=4)
    y_s = _post2_sample(x1_s, o_s.reshape(Bd, D_MODEL), ln, wxo_b, wup_b, wdown_b)

    mem_shape = (1, B, mem_tokens, MEM_HEADS, MEM_HEAD_DIM)
    return (y_p.reshape(B, S, D_MODEL),
            y_s.reshape(Bd, T, D_MODEL),
            ckv_p.reshape(1, B, S, KV_LORA),
            jnp.swapaxes(kr_p, 1, 2)[None],
            u_p.reshape(B, S, CONV_CH)[None, :, S - CONV_STATE:, :],
            mk_p.reshape(mem_shape),
            mv_p.reshape(mem_shape),
            ckv_s.reshape(1, Bd, T, KV_LORA),
            kr_s.reshape(1, Bd, T, ROPE_DIM),
            new_state[None])
```

```python
import functools
import math

import jax
import jax.numpy as jnp
from jax import lax
from jax.experimental import pallas as pl
from jax.experimental.pallas import tpu as pltpu

D_MODEL = 1024
MLA_HEADS = 8
MLA_HEAD_DIM = 64
ROPE_DIM = 32
KV_LORA = D_MODEL // 4
ATTN_WIDTH = MLA_HEADS * MLA_HEAD_DIM
CONV_CH = D_MODEL - ATTN_WIDTH
CONV_K = 31
CONV_STATE = CONV_K - 1
MEM_HEADS = 4
MEM_HEAD_DIM = D_MODEL // MEM_HEADS
D_FF = 4 * D_MODEL
ROPE_BASE = 10000.0
LN_EPS = 1e-5
DEPTH = 1
ALPHA = (2.0 * DEPTH) ** 0.25
Q_COLS = MLA_HEADS * (MLA_HEAD_DIM + ROPE_DIM)
LOG2E = math.log2(math.e)
MLA_QSCALE = (MLA_HEAD_DIM + ROPE_DIM) ** -0.5 * LOG2E
MEM_QSCALE = MEM_HEAD_DIM ** -0.5 * LOG2E

LANES = 128
SUBLANES = 8
BF16_SUBLANES = 16
HEAD_BLOCK = LANES
HALF_ROPE = ROPE_DIM // 2
NEG_BIG = -1e30
MIB = 1024 * 1024

F32 = jnp.float32
BF16 = jnp.bfloat16

QK_COLS = (MLA_HEADS + 1) * HEAD_BLOCK
IN_COLS_PADDED = QK_COLS + KV_LORA + 2 * CONV_CH


def _dot(a, b):
    return jnp.dot(a, b, preferred_element_type=F32)


def _dot_nt(a, b):
    return lax.dot_general(a, b, (((1,), (1,)), ((), ())), preferred_element_type=F32)


def _layer_norm(x, g, b):
    mu = jnp.mean(x, axis=-1, keepdims=True)
    xc = x - mu
    var = jnp.mean(xc * xc, axis=-1, keepdims=True)
    return xc * lax.rsqrt(var + LN_EPS) * g + b


def _resident(shape):
    nd = len(shape)
    return pl.BlockSpec(shape, lambda *_: (0,) * nd, pipeline_mode=pl.Buffered(1))


def _rope_block(zb, cos_t, sin_a, sin_b):
    return zb * cos_t + pltpu.roll(zb, LANES - HALF_ROPE, 1) * sin_a + pltpu.roll(zb, HALF_ROPE, 1) * sin_b


def _inproj_common(x_ref, w_ref, tab_ref, g_ref):
    x = x_ref[...].astype(BF16)
    tab = tab_ref[...]
    cos_t, sin_a, sin_b = tab[:, :LANES], tab[:, LANES:2 * LANES], tab[:, 2 * LANES:]
    zq = _dot(x, w_ref[:, :QK_COLS])
    blocks = [_rope_block(zq[:, h * HEAD_BLOCK:(h + 1) * HEAD_BLOCK], cos_t, sin_a, sin_b)
              for h in range(MLA_HEADS + 1)]
    zc = _dot(x, w_ref[:, QK_COLS:QK_COLS + KV_LORA])
    c = zc * lax.rsqrt(jnp.mean(zc * zc, axis=-1, keepdims=True) + LN_EPS) * g_ref[...]
    zg = _dot(x, w_ref[:, QK_COLS + KV_LORA:])
    u = zg[:, :CONV_CH] * jax.nn.sigmoid(zg[:, CONV_CH:])
    return blocks, c, u


V_ONES_ROW = MLA_HEAD_DIM
V_BLOCK = -(-(MLA_HEAD_DIM + 1) // BF16_SUBLANES) * BF16_SUBLANES


def _inproj_prompt_kernel(x_ref, w_ref, tab_ref, g_ref, wuk_ref, wuvt_ref,
                          q_ref, ckv_ref, kr_ref, u_ref, k_ref, vt_ref):
    blocks, c, u = _inproj_common(x_ref, w_ref, tab_ref, g_ref)
    kr = blocks[MLA_HEADS]
    for h in range(MLA_HEADS):
        q_ref[:, h * HEAD_BLOCK:(h + 1) * HEAD_BLOCK] = (blocks[h] * MLA_QSCALE).astype(BF16)
    kr_ref[0] = kr.T[0:ROPE_DIM, :]
    ckv_ref[...] = c
    u_ref[...] = u
    cb = c.astype(BF16)
    kn = _dot(cb, wuk_ref[...])
    for h in range(MLA_HEADS):
        k_ref[:, h * HEAD_BLOCK:(h + 1) * HEAD_BLOCK] = (kn[:, h * HEAD_BLOCK:(h + 1) * HEAD_BLOCK] + kr).astype(BF16)
    vt = _dot_nt(wuvt_ref[...], cb)
    row = lax.broadcasted_iota(jnp.int32, vt.shape, 0)
    vt_ref[0] = jnp.where(row % V_BLOCK == V_ONES_ROW, 1.0, vt).astype(BF16)


def _inproj_sample_kernel(x_ref, w_ref, tab_ref, g_ref, wq2l_ref,
                          q_ref, ckv_ref, kr_ref, u_ref, qlat_ref):
    blocks, c, u = _inproj_common(x_ref, w_ref, tab_ref, g_ref)
    kr = blocks[MLA_HEADS]
    for h in range(MLA_HEADS):
        qb = blocks[h] * MLA_QSCALE
        q_ref[:, h * HEAD_BLOCK:(h + 1) * HEAD_BLOCK] = qb
        qlat_ref[:, h * KV_LORA:(h + 1) * KV_LORA] = _dot(qb.astype(BF16), wq2l_ref[h])
    kr_ref[...] = kr[:, :ROPE_DIM]
    ckv_ref[...] = c
    u_ref[...] = u


def _inproj(x2d, w_all, tab, g, extra_ws, *, tm, sample):
    n = x2d.shape[0]
    n_tiles = n // tm
    n_tab_blocks = tab.shape[0] // tm
    row = lambda width: pl.BlockSpec((tm, width), lambda i: (i, 0))
    in_specs = [row(D_MODEL), _resident(w_all.shape),
                pl.BlockSpec((tm, 3 * LANES), lambda i: (i % n_tab_blocks, 0)),
                _resident(g.shape)] + [_resident(w.shape) for w in extra_ws]
    widths = [(D_MODEL, F32 if sample else BF16), (KV_LORA, F32), (ROPE_DIM, F32), (CONV_CH, F32)]
    widths.append((MLA_HEADS * KV_LORA, F32) if sample else (MLA_HEADS * HEAD_BLOCK, BF16))
    out_specs = [row(w) for w, _ in widths]
    out_shape = [jax.ShapeDtypeStruct((n, w), dt) for w, dt in widths]
    if not sample:
        out_specs.append(pl.BlockSpec((1, MLA_HEADS * V_BLOCK, tm), lambda i: (i, 0, 0)))
        out_shape.append(jax.ShapeDtypeStruct((n_tiles, MLA_HEADS * V_BLOCK, tm), BF16))
        out_specs[2] = pl.BlockSpec((1, ROPE_DIM, tm), lambda i: (i // n_tab_blocks, 0, i % n_tab_blocks))
        out_shape[2] = jax.ShapeDtypeStruct((n_tiles // n_tab_blocks, ROPE_DIM, tab.shape[0]), F32)
    return pl.pallas_call(
        _inproj_sample_kernel if sample else _inproj_prompt_kernel,
        grid=(n_tiles,),
        in_specs=in_specs,
        out_specs=out_specs,
        out_shape=out_shape,
        compiler_params=pltpu.CompilerParams(dimension_semantics=("arbitrary",), vmem_limit_bytes=48 * MIB),
        name="inproj_sample" if sample else "inproj_prompt",
    )(x2d, w_all, tab, g, *extra_ws)


CONV_HALO = 32
CONV_ROWS = 64


def _conv_tail(y, cb, g, b):
    t = _layer_norm(y + cb, g, b)
    return t * jax.nn.sigmoid(t)


def _conv_sample_kernel(s_ref, u_ref, w_ref, p_ref, o_ref, ns_ref):
    st = s_ref[...]
    u = u_ref[...]
    w_hist = w_ref[0:CONV_STATE, :]
    y = jnp.sum(st * w_hist[None, :, :], axis=1, keepdims=True) + u * w_ref[CONV_STATE:CONV_K, :][None]
    o_ref[...] = _conv_tail(y, p_ref[0:1, :][None], p_ref[1:2, :][None], p_ref[2:3, :][None]).astype(o_ref.dtype)
    ns_ref[:, 0:CONV_STATE - 1, :] = s_ref[:, 1:CONV_STATE, :]
    ns_ref[:, CONV_STATE - 1:CONV_STATE, :] = u


def _conv_sample(state, u, conv_w, conv_params, *, bb):
    nb = state.shape[0]
    return pl.pallas_call(
        _conv_sample_kernel,
        grid=(nb // bb,),
        in_specs=[pl.BlockSpec((bb, CONV_STATE, CONV_CH), lambda i: (i, 0, 0)),
                  pl.BlockSpec((bb, 1, CONV_CH), lambda i: (i, 0, 0)),
                  _resident(conv_w.shape), _resident(conv_params.shape)],
        out_specs=[pl.BlockSpec((bb, 1, CONV_CH), lambda i: (i, 0, 0)),
                   pl.BlockSpec((bb, CONV_STATE, CONV_CH), lambda i: (i, 0, 0))],
        out_shape=[jax.ShapeDtypeStruct((nb, 1, CONV_CH), BF16),
                   jax.ShapeDtypeStruct(state.shape, F32)],
        compiler_params=pltpu.CompilerParams(dimension_semantics=("arbitrary",), vmem_limit_bytes=32 * MIB),
        name="conv_sample",
    )(state, u, conv_w, conv_params)


FLASH_HEAD_GROUP = 4


def _flash_kernel(q_ref, k_ref, vt_ref, o_ref, *state, t):
    i = pl.program_id(1)
    kpos = lax.broadcasted_iota(jnp.int32, (t, t), 0)
    qpos = lax.broadcasted_iota(jnp.int32, (t, t), 1)
    causal = kpos <= qpos

    m_refs, acc_refs = state[:MLA_HEADS], state[MLA_HEADS:]
    for h in range(MLA_HEADS):
        m_refs[h][...] = jnp.full((1, t), NEG_BIG, F32)
        acc_refs[h][...] = jnp.zeros((V_BLOCK, t), F32)

    def kv_tile(j, masked):
        rows = pl.ds(pl.multiple_of(j * t, t), t)
        for g in range(0, MLA_HEADS, FLASH_HEAD_GROUP):
            heads = range(g, g + FLASH_HEAD_GROUP)
            hs = {h: slice(h * HEAD_BLOCK, (h + 1) * HEAD_BLOCK) for h in heads}
            vs = {h: slice(h * V_BLOCK, (h + 1) * V_BLOCK) for h in heads}
            sts = {h: _dot_nt(k_ref[rows, hs[h]], q_ref[:, hs[h]]) for h in heads}
            pts, alphas = {}, {}
            for h in heads:
                st = jnp.where(causal, sts[h], NEG_BIG) if masked else sts[h]
                m_old = m_refs[h][...]
                m_new = jnp.maximum(m_old, jnp.max(st, axis=0, keepdims=True))
                m_refs[h][...] = m_new
                alphas[h] = jnp.exp2(m_old - m_new)
                pts[h] = jnp.exp2(st - m_new).astype(BF16)
            for h in heads:
                acc_refs[h][...] = alphas[h] * acc_refs[h][...] + _dot(vt_ref[j, vs[h], :], pts[h])

    def off_diag(j, carry):
        kv_tile(j, False)
        return carry

    lax.fori_loop(0, i, off_diag, 0)
    kv_tile(i, True)

    def finish(h):
        acc = acc_refs[h][...]
        return acc[0:MLA_HEAD_DIM, :] / acc[V_ONES_ROW:V_ONES_ROW + 1, :]

    for pair in range(MLA_HEADS // 2):
        o_t = jnp.concatenate([finish(2 * pair), finish(2 * pair + 1)], axis=0)
        o_ref[:, pair * LANES:(pair + 1) * LANES] = o_t.T.astype(o_ref.dtype)


def _flash(q, k, vt, *, batch, seq, t):
    n_t = seq // t
    return pl.pallas_call(
        functools.partial(_flash_kernel, t=t),
        grid=(batch, n_t),
        in_specs=[pl.BlockSpec((t, q.shape[1]), lambda b, i: (b * n_t + i, 0)),
                  pl.BlockSpec((seq, k.shape[1]), lambda b, i: (b, 0)),
                  pl.BlockSpec((n_t, vt.shape[1], t), lambda b, i: (b, 0, 0))],
        out_specs=pl.BlockSpec((t, ATTN_WIDTH), lambda b, i: (b * n_t + i, 0)),
        out_shape=jax.ShapeDtypeStruct((batch * seq, ATTN_WIDTH), BF16),
        scratch_shapes=[pltpu.VMEM((1, t), F32)] * MLA_HEADS + [pltpu.VMEM((V_BLOCK, t), F32)] * MLA_HEADS,
        compiler_params=pltpu.CompilerParams(dimension_semantics=("arbitrary", "arbitrary"),
                                             vmem_limit_bytes=48 * MIB),
        name="mla_prompt_flash",
    )(q, k, vt)


def _memkv_kernel(m_ref, w_ref, k_ref, v_ref, kvb_ref):
    kv = _dot(m_ref[...].astype(BF16), w_ref[...])
    kvb_ref[...] = kv.astype(BF16)
    for h in range(MEM_HEADS):
        k_ref[:, h, :] = kv[:, h * MEM_HEAD_DIM:(h + 1) * MEM_HEAD_DIM]
        v_ref[:, h, :] = kv[:, D_MODEL + h * MEM_HEAD_DIM:D_MODEL + (h + 1) * MEM_HEAD_DIM]


def _memkv(mem2d, w_xkv, *, tm):
    n = mem2d.shape[0]
    row = pl.BlockSpec((tm, D_MODEL), lambda i: (i, 0))
    cache = pl.BlockSpec((tm, MEM_HEADS, MEM_HEAD_DIM), lambda i: (i, 0, 0))
    return pl.pallas_call(
        _memkv_kernel,
        grid=(n // tm,),
        in_specs=[row, _resident(w_xkv.shape)],
        out_specs=[cache, cache, pl.BlockSpec((tm, 2 * D_MODEL), lambda i: (i, 0))],
        out_shape=[jax.ShapeDtypeStruct((n, MEM_HEADS, MEM_HEAD_DIM), F32)] * 2
        + [jax.ShapeDtypeStruct((n, 2 * D_MODEL), BF16)],
        compiler_params=pltpu.CompilerParams(dimension_semantics=("arbitrary",), vmem_limit_bytes=40 * MIB),
        name="mem_kv",
    )(mem2d, w_xkv)


FFN_CHUNK = 1024


def _mix_ln1_q(xs, mixes, ln_ref, wxq_ref):
    x1s = [_layer_norm(ALPHA * x + mix, ln_ref[0:1, :], ln_ref[1:2, :]) for x, mix in zip(xs, mixes)]
    return x1s, [_dot(x1.astype(BF16), wxq_ref[...]) for x1 in x1s]


def _xo_ffn(x1s, o_heads_per_slab, ln_ref, wxo_ref, wup_ref, wdown_ref):
    atts = []
    for o_heads in o_heads_per_slab:
        att = None
        for h, oh in enumerate(o_heads):
            t = _dot(oh.astype(BF16), wxo_ref[h * MEM_HEAD_DIM:(h + 1) * MEM_HEAD_DIM, :])
            att = t if att is None else att + t
        atts.append(att)
    x2s = [_layer_norm(ALPHA * x1 + att, ln_ref[2:3, :], ln_ref[3:4, :]) for x1, att in zip(x1s, atts)]
    x2bs = [x2.astype(BF16) for x2 in x2s]
    ffns = [None] * len(x2s)
    for c in range(D_FF // FFN_CHUNK):
        cs = slice(c * FFN_CHUNK, (c + 1) * FFN_CHUNK)
        hcols = [jnp.maximum(_dot(x2b, wup_ref[:, cs]), 0.0) for x2b in x2bs]
        for n, hcol in enumerate(hcols):
            t = _dot((hcol * hcol).astype(BF16), wdown_ref[cs, :])
            ffns[n] = t if ffns[n] is None else ffns[n] + t
    return [_layer_norm(ALPHA * x2 + ffn, ln_ref[4:5, :], ln_ref[5:6, :]) for x2, ffn in zip(x2s, ffns)]


POST_SLABS = 2


def _post_prompt_kernel(x_ref, a_ref, c_ref, mkv_ref, ln_ref,
                        wo_ref, wxq_ref, wxo_ref, wup_ref, wdown_ref, y_ref):
    rows_per_slab = x_ref.shape[0] // POST_SLABS
    slabs = [slice(n * rows_per_slab, (n + 1) * rows_per_slab) for n in range(POST_SLABS)]
    head_cols = [slice(h * MEM_HEAD_DIM, (h + 1) * MEM_HEAD_DIM) for h in range(MEM_HEADS)]

    def mix(n):
        r = slabs[n]
        return _dot(a_ref[r, :], wo_ref[0:ATTN_WIDTH, :]) + _dot(c_ref[r, :], wo_ref[ATTN_WIDTH:, :])

    def ln1(n, mix_n):
        return _layer_norm(ALPHA * x_ref[slabs[n], :] + mix_n, ln_ref[0:1, :], ln_ref[1:2, :])

    def xq(x1):
        return _dot(x1.astype(BF16), wxq_ref[...]) * MEM_QSCALE

    def scores(q2):
        return [_dot_nt(q2[:, hs].astype(BF16), mkv_ref[:, hs]) for hs in head_cols]

    def softmax(ss):
        ps = [jnp.exp2(s - jnp.max(s, axis=-1, keepdims=True)) for s in ss]
        return ps, [1.0 / jnp.sum(p, axis=-1, keepdims=True) for p in ps]

    def values(ps, inv_ls):
        return [_dot(p.astype(BF16), mkv_ref[:, D_MODEL + h * MEM_HEAD_DIM:D_MODEL + (h + 1) * MEM_HEAD_DIM]) * inv_l
                for h, (p, inv_l) in enumerate(zip(ps, inv_ls))]

    def xo(o_heads):
        att = None
        for h, oh in enumerate(o_heads):
            t = _dot(oh.astype(BF16), wxo_ref[head_cols[h], :])
            att = t if att is None else att + t
        return att

    def ln2(x1, att):
        return _layer_norm(ALPHA * x1 + att, ln_ref[2:3, :], ln_ref[3:4, :])

    mix_a, mix_b = mix(0), mix(1)
    x1_a = ln1(0, mix_a)
    q2_a = xq(x1_a)
    x1_b = ln1(1, mix_b)
    s_a = scores(q2_a)
    q2_b = xq(x1_b)
    p_a = softmax(s_a)
    o_a = values(*p_a)
    s_b = scores(q2_b)
    att_a = xo(o_a)
    p_b = softmax(s_b)
    o_b = values(*p_b)
    x2_a = ln2(x1_a, att_a)
    att_b = xo(o_b)
    x2_b = ln2(x1_b, att_b)

    x2s = [x2_a, x2_b]
    x2bs = [x2.astype(BF16) for x2 in x2s]
    ffns = [None] * POST_SLABS
    for c in range(D_FF // FFN_CHUNK):
        cs = slice(c * FFN_CHUNK, (c + 1) * FFN_CHUNK)
        hcols = [jnp.maximum(_dot(x2b, wup_ref[:, cs]), 0.0) for x2b in x2bs]
        for n, hcol in enumerate(hcols):
            t = _dot((hcol * hcol).astype(BF16), wdown_ref[cs, :])
            ffns[n] = t if ffns[n] is None else ffns[n] + t
    for n in range(POST_SLABS):
        y_ref[slabs[n], :] = _layer_norm(ALPHA * x2s[n] + ffns[n], ln_ref[4:5, :], ln_ref[5:6, :])


def _post_prompt(x2d, attn, conv, mkv, ln, w_o, w_xq, w_xo, w_up, w_down, *, batch, seq, tm, mem_tokens):
    n_t = seq // tm
    row = lambda width: pl.BlockSpec((tm, width), lambda b, t: (b * n_t + t, 0))
    mem = pl.BlockSpec((mem_tokens, 2 * D_MODEL), lambda b, t: (b, 0))
    weights = (ln, w_o, w_xq, w_xo, w_up, w_down)
    return pl.pallas_call(
        _post_prompt_kernel,
        grid=(batch, n_t),
        in_specs=[row(D_MODEL), row(ATTN_WIDTH), row(CONV_CH), mem] + [_resident(w.shape) for w in weights],
        out_specs=row(D_MODEL),
        out_shape=jax.ShapeDtypeStruct((batch * seq, D_MODEL), F32),
        compiler_params=pltpu.CompilerParams(dimension_semantics=("arbitrary", "arbitrary"),
                                             vmem_limit_bytes=56 * MIB),
        name="post_prompt",
    )(x2d, attn, conv, mkv, *weights)


def _post1_sample_kernel(x_ref, ol_ref, c_ref, wuv_ref, ln_ref, wo_ref, wxq_ref, x1_ref, q2_ref):
    mix = _dot(c_ref[...], wo_ref[ATTN_WIDTH:, :])
    for h in range(MLA_HEADS):
        a_h = _dot(ol_ref[:, h * KV_LORA:(h + 1) * KV_LORA].astype(BF16), wuv_ref[h])
        mix = mix + _dot(a_h.astype(BF16), wo_ref[h * MLA_HEAD_DIM:(h + 1) * MLA_HEAD_DIM, :])
    (x1,), (q2,) = _mix_ln1_q([x_ref[...]], [mix], ln_ref, wxq_ref)
    x1_ref[...] = x1
    q2_ref[...] = q2 * MEM_QSCALE


def _post1_sample(x2d, o_lat, conv, w_uv, ln, w_o, w_xq):
    args = (x2d, o_lat, conv, w_uv, ln, w_o, w_xq)
    n = x2d.shape[0]
    return pl.pallas_call(
        _post1_sample_kernel,
        grid=(1,),
        in_specs=[_resident(a.shape) for a in args],
        out_specs=[pl.BlockSpec((n, D_MODEL), lambda i: (0, 0))] * 2,
        out_shape=[jax.ShapeDtypeStruct((n, D_MODEL), F32)] * 2,
        compiler_params=pltpu.CompilerParams(dimension_semantics=("arbitrary",), vmem_limit_bytes=32 * MIB),
        name="post1_sample",
    )(*args)


def _cross_sample_kernel(q_ref, k_ref, v_ref, o_ref, *, bb):
    pairs = k_ref.shape[1] // 2
    for i in range(bb):
        q = q_ref[i]
        q2 = jnp.concatenate([q, q], axis=0)[None]
        k = k_ref[i].reshape(pairs, 2 * MEM_HEADS, MEM_HEAD_DIM)
        v = v_ref[i].reshape(pairs, 2 * MEM_HEADS, MEM_HEAD_DIM)
        s = jnp.sum(k * q2, axis=-1, keepdims=True)
        m = jnp.max(s, axis=0, keepdims=True)
        m = jnp.maximum(m[:, :MEM_HEADS], m[:, MEM_HEADS:])
        p = jnp.exp2(s - jnp.concatenate([m, m], axis=1))
        l = jnp.sum(p, axis=0)
        o = jnp.sum(p * v, axis=0)
        o_ref[i] = (o[:MEM_HEADS] + o[MEM_HEADS:]) / (l[:MEM_HEADS] + l[MEM_HEADS:])


def _cross_sample(q2, mem_k, mem_v, *, bb):
    nb, m_tok = mem_k.shape[:2]
    q_spec = pl.BlockSpec((bb, MEM_HEADS, MEM_HEAD_DIM), lambda i: (i, 0, 0))
    kv_spec = pl.BlockSpec((bb, m_tok, MEM_HEADS, MEM_HEAD_DIM), lambda i: (i, 0, 0, 0))
    return pl.pallas_call(
        functools.partial(_cross_sample_kernel, bb=bb),
        grid=(nb // bb,),
        in_specs=[q_spec, kv_spec, kv_spec],
        out_specs=q_spec,
        out_shape=jax.ShapeDtypeStruct((nb, MEM_HEADS, MEM_HEAD_DIM), F32),
        compiler_params=pltpu.CompilerParams(dimension_semantics=("arbitrary",), vmem_limit_bytes=40 * MIB),
        name="cross_sample",
    )(q2, mem_k, mem_v)


def _post2_sample_kernel(x1_ref, o_ref, ln_ref, wxo_ref, wup_ref, wdown_ref, y_ref):
    o_heads = [o_ref[:, h * MEM_HEAD_DIM:(h + 1) * MEM_HEAD_DIM] for h in range(MEM_HEADS)]
    (y_ref[...],) = _xo_ffn([x1_ref[...]], [o_heads], ln_ref, wxo_ref, wup_ref, wdown_ref)


def _post2_sample(x1, o, ln, w_xo, w_up, w_down):
    args = (x1, o, ln, w_xo, w_up, w_down)
    n = x1.shape[0]
    return pl.pallas_call(
        _post2_sample_kernel,
        grid=(1,),
        in_specs=[_resident(a.shape) for a in args],
        out_specs=pl.BlockSpec((n, D_MODEL), lambda i: (0, 0)),
        out_shape=jax.ShapeDtypeStruct((n, D_MODEL), F32),
        compiler_params=pltpu.CompilerParams(dimension_semantics=("arbitrary",), vmem_limit_bytes=40 * MIB),
        name="post2_sample",
    )(*args)


DECODE_CHUNK_PAGES = 8
DECODE_ISSUE_UNROLL = 8


def _decode_kernel(pt_ref, ql_ref, qr_ref, cn_ref, krn_ref, u_ref, cw_ref, cp_ref, cpool_ref, rpool_ref,
                   o_ref, conv_ref, cbuf, rbuf, sem, ext_ref, sh_ref, *, n_pages, page_size, conv_tiles_per_seq):
    b = pl.program_id(0)
    nb = pl.num_programs(0)
    conv_rows = u_ref.shape[0]

    @pl.when(b % conv_tiles_per_seq == 0)
    def _():
        ext_ref[0:CONV_HALO, :] = jnp.zeros((CONV_HALO, CONV_CH), F32)

    @pl.when(b % conv_tiles_per_seq != 0)
    def _():
        ext_ref[0:CONV_HALO, :] = ext_ref[conv_rows:conv_rows + CONV_HALO, :]

    ext_ref[CONV_HALO:CONV_HALO + conv_rows, :] = u_ref[...]
    conv_off = CONV_HALO - CONV_STATE
    for r in range(1, SUBLANES):
        sh_ref[r - 1] = ext_ref[r:r + sh_ref.shape[1], :]

    def conv_chunk(r0):
        acc = jnp.zeros((CONV_ROWS, CONV_CH), F32)
        for k in range(CONV_K):
            a, r = divmod(k + conv_off, SUBLANES)
            lo = r0 + SUBLANES * a
            src = ext_ref[lo:lo + CONV_ROWS, :] if r == 0 else sh_ref[r - 1, lo:lo + CONV_ROWS, :]
            acc = acc + cw_ref[k:k + 1, :] * src
        conv_ref[r0:r0 + CONV_ROWS, :] = _conv_tail(acc, cp_ref[0:1, :], cp_ref[1:2, :], cp_ref[2:3, :]
                                                    ).astype(conv_ref.dtype)

    conv_chunks = list(range(0, conv_rows, CONV_ROWS))

    def start_all(bi, slot):
        def body(p, carry):
            pg = pt_ref[bi * n_pages + p]
            pltpu.make_async_copy(cpool_ref.at[0, pg], cbuf.at[slot, p], sem.at[slot, 0]).start()
            pltpu.make_async_copy(rpool_ref.at[pg], rbuf.at[slot, p], sem.at[slot, 1]).start()
            return carry
        lax.fori_loop(0, n_pages, body, 0, unroll=DECODE_ISSUE_UNROLL)

    def wait_all(slot):
        pltpu.make_async_copy(cpool_ref.at[0, pl.ds(0, n_pages)], cbuf.at[slot], sem.at[slot, 0]).wait()
        pltpu.make_async_copy(rpool_ref.at[pl.ds(0, n_pages)], rbuf.at[slot], sem.at[slot, 1]).wait()

    slot = b % 2

    @pl.when(b == 0)
    def _():
        start_all(0, 0)

    @pl.when(b + 1 < nb)
    def _():
        start_all(b + 1, 1 - slot)

    wait_all(slot)

    ql32 = ql_ref[0]
    qr32 = qr_ref[0]
    c_new = cn_ref[0]
    kr_new = krn_ref[0]
    ql, qr = ql32.astype(BF16), qr32.astype(BF16)

    def chunk(c):
        pages = range(c * DECODE_CHUNK_PAGES, (c + 1) * DECODE_CHUNK_PAGES)
        cb = cbuf[slot, pages.start:pages.stop].reshape(DECODE_CHUNK_PAGES * page_size, KV_LORA).astype(BF16)
        rb = jnp.concatenate([rbuf[slot, p] for p in pages], axis=1).astype(BF16)
        return cb, _dot_nt(ql, cb) + _dot(qr, rb)

    m = (jnp.sum(ql32 * c_new, axis=-1, keepdims=True)
         + jnp.sum(qr32 * kr_new, axis=-1, keepdims=True))
    l = jnp.ones_like(m)
    o = jnp.broadcast_to(c_new, (MLA_HEADS, KV_LORA))
    n_chunks = n_pages // DECODE_CHUNK_PAGES
    nxt = chunk(0)
    for c in range(n_chunks):
        cb, s = nxt
        if c + 1 < n_chunks:
            nxt = chunk(c + 1)
        m_new = jnp.maximum(m, jnp.max(s, axis=-1, keepdims=True))
        alpha = jnp.exp2(m - m_new)
        p = jnp.exp2(s - m_new)
        l = alpha * l + jnp.sum(p, axis=-1, keepdims=True)
        o = alpha * o + _dot(p.astype(BF16), cb)
        m = m_new
    o_ref[0] = o / l
    for r0 in conv_chunks:
        conv_chunk(r0)


def _decode_and_prompt_conv(page_table, q_lat, q_rope, c_new, kr_new, cpool, rpool, u2d, conv_w, conv_params, *, seq):
    nb, n_pages = page_table.shape
    page_size = cpool.shape[2]
    assert n_pages % DECODE_CHUNK_PAGES == 0 and n_pages % DECODE_ISSUE_UNROLL == 0
    conv_rows, rem = divmod(u2d.shape[0], nb)
    assert rem == 0 and seq % conv_rows == 0 and conv_rows % CONV_ROWS == 0 and conv_rows >= CONV_HALO
    per_seq = lambda *tail: pl.BlockSpec((1,) + tail, lambda b, pt: (b, 0, 0))
    conv_tile = pl.BlockSpec((conv_rows, CONV_CH), lambda b, pt: (b, 0))
    whole = lambda a: pl.BlockSpec(a.shape, lambda b, pt: (0, 0))
    grid_spec = pltpu.PrefetchScalarGridSpec(
        num_scalar_prefetch=1,
        grid=(nb,),
        in_specs=[per_seq(MLA_HEADS, KV_LORA), per_seq(MLA_HEADS, ROPE_DIM), per_seq(1, KV_LORA), per_seq(1, ROPE_DIM),
                  conv_tile, whole(conv_w), whole(conv_params),
                  pl.BlockSpec(memory_space=pl.ANY),
                  pl.BlockSpec(memory_space=pl.ANY)],
        out_specs=[per_seq(MLA_HEADS, KV_LORA), conv_tile],
        scratch_shapes=[pltpu.VMEM((2, n_pages, page_size, KV_LORA), F32),
                        pltpu.VMEM((2, n_pages, ROPE_DIM, page_size), F32),
                        pltpu.SemaphoreType.DMA((2, 2)),
                        pltpu.VMEM((conv_rows + CONV_HALO, CONV_CH), F32),
                        pltpu.VMEM((SUBLANES - 1, conv_rows + CONV_HALO - SUBLANES, CONV_CH), F32)],
    )
    return pl.pallas_call(
        functools.partial(_decode_kernel, n_pages=n_pages, page_size=page_size, conv_tiles_per_seq=seq // conv_rows),
        grid_spec=grid_spec,
        out_shape=[jax.ShapeDtypeStruct((nb, MLA_HEADS, KV_LORA), F32),
                   jax.ShapeDtypeStruct((u2d.shape[0], CONV_CH), BF16)],
        compiler_params=pltpu.CompilerParams(dimension_semantics=("arbitrary",), vmem_limit_bytes=48 * MIB),
        name="mla_sample_paged_and_prompt_conv",
    )(page_table.reshape(-1), q_lat, q_rope, c_new, kr_new, u2d, conv_w, conv_params, cpool, rpool)


def _rope_tables(pos):
    inv_freq = jnp.exp(-math.log(ROPE_BASE) * jnp.arange(HALF_ROPE, dtype=F32) / HALF_ROPE)
    ang = pos.astype(F32)[:, None] * inv_freq[None, :]
    cos, sin = jnp.cos(ang), jnp.sin(ang)
    n = pos.shape[0]
    zeros = jnp.zeros((n, LANES - HALF_ROPE), F32)
    cos_t = jnp.concatenate([cos, cos, jnp.ones((n, LANES - ROPE_DIM), F32)], axis=1)
    sin_a = jnp.concatenate([-sin, zeros], axis=1)
    sin_b = jnp.concatenate([jnp.zeros((n, HALF_ROPE), F32), sin, zeros[:, HALF_ROPE:]], axis=1)
    return jnp.concatenate([cos_t, sin_a, sin_b], axis=1)


def _pad_cols(w, left, right):
    return jnp.pad(w, [(0, 0)] * (w.ndim - 1) + [(left, right)])


def _layout_w_in(w_in):
    wq = w_in[:, :Q_COLS].reshape(D_MODEL, MLA_HEADS, MLA_HEAD_DIM + ROPE_DIM)
    wq = jnp.concatenate([wq[..., MLA_HEAD_DIM:], wq[..., :MLA_HEAD_DIM]], axis=-1)
    wq = _pad_cols(wq, 0, HEAD_BLOCK - (MLA_HEAD_DIM + ROPE_DIM)).reshape(D_MODEL, MLA_HEADS * HEAD_BLOCK)
    c0 = Q_COLS
    wc = w_in[:, c0:c0 + KV_LORA]
    wkr = _pad_cols(w_in[:, c0 + KV_LORA:c0 + KV_LORA + ROPE_DIM], 0, HEAD_BLOCK - ROPE_DIM)
    wg = w_in[:, c0 + KV_LORA + ROPE_DIM:]
    return jnp.concatenate([wq, wkr, wc, wg], axis=1).astype(BF16)


def kernel(x_prompt, x_sample, cache_ckv, cache_krope, state_conv, cache_mem_k, cache_mem_v, page_table, mem_prompt, w_in, kv_norm_g, w_uk, w_uv, conv_w, conv_b, conv_ln_g, conv_ln_b, w_o, ln1_g, ln1_b, w_xq, w_xk, w_xv, w_xo, ln2_g, ln2_b, w_up, w_down, ln3_g, ln3_b):
    assert w_in.shape[0] == DEPTH == 1
    B, S, _ = x_prompt.shape
    Bd, T, _ = x_sample.shape
    assert T == 1
    n_pool, page_size, _ = cache_ckv.shape[1:]
    n_pages = page_table.shape[1]
    past_len = n_pages * page_size
    mem_tokens = mem_prompt.shape[1]
    l = 0

    w_all = _layout_w_in(w_in[l])
    g_kv = kv_norm_g[l][None, :]
    wuk_blk = _pad_cols(w_uk[l], ROPE_DIM, HEAD_BLOCK - ROPE_DIM - MLA_HEAD_DIM
                        ).reshape(KV_LORA, MLA_HEADS * HEAD_BLOCK).astype(BF16)
    wuv_t = jnp.pad(jnp.transpose(w_uv[l], (1, 2, 0)), ((0, 0), (0, V_BLOCK - MLA_HEAD_DIM), (0, 0))
                    ).reshape(MLA_HEADS * V_BLOCK, KV_LORA).astype(BF16)
    wq2l = jnp.pad(jnp.transpose(w_uk[l], (1, 2, 0)),
                   ((0, 0), (ROPE_DIM, HEAD_BLOCK - ROPE_DIM - MLA_HEAD_DIM), (0, 0))).astype(BF16)
    wuv_h = jnp.transpose(w_uv[l], (1, 0, 2)).astype(BF16)
    conv_params = jnp.stack([conv_b[l], conv_ln_g[l], conv_ln_b[l]])
    ln = jnp.stack([ln1_g[l], ln1_b[l], ln2_g[l], ln2_b[l], ln3_g[l], ln3_b[l],
                    jnp.zeros_like(ln1_g[l]), jnp.zeros_like(ln1_g[l])])
    wo_b, wxq_b, wxo_b = w_o[l].astype(BF16), w_xq[l].astype(BF16), w_xo[l].astype(BF16)
    wup_b, wdown_b = w_up[l].astype(BF16), w_down[l].astype(BF16)
    w_xkv = jnp.concatenate([w_xk[l], w_xv[l]], axis=1).astype(BF16)

    xp = x_prompt.reshape(B * S, D_MODEL)
    tab_p = _rope_tables(jnp.arange(S, dtype=jnp.int32))
    attn_tile = 512
    q_p, ckv_p, kr_p, u_p, k_p, vt_p = _inproj(xp, w_all, tab_p, g_kv, (wuk_blk, wuv_t), tm=attn_tile, sample=False)
    attn_p = _flash(q_p, k_p, vt_p, batch=B, seq=S, t=attn_tile)
    mk_p, mv_p, mkv_p = _memkv(mem_prompt.reshape(B * mem_tokens, D_MODEL), w_xkv, tm=512)

    xs = x_sample.reshape(Bd, D_MODEL)
    tab_s = _rope_tables(jnp.full((Bd,), past_len, dtype=jnp.int32))
    q_s, ckv_s, kr_s, u_s, qlat_s = _inproj(xs, w_all, tab_s, g_kv, (wq2l,), tm=Bd, sample=True)
    q_rope = q_s.reshape(Bd, MLA_HEADS, HEAD_BLOCK)[:, :, :ROPE_DIM]
    rpool = jnp.swapaxes(cache_krope[l], 1, 2)
    o_lat, conv_p = _decode_and_prompt_conv(page_table, qlat_s.reshape(Bd, MLA_HEADS, KV_LORA), q_rope,
                                            ckv_s[:, None, :], kr_s[:, None, :], cache_ckv, rpool,
                                            u_p, conv_w[l], conv_params, seq=S)
    y_p = _post_prompt(xp, attn_p, conv_p, mkv_p, ln, wo_b, wxq_b, wxo_b, wup_b, wdown_b,
                       batch=B, seq=S, tm=512, mem_tokens=mem_tokens)
    conv_s, new_state = _conv_sample(state_conv[l], u_s[:, None, :], conv_w[l], conv_params, bb=16)
    x1_s, q2_s = _post1_sample(xs, o_lat.reshape(Bd, MLA_HEADS * KV_LORA), conv_s.reshape(Bd, CONV_CH),
                               wuv_h, ln, wo_b, wxq_b)
    o_s = _cross_sample(q2_s.reshape(Bd, MEM_HEADS, MEM_HEAD_DIM), cache_mem_k[l], cache_mem_v[l], bb=4)
    y_s = _post2_sample(x1_s, o_s.reshape(Bd, D_MODEL), ln, wxo_b, wup_b, wdown_b)

    mem_shape = (1, B, mem_tokens, MEM_HEADS, MEM_HEAD_DIM)
    return (y_p.reshape(B, S, D_MODEL),
            y_s.reshape(Bd, T, D_MODEL),
            ckv_p.reshape(1, B, S, KV_LORA),
            jnp.swapaxes(kr_p, 1, 2)[None],
            u_p.reshape(B, S, CONV_CH)[None, :, S - CONV_STATE:, :],
            mk_p.reshape(mem_shape),
            mv_p.reshape(mem_shape),
            ckv_s.reshape(1, Bd, T, KV_LORA),
            kr_s.reshape(1, Bd, T, ROPE_DIM),
            new_state[None])
```

```python
import functools
import math

import jax
import jax.numpy as jnp
from jax import lax
from jax.experimental import pallas as pl
from jax.experimental.pallas import tpu as pltpu

D_MODEL = 1024
MLA_HEADS = 8
MLA_HEAD_DIM = 64
ROPE_DIM = 32
KV_LORA = D_MODEL // 4
ATTN_WIDTH = MLA_HEADS * MLA_HEAD_DIM
CONV_CH = D_MODEL - ATTN_WIDTH
CONV_K = 31
CONV_STATE = CONV_K - 1
MEM_HEADS = 4
MEM_HEAD_DIM = D_MODEL // MEM_HEADS
D_FF = 4 * D_MODEL
ROPE_BASE = 10000.0
LN_EPS = 1e-5
DEPTH = 1
ALPHA = (2.0 * DEPTH) ** 0.25
Q_COLS = MLA_HEADS * (MLA_HEAD_DIM + ROPE_DIM)
LOG2E = math.log2(math.e)
MLA_QSCALE = (MLA_HEAD_DIM + ROPE_DIM) ** -0.5 * LOG2E
MEM_QSCALE = MEM_HEAD_DIM ** -0.5 * LOG2E

LANES = 128
SUBLANES = 8
BF16_SUBLANES = 16
HEAD_BLOCK = LANES
HALF_ROPE = ROPE_DIM // 2
NEG_BIG = -1e30
MIB = 1024 * 1024

F32 = jnp.float32
BF16 = jnp.bfloat16

QK_COLS = (MLA_HEADS + 1) * HEAD_BLOCK
IN_COLS_PADDED = QK_COLS + KV_LORA + 2 * CONV_CH


def _dot(a, b):
    return jnp.dot(a, b, preferred_element_type=F32)


def _dot_nt(a, b):
    return lax.dot_general(a, b, (((1,), (1,)), ((), ())), preferred_element_type=F32)


def _layer_norm(x, g, b):
    mu = jnp.mean(x, axis=-1, keepdims=True)
    xc = x - mu
    var = jnp.mean(xc * xc, axis=-1, keepdims=True)
    return xc * lax.rsqrt(var + LN_EPS) * g + b


def _resident(shape):
    nd = len(shape)
    return pl.BlockSpec(shape, lambda *_: (0,) * nd, pipeline_mode=pl.Buffered(1))


def _rope_block(zb, cos_t, sin_a, sin_b):
    return zb * cos_t + pltpu.roll(zb, LANES - HALF_ROPE, 1) * sin_a + pltpu.roll(zb, HALF_ROPE, 1) * sin_b


def _inproj_common(x_ref, w_ref, tab_ref, g_ref):
    x = x_ref[...].astype(BF16)
    tab = tab_ref[...]
    cos_t, sin_a, sin_b = tab[:, :LANES], tab[:, LANES:2 * LANES], tab[:, 2 * LANES:]
    zq = _dot(x, w_ref[:, :QK_COLS])
    blocks = [_rope_block(zq[:, h * HEAD_BLOCK:(h + 1) * HEAD_BLOCK], cos_t, sin_a, sin_b)
              for h in range(MLA_HEADS + 1)]
    zc = _dot(x, w_ref[:, QK_COLS:QK_COLS + KV_LORA])
    c = zc * lax.rsqrt(jnp.mean(zc * zc, axis=-1, keepdims=True) + LN_EPS) * g_ref[...]
    zg = _dot(x, w_ref[:, QK_COLS + KV_LORA:])
    u = zg[:, :CONV_CH] * jax.nn.sigmoid(zg[:, CONV_CH:])
    return blocks, c, u


V_ONES_ROW = MLA_HEAD_DIM
V_BLOCK = -(-(MLA_HEAD_DIM + 1) // BF16_SUBLANES) * BF16_SUBLANES


def _inproj_prompt_kernel(x_ref, w_ref, tab_ref, g_ref, wuk_ref, wuvt_ref,
                          q_ref, ckv_ref, kr_ref, u_ref, k_ref, vt_ref):
    blocks, c, u = _inproj_common(x_ref, w_ref, tab_ref, g_ref)
    kr = blocks[MLA_HEADS]
    for h in range(MLA_HEADS):
        q_ref[:, h * HEAD_BLOCK:(h + 1) * HEAD_BLOCK] = (blocks[h] * MLA_QSCALE).astype(BF16)
    kr_ref[0] = kr.T[0:ROPE_DIM, :]
    ckv_ref[...] = c
    u_ref[...] = u
    cb = c.astype(BF16)
    kn = _dot(cb, wuk_ref[...])
    for h in range(MLA_HEADS):
        k_ref[:, h * HEAD_BLOCK:(h + 1) * HEAD_BLOCK] = (kn[:, h * HEAD_BLOCK:(h + 1) * HEAD_BLOCK] + kr).astype(BF16)
    vt = _dot_nt(wuvt_ref[...], cb)
    row = lax.broadcasted_iota(jnp.int32, vt.shape, 0)
    vt_ref[0] = jnp.where(row % V_BLOCK == V_ONES_ROW, 1.0, vt).astype(BF16)


def _inproj_sample_kernel(x_ref, w_ref, tab_ref, g_ref, wq2l_ref,
                          q_ref, ckv_ref, kr_ref, u_ref, qlat_ref):
    blocks, c, u = _inproj_common(x_ref, w_ref, tab_ref, g_ref)
    kr = blocks[MLA_HEADS]
    for h in range(MLA_HEADS):
        qb = blocks[h] * MLA_QSCALE
        q_ref[:, h * HEAD_BLOCK:(h + 1) * HEAD_BLOCK] = qb
        qlat_ref[:, h * KV_LORA:(h + 1) * KV_LORA] = _dot(qb.astype(BF16), wq2l_ref[h])
    kr_ref[...] = kr[:, :ROPE_DIM]
    ckv_ref[...] = c
    u_ref[...] = u


def _inproj(x2d, w_all, tab, g, extra_ws, *, tm, sample):
    n = x2d.shape[0]
    n_tiles = n // tm
    n_tab_blocks = tab.shape[0] // tm
    row = lambda width: pl.BlockSpec((tm, width), lambda i: (i, 0))
    in_specs = [row(D_MODEL), _resident(w_all.shape),
                pl.BlockSpec((tm, 3 * LANES), lambda i: (i % n_tab_blocks, 0)),
                _resident(g.shape)] + [_resident(w.shape) for w in extra_ws]
    widths = [(D_MODEL, F32 if sample else BF16), (KV_LORA, F32), (ROPE_DIM, F32), (CONV_CH, F32)]
    widths.append((MLA_HEADS * KV_LORA, F32) if sample else (MLA_HEADS * HEAD_BLOCK, BF16))
    out_specs = [row(w) for w, _ in widths]
    out_shape = [jax.ShapeDtypeStruct((n, w), dt) for w, dt in widths]
    if not sample:
        out_specs.append(pl.BlockSpec((1, MLA_HEADS * V_BLOCK, tm), lambda i: (i, 0, 0)))
        out_shape.append(jax.ShapeDtypeStruct((n_tiles, MLA_HEADS * V_BLOCK, tm), BF16))
        out_specs[2] = pl.BlockSpec((1, ROPE_DIM, tm), lambda i: (i // n_tab_blocks, 0, i % n_tab_blocks))
        out_shape[2] = jax.ShapeDtypeStruct((n_tiles // n_tab_blocks, ROPE_DIM, tab.shape[0]), F32)
    return pl.pallas_call(
        _inproj_sample_kernel if sample else _inproj_prompt_kernel,
        grid=(n_tiles,),
        in_specs=in_specs,
        out_specs=out_specs,
        out_shape=out_shape,
        compiler_params=pltpu.CompilerParams(dimension_semantics=("arbitrary",), vmem_limit_bytes=48 * MIB),
        name="inproj_sample" if sample else "inproj_prompt",
    )(x2d, w_all, tab, g, *extra_ws)


CONV_HALO = 32
CONV_ROWS = 64


def _conv_tail(y, cb, g, b):
    t = _layer_norm(y + cb, g, b)
    return t * jax.nn.sigmoid(t)


def _conv_sample_kernel(s_ref, u_ref, w_ref, p_ref, o_ref, ns_ref):
    st = s_ref[...]
    u = u_ref[...]
    w_hist = w_ref[0:CONV_STATE, :]
    y = jnp.sum(st * w_hist[None, :, :], axis=1, keepdims=True) + u * w_ref[CONV_STATE:CONV_K, :][None]
    o_ref[...] = _conv_tail(y, p_ref[0:1, :][None], p_ref[1:2, :][None], p_ref[2:3, :][None]).astype(o_ref.dtype)
    ns_ref[:, 0:CONV_STATE - 1, :] = s_ref[:, 1:CONV_STATE, :]
    ns_ref[:, CONV_STATE - 1:CONV_STATE, :] = u


def _conv_sample(state, u, conv_w, conv_params, *, bb):
    nb = state.shape[0]
    return pl.pallas_call(
        _conv_sample_kernel,
        grid=(nb // bb,),
        in_specs=[pl.BlockSpec((bb, CONV_STATE, CONV_CH), lambda i: (i, 0, 0)),
                  pl.BlockSpec((bb, 1, CONV_CH), lambda i: (i, 0, 0)),
                  _resident(conv_w.shape), _resident(conv_params.shape)],
        out_specs=[pl.BlockSpec((bb, 1, CONV_CH), lambda i: (i, 0, 0)),
                   pl.BlockSpec((bb, CONV_STATE, CONV_CH), lambda i: (i, 0, 0))],
        out_shape=[jax.ShapeDtypeStruct((nb, 1, CONV_CH), BF16),
                   jax.ShapeDtypeStruct(state.shape, F32)],
        compiler_params=pltpu.CompilerParams(dimension_semantics=("arbitrary",), vmem_limit_bytes=32 * MIB),
        name="conv_sample",
    )(state, u, conv_w, conv_params)


FLASH_HEAD_GROUP = 4


def _flash_kernel(q_ref, k_ref, vt_ref, o_ref, *state, t):
    i = pl.program_id(1)
    kpos = lax.broadcasted_iota(jnp.int32, (t, t), 0)
    qpos = lax.broadcasted_iota(jnp.int32, (t, t), 1)
    causal = kpos <= qpos

    m_refs, acc_refs = state[:MLA_HEADS], state[MLA_HEADS:]
    for h in range(MLA_HEADS):
        m_refs[h][...] = jnp.full((1, t), NEG_BIG, F32)
        acc_refs[h][...] = jnp.zeros((V_BLOCK, t), F32)

    def kv_tile(j, masked):
        rows = pl.ds(pl.multiple_of(j * t, t), t)
        for g in range(0, MLA_HEADS, FLASH_HEAD_GROUP):
            heads = range(g, g + FLASH_HEAD_GROUP)
            hs = {h: slice(h * HEAD_BLOCK, (h + 1) * HEAD_BLOCK) for h in heads}
            vs = {h: slice(h * V_BLOCK, (h + 1) * V_BLOCK) for h in heads}
            sts = {h: _dot_nt(k_ref[rows, hs[h]], q_ref[:, hs[h]]) for h in heads}
            pts, alphas = {}, {}
            for h in heads:
                st = jnp.where(causal, sts[h], NEG_BIG) if masked else sts[h]
                m_old = m_refs[h][...]
                m_new = jnp.maximum(m_old, jnp.max(st, axis=0, keepdims=True))
                m_refs[h][...] = m_new
                alphas[h] = jnp.exp2(m_old - m_new)
                pts[h] = jnp.exp2(st - m_new).astype(BF16)
            for h in heads:
                acc_refs[h][...] = alphas[h] * acc_refs[h][...] + _dot(vt_ref[j, vs[h], :], pts[h])

    def off_diag(j, carry):
        kv_tile(j, False)
        return carry

    lax.fori_loop(0, i, off_diag, 0)
    kv_tile(i, True)

    def finish(h):
        acc = acc_refs[h][...]
        return acc[0:MLA_HEAD_DIM, :] / acc[V_ONES_ROW:V_ONES_ROW + 1, :]

    for pair in range(MLA_HEADS // 2):
        o_t = jnp.concatenate([finish(2 * pair), finish(2 * pair + 1)], axis=0)
        o_ref[:, pair * LANES:(pair + 1) * LANES] = o_t.T.astype(o_ref.dtype)


def _flash(q, k, vt, *, batch, seq, t):
    n_t = seq // t
    return pl.pallas_call(
        functools.partial(_flash_kernel, t=t),
        grid=(batch, n_t),
        in_specs=[pl.BlockSpec((t, q.shape[1]), lambda b, i: (b * n_t + i, 0)),
                  pl.BlockSpec((seq, k.shape[1]), lambda b, i: (b, 0)),
                  pl.BlockSpec((n_t, vt.shape[1], t), lambda b, i: (b, 0, 0))],
        out_specs=pl.BlockSpec((t, ATTN_WIDTH), lambda b, i: (b * n_t + i, 0)),
        out_shape=jax.ShapeDtypeStruct((batch * seq, ATTN_WIDTH), BF16),
        scratch_shapes=[pltpu.VMEM((1, t), F32)] * MLA_HEADS + [pltpu.VMEM((V_BLOCK, t), F32)] * MLA_HEADS,
        compiler_params=pltpu.CompilerParams(dimension_semantics=("arbitrary", "arbitrary"),
                                             vmem_limit_bytes=48 * MIB),
        name="mla_prompt_flash",
    )(q, k, vt)


def _memkv_kernel(m_ref, w_ref, k_ref, v_ref, kvb_ref):
    kv = _dot(m_ref[...].astype(BF16), w_ref[...])
    kvb_ref[...] = kv.astype(BF16)
    for h in range(MEM_HEADS):
        k_ref[:, h, :] = kv[:, h * MEM_HEAD_DIM:(h + 1) * MEM_HEAD_DIM]
        v_ref[:, h, :] = kv[:, D_MODEL + h * MEM_HEAD_DIM:D_MODEL + (h + 1) * MEM_HEAD_DIM]


def _memkv(mem2d, w_xkv, *, tm):
    n = mem2d.shape[0]
    row = pl.BlockSpec((tm, D_MODEL), lambda i: (i, 0))
    cache = pl.BlockSpec((tm, MEM_HEADS, MEM_HEAD_DIM), lambda i: (i, 0, 0))
    return pl.pallas_call(
        _memkv_kernel,
        grid=(n // tm,),
        in_specs=[row, _resident(w_xkv.shape)],
        out_specs=[cache, cache, pl.BlockSpec((tm, 2 * D_MODEL), lambda i: (i, 0))],
        out_shape=[jax.ShapeDtypeStruct((n, MEM_HEADS, MEM_HEAD_DIM), F32)] * 2
        + [jax.ShapeDtypeStruct((n, 2 * D_MODEL), BF16)],
        compiler_params=pltpu.CompilerParams(dimension_semantics=("arbitrary",), vmem_limit_bytes=40 * MIB),
        name="mem_kv",
    )(mem2d, w_xkv)


FFN_CHUNK = 1024


def _mix_ln1_q(xs, mixes, ln_ref, wxq_ref):
    x1s = [_layer_norm(ALPHA * x + mix, ln_ref[0:1, :], ln_ref[1:2, :]) for x, mix in zip(xs, mixes)]
    return x1s, [_dot(x1.astype(BF16), wxq_ref[...]) for x1 in x1s]


def _xo_ffn(x1s, o_heads_per_slab, ln_ref, wxo_ref, wup_ref, wdown_ref):
    atts = []
    for o_heads in o_heads_per_slab:
        att = None
        for h, oh in enumerate(o_heads):
            t = _dot(oh.astype(BF16), wxo_ref[h * MEM_HEAD_DIM:(h + 1) * MEM_HEAD_DIM, :])
            att = t if att is None else att + t
        atts.append(att)
    x2s = [_layer_norm(ALPHA * x1 + att, ln_ref[2:3, :], ln_ref[3:4, :]) for x1, att in zip(x1s, atts)]
    x2bs = [x2.astype(BF16) for x2 in x2s]
    ffns = [None] * len(x2s)
    for c in range(D_FF // FFN_CHUNK):
        cs = slice(c * FFN_CHUNK, (c + 1) * FFN_CHUNK)
        hcols = [jnp.maximum(_dot(x2b, wup_ref[:, cs]), 0.0) for x2b in x2bs]
        for n, hcol in enumerate(hcols):
            t = _dot((hcol * hcol).astype(BF16), wdown_ref[cs, :])
            ffns[n] = t if ffns[n] is None else ffns[n] + t
    return [_layer_norm(ALPHA * x2 + ffn, ln_ref[4:5, :], ln_ref[5:6, :]) for x2, ffn in zip(x2s, ffns)]


POST_SLABS = 2


def _post_prompt_kernel(x_ref, a_ref, c_ref, mkv_ref, ln_ref,
                        wo_ref, wxq_ref, wxo_ref, wup_ref, wdown_ref, y_ref):
    rows_per_slab = x_ref.shape[0] // POST_SLABS
    slabs = [slice(n * rows_per_slab, (n + 1) * rows_per_slab) for n in range(POST_SLABS)]
    head_cols = [slice(h * MEM_HEAD_DIM, (h + 1) * MEM_HEAD_DIM) for h in range(MEM_HEADS)]

    def mix(n):
        r = slabs[n]
        return _dot(a_ref[r, :], wo_ref[0:ATTN_WIDTH, :]) + _dot(c_ref[r, :], wo_ref[ATTN_WIDTH:, :])

    def ln1(n, mix_n):
        return _layer_norm(ALPHA * x_ref[slabs[n], :] + mix_n, ln_ref[0:1, :], ln_ref[1:2, :])

    def xq(x1):
        return _dot(x1.astype(BF16), wxq_ref[...]) * MEM_QSCALE

    def scores(q2):
        return [_dot_nt(q2[:, hs].astype(BF16), mkv_ref[:, hs]) for hs in head_cols]

    def softmax(ss):
        ps = [jnp.exp2(s - jnp.max(s, axis=-1, keepdims=True)) for s in ss]
        return ps, [1.0 / jnp.sum(p, axis=-1, keepdims=True) for p in ps]

    def values(ps, inv_ls):
        return [_dot(p.astype(BF16), mkv_ref[:, D_MODEL + h * MEM_HEAD_DIM:D_MODEL + (h + 1) * MEM_HEAD_DIM]) * inv_l
                for h, (p, inv_l) in enumerate(zip(ps, inv_ls))]

    def xo(o_heads):
        att = None
        for h, oh in enumerate(o_heads):
            t = _dot(oh.astype(BF16), wxo_ref[head_cols[h], :])
            att = t if att is None else att + t
        return att

    def ln2(x1, att):
        return _layer_norm(ALPHA * x1 + att, ln_ref[2:3, :], ln_ref[3:4, :])

    mix_a, mix_b = mix(0), mix(1)
    x1_a = ln1(0, mix_a)
    q2_a = xq(x1_a)
    x1_b = ln1(1, mix_b)
    s_a = scores(q2_a)
    q2_b = xq(x1_b)
    p_a = softmax(s_a)
    o_a = values(*p_a)
    s_b = scores(q2_b)
    att_a = xo(o_a)
    p_b = softmax(s_b)
    o_b = values(*p_b)
    x2_a = ln2(x1_a, att_a)
    att_b = xo(o_b)
    x2_b = ln2(x1_b, att_b)

    x2s = [x2_a, x2_b]
    x2bs = [x2.astype(BF16) for x2 in x2s]
    ffns = [None] * POST_SLABS
    for c in range(D_FF // FFN_CHUNK):
        cs = slice(c * FFN_CHUNK, (c + 1) * FFN_CHUNK)
        hcols = [jnp.maximum(_dot(x2b, wup_ref[:, cs]), 0.0) for x2b in x2bs]
        for n, hcol in enumerate(hcols):
            t = _dot((hcol * hcol).astype(BF16), wdown_ref[cs, :])
            ffns[n] = t if ffns[n] is None else ffns[n] + t
    for n in range(POST_SLABS):
        y_ref[slabs[n], :] = _layer_norm(ALPHA * x2s[n] + ffns[n], ln_ref[4:5, :], ln_ref[5:6, :])


def _post_prompt(x2d, attn, conv, mkv, ln, w_o, w_xq, w_xo, w_up, w_down, *, batch, seq, tm, mem_tokens):
    n_t = seq // tm
    row = lambda width: pl.BlockSpec((tm, width), lambda b, t: (b * n_t + t, 0))
    mem = pl.BlockSpec((mem_tokens, 2 * D_MODEL), lambda b, t: (b, 0))
    weights = (ln, w_o, w_xq, w_xo, w_up, w_down)
    return pl.pallas_call(
        _post_prompt_kernel,
        grid=(batch, n_t),
        in_specs=[row(D_MODEL), row(ATTN_WIDTH), row(CONV_CH), mem] + [_resident(w.shape) for w in weights],
        out_specs=row(D_MODEL),
        out_shape=jax.ShapeDtypeStruct((batch * seq, D_MODEL), F32),
        compiler_params=pltpu.CompilerParams(dimension_semantics=("arbitrary", "arbitrary"),
                                             vmem_limit_bytes=56 * MIB),
        name="post_prompt",
    )(x2d, attn, conv, mkv, *weights)


def _post1_sample_kernel(x_ref, ol_ref, c_ref, wuv_ref, ln_ref, wo_ref, wxq_ref, x1_ref, q2_ref):
    mix = _dot(c_ref[...], wo_ref[ATTN_WIDTH:, :])
    for h in range(MLA_HEADS):
        a_h = _dot(ol_ref[:, h * KV_LORA:(h + 1) * KV_LORA].astype(BF16), wuv_ref[h])
        mix = mix + _dot(a_h.astype(BF16), wo_ref[h * MLA_HEAD_DIM:(h + 1) * MLA_HEAD_DIM, :])
    (x1,), (q2,) = _mix_ln1_q([x_ref[...]], [mix], ln_ref, wxq_ref)
    x1_ref[...] = x1
    q2_ref[...] = q2 * MEM_QSCALE


def _post1_sample(x2d, o_lat, conv, w_uv, ln, w_o, w_xq):
    args = (x2d, o_lat, conv, w_uv, ln, w_o, w_xq)
    n = x2d.shape[0]
    return pl.pallas_call(
        _post1_sample_kernel,
        grid=(1,),
        in_specs=[_resident(a.shape) for a in args],
        out_specs=[pl.BlockSpec((n, D_MODEL), lambda i: (0, 0))] * 2,
        out_shape=[jax.ShapeDtypeStruct((n, D_MODEL), F32)] * 2,
        compiler_params=pltpu.CompilerParams(dimension_semantics=("arbitrary",), vmem_limit_bytes=32 * MIB),
        name="post1_sample",
    )(*args)


def _cross_sample_kernel(q_ref, k_ref, v_ref, o_ref, *, bb):
    pairs = k_ref.shape[1] // 2
    for i in range(bb):
        q = q_ref[i]
        q2 = jnp.concatenate([q, q], axis=0)[None]
        k = k_ref[i].reshape(pairs, 2 * MEM_HEADS, MEM_HEAD_DIM)
        v = v_ref[i].reshape(pairs, 2 * MEM_HEADS, MEM_HEAD_DIM)
        s = jnp.sum(k * q2, axis=-1, keepdims=True)
        m = jnp.max(s, axis=0, keepdims=True)
        m = jnp.maximum(m[:, :MEM_HEADS], m[:, MEM_HEADS:])
        p = jnp.exp2(s - jnp.concatenate([m, m], axis=1))
        l = jnp.sum(p, axis=0)
        o = jnp.sum(p * v, axis=0)
        o_ref[i] = (o[:MEM_HEADS] + o[MEM_HEADS:]) / (l[:MEM_HEADS] + l[MEM_HEADS:])


def _cross_sample(q2, mem_k, mem_v, *, bb):
    nb, m_tok = mem_k.shape[:2]
    q_spec = pl.BlockSpec((bb, MEM_HEADS, MEM_HEAD_DIM), lambda i: (i, 0, 0))
    kv_spec = pl.BlockSpec((bb, m_tok, MEM_HEADS, MEM_HEAD_DIM), lambda i: (i, 0, 0, 0))
    return pl.pallas_call(
        functools.partial(_cross_sample_kernel, bb=bb),
        grid=(nb // bb,),
        in_specs=[q_spec, kv_spec, kv_spec],
        out_specs=q_spec,
        out_shape=jax.ShapeDtypeStruct((nb, MEM_HEADS, MEM_HEAD_DIM), F32),
        compiler_params=pltpu.CompilerParams(dimension_semantics=("arbitrary",), vmem_limit_bytes=40 * MIB),
        name="cross_sample",
    )(q2, mem_k, mem_v)


def _post2_sample_kernel(x1_ref, o_ref, ln_ref, wxo_ref, wup_ref, wdown_ref, y_ref):
    o_heads = [o_ref[:, h * MEM_HEAD_DIM:(h + 1) * MEM_HEAD_DIM] for h in range(MEM_HEADS)]
    (y_ref[...],) = _xo_ffn([x1_ref[...]], [o_heads], ln_ref, wxo_ref, wup_ref, wdown_ref)


def _post2_sample(x1, o, ln, w_xo, w_up, w_down):
    args = (x1, o, ln, w_xo, w_up, w_down)
    n = x1.shape[0]
    return pl.pallas_call(
        _post2_sample_kernel,
        grid=(1,),
        in_specs=[_resident(a.shape) for a in args],
        out_specs=pl.BlockSpec((n, D_MODEL), lambda i: (0, 0)),
        out_shape=jax.ShapeDtypeStruct((n, D_MODEL), F32),
        compiler_params=pltpu.CompilerParams(dimension_semantics=("arbitrary",), vmem_limit_bytes=40 * MIB),
        name="post2_sample",
    )(*args)


DECODE_CHUNK_PAGES = 8
DECODE_ISSUE_UNROLL = 8


def _decode_kernel(pt_ref, ql_ref, qr_ref, cn_ref, krn_ref, u_ref, cw_ref, cp_ref, cpool_ref, rpool_ref,
                   o_ref, conv_ref, cbuf, rbuf, sem, ext_ref, sh_ref, *, n_pages, page_size, conv_tiles_per_seq):
    b = pl.program_id(0)
    nb = pl.num_programs(0)
    conv_rows = u_ref.shape[0]

    @pl.when(b % conv_tiles_per_seq == 0)
    def _():
        ext_ref[0:CONV_HALO, :] = jnp.zeros((CONV_HALO, CONV_CH), F32)

    @pl.when(b % conv_tiles_per_seq != 0)
    def _():
        ext_ref[0:CONV_HALO, :] = ext_ref[conv_rows:conv_rows + CONV_HALO, :]

    ext_ref[CONV_HALO:CONV_HALO + conv_rows, :] = u_ref[...]
    conv_off = CONV_HALO - CONV_STATE
    for r in range(1, SUBLANES):
        sh_ref[r - 1] = ext_ref[r:r + sh_ref.shape[1], :]

    def conv_chunk(r0):
        parts = []
        for c0 in range(0, CONV_CH, LANES):
            ch = slice(c0, c0 + LANES)
            acc = jnp.zeros((CONV_ROWS, LANES), F32)
            for k in range(CONV_K):
                a, r = divmod(k + conv_off, SUBLANES)
                lo = r0 + SUBLANES * a
                src = ext_ref[lo:lo + CONV_ROWS, ch] if r == 0 else sh_ref[r - 1, lo:lo + CONV_ROWS, ch]
                acc = acc + cw_ref[k:k + 1, ch] * src
            parts.append(acc)
        y = jnp.concatenate(parts, axis=1)
        conv_ref[r0:r0 + CONV_ROWS, :] = _conv_tail(y, cp_ref[0:1, :], cp_ref[1:2, :], cp_ref[2:3, :]
                                                    ).astype(conv_ref.dtype)

    conv_chunks = list(range(0, conv_rows, CONV_ROWS))

    def start_all(bi, slot):
        def body(p, carry):
            pg = pt_ref[bi * n_pages + p]
            pltpu.make_async_copy(cpool_ref.at[0, pg], cbuf.at[slot, p], sem.at[slot, 0]).start()
            pltpu.make_async_copy(rpool_ref.at[pg], rbuf.at[slot, p], sem.at[slot, 1]).start()
            return carry
        lax.fori_loop(0, n_pages, body, 0, unroll=DECODE_ISSUE_UNROLL)

    def wait_all(slot):
        pltpu.make_async_copy(cpool_ref.at[0, pl.ds(0, n_pages)], cbuf.at[slot], sem.at[slot, 0]).wait()
        pltpu.make_async_copy(rpool_ref.at[pl.ds(0, n_pages)], rbuf.at[slot], sem.at[slot, 1]).wait()

    slot = b % 2

    @pl.when(b == 0)
    def _():
        start_all(0, 0)

    @pl.when(b + 1 < nb)
    def _():
        start_all(b + 1, 1 - slot)

    wait_all(slot)

    ql32 = ql_ref[0]
    qr32 = qr_ref[0]
    c_new = cn_ref[0]
    kr_new = krn_ref[0]
    ql, qr = ql32.astype(BF16), qr32.astype(BF16)

    def chunk(c):
        pages = range(c * DECODE_CHUNK_PAGES, (c + 1) * DECODE_CHUNK_PAGES)
        cb = cbuf[slot, pages.start:pages.stop].reshape(DECODE_CHUNK_PAGES * page_size, KV_LORA).astype(BF16)
        rb = jnp.concatenate([rbuf[slot, p] for p in pages], axis=1).astype(BF16)
        return cb, _dot_nt(ql, cb) + _dot(qr, rb)

    m = (jnp.sum(ql32 * c_new, axis=-1, keepdims=True)
         + jnp.sum(qr32 * kr_new, axis=-1, keepdims=True))
    l = jnp.ones_like(m)
    o = jnp.broadcast_to(c_new, (MLA_HEADS, KV_LORA))
    n_chunks = n_pages // DECODE_CHUNK_PAGES
    nxt = chunk(0)
    for c in range(n_chunks):
        cb, s = nxt
        if c + 1 < n_chunks:
            nxt = chunk(c + 1)
        m_new = jnp.maximum(m, jnp.max(s, axis=-1, keepdims=True))
        alpha = jnp.exp2(m - m_new)
        p = jnp.exp2(s - m_new)
        l = alpha * l + jnp.sum(p, axis=-1, keepdims=True)
        o = alpha * o + _dot(p.astype(BF16), cb)
        m = m_new
    o_ref[0] = o / l
    for r0 in conv_chunks:
        conv_chunk(r0)


def _decode_and_prompt_conv(page_table, q_lat, q_rope, c_new, kr_new, cpool, rpool, u2d, conv_w, conv_params, *, seq):
    nb, n_pages = page_table.shape
    page_size = cpool.shape[2]
    assert n_pages % DECODE_CHUNK_PAGES == 0 and n_pages % DECODE_ISSUE_UNROLL == 0
    conv_rows, rem = divmod(u2d.shape[0], nb)
    assert rem == 0 and seq % conv_rows == 0 and conv_rows % CONV_ROWS == 0 and conv_rows >= CONV_HALO
    per_seq = lambda *tail: pl.BlockSpec((1,) + tail, lambda b, pt: (b, 0, 0))
    conv_tile = pl.BlockSpec((conv_rows, CONV_CH), lambda b, pt: (b, 0))
    whole = lambda a: pl.BlockSpec(a.shape, lambda b, pt: (0, 0))
    grid_spec = pltpu.PrefetchScalarGridSpec(
        num_scalar_prefetch=1,
        grid=(nb,),
        in_specs=[per_seq(MLA_HEADS, KV_LORA), per_seq(MLA_HEADS, ROPE_DIM), per_seq(1, KV_LORA), per_seq(1, ROPE_DIM),
                  conv_tile, whole(conv_w), whole(conv_params),
                  pl.BlockSpec(memory_space=pl.ANY),
                  pl.BlockSpec(memory_space=pl.ANY)],
        out_specs=[per_seq(MLA_HEADS, KV_LORA), conv_tile],
        scratch_shapes=[pltpu.VMEM((2, n_pages, page_size, KV_LORA), F32),
                        pltpu.VMEM((2, n_pages, ROPE_DIM, page_size), F32),
                        pltpu.SemaphoreType.DMA((2, 2)),
                        pltpu.VMEM((conv_rows + CONV_HALO, CONV_CH), F32),
                        pltpu.VMEM((SUBLANES - 1, conv_rows + CONV_HALO - SUBLANES, CONV_CH), F32)],
    )
    return pl.pallas_call(
        functools.partial(_decode_kernel, n_pages=n_pages, page_size=page_size, conv_tiles_per_seq=seq // conv_rows),
        grid_spec=grid_spec,
        out_shape=[jax.ShapeDtypeStruct((nb, MLA_HEADS, KV_LORA), F32),
                   jax.ShapeDtypeStruct((u2d.shape[0], CONV_CH), BF16)],
        compiler_params=pltpu.CompilerParams(dimension_semantics=("arbitrary",), vmem_limit_bytes=48 * MIB),
        name="mla_sample_paged_and_prompt_conv",
    )(page_table.reshape(-1), q_lat, q_rope, c_new, kr_new, u2d, conv_w, conv_params, cpool, rpool)


def _rope_tables(pos):
    inv_freq = jnp.exp(-math.log(ROPE_BASE) * jnp.arange(HALF_ROPE, dtype=F32) / HALF_ROPE)
    ang = pos.astype(F32)[:, None] * inv_freq[None, :]
    cos, sin = jnp.cos(ang), jnp.sin(ang)
    n = pos.shape[0]
    zeros = jnp.zeros((n, LANES - HALF_ROPE), F32)
    cos_t = jnp.concatenate([cos, cos, jnp.ones((n, LANES - ROPE_DIM), F32)], axis=1)
    sin_a = jnp.concatenate([-sin, zeros], axis=1)
    sin_b = jnp.concatenate([jnp.zeros((n, HALF_ROPE), F32), sin, zeros[:, HALF_ROPE:]], axis=1)
    return jnp.concatenate([cos_t, sin_a, sin_b], axis=1)


def _pad_cols(w, left, right):
    return jnp.pad(w, [(0, 0)] * (w.ndim - 1) + [(left, right)])


def _layout_w_in(w_in):
    wq = w_in[:, :Q_COLS].reshape(D_MODEL, MLA_HEADS, MLA_HEAD_DIM + ROPE_DIM)
    wq = jnp.concatenate([wq[..., MLA_HEAD_DIM:], wq[..., :MLA_HEAD_DIM]], axis=-1)
    wq = _pad_cols(wq, 0, HEAD_BLOCK - (MLA_HEAD_DIM + ROPE_DIM)).reshape(D_MODEL, MLA_HEADS * HEAD_BLOCK)
    c0 = Q_COLS
    wc = w_in[:, c0:c0 + KV_LORA]
    wkr = _pad_cols(w_in[:, c0 + KV_LORA:c0 + KV_LORA + ROPE_DIM], 0, HEAD_BLOCK - ROPE_DIM)
    wg = w_in[:, c0 + KV_LORA + ROPE_DIM:]
    return jnp.concatenate([wq, wkr, wc, wg], axis=1).astype(BF16)


def kernel(x_prompt, x_sample, cache_ckv, cache_krope, state_conv, cache_mem_k, cache_mem_v, page_table, mem_prompt, w_in, kv_norm_g, w_uk, w_uv, conv_w, conv_b, conv_ln_g, conv_ln_b, w_o, ln1_g, ln1_b, w_xq, w_xk, w_xv, w_xo, ln2_g, ln2_b, w_up, w_down, ln3_g, ln3_b):
    assert w_in.shape[0] == DEPTH == 1
    B, S, _ = x_prompt.shape
    Bd, T, _ = x_sample.shape
    assert T == 1
    n_pool, page_size, _ = cache_ckv.shape[1:]
    n_pages = page_table.shape[1]
    past_len = n_pages * page_size
    mem_tokens = mem_prompt.shape[1]
    l = 0

    w_all = _layout_w_in(w_in[l])
    g_kv = kv_norm_g[l][None, :]
    wuk_blk = _pad_cols(w_uk[l], ROPE_DIM, HEAD_BLOCK - ROPE_DIM - MLA_HEAD_DIM
                        ).reshape(KV_LORA, MLA_HEADS * HEAD_BLOCK).astype(BF16)
    wuv_t = jnp.pad(jnp.transpose(w_uv[l], (1, 2, 0)), ((0, 0), (0, V_BLOCK - MLA_HEAD_DIM), (0, 0))
                    ).reshape(MLA_HEADS * V_BLOCK, KV_LORA).astype(BF16)
    wq2l = jnp.pad(jnp.transpose(w_uk[l], (1, 2, 0)),
                   ((0, 0), (ROPE_DIM, HEAD_BLOCK - ROPE_DIM - MLA_HEAD_DIM), (0, 0))).astype(BF16)
    wuv_h = jnp.transpose(w_uv[l], (1, 0, 2)).astype(BF16)
    conv_params = jnp.stack([conv_b[l], conv_ln_g[l], conv_ln_b[l]])
    ln = jnp.stack([ln1_g[l], ln1_b[l], ln2_g[l], ln2_b[l], ln3_g[l], ln3_b[l],
                    jnp.zeros_like(ln1_g[l]), jnp.zeros_like(ln1_g[l])])
    wo_b, wxq_b, wxo_b = w_o[l].astype(BF16), w_xq[l].astype(BF16), w_xo[l].astype(BF16)
    wup_b, wdown_b = w_up[l].astype(BF16), w_down[l].astype(BF16)
    w_xkv = jnp.concatenate([w_xk[l], w_xv[l]], axis=1).astype(BF16)

    xp = x_prompt.reshape(B * S, D_MODEL)
    tab_p = _rope_tables(jnp.arange(S, dtype=jnp.int32))
    attn_tile = 512
    q_p, ckv_p, kr_p, u_p, k_p, vt_p = _inproj(xp, w_all, tab_p, g_kv, (wuk_blk, wuv_t), tm=attn_tile, sample=False)
    attn_p = _flash(q_p, k_p, vt_p, batch=B, seq=S, t=attn_tile)
    mk_p, mv_p, mkv_p = _memkv(mem_prompt.reshape(B * mem_tokens, D_MODEL), w_xkv, tm=512)

    xs = x_sample.reshape(Bd, D_MODEL)
    tab_s = _rope_tables(jnp.full((Bd,), past_len, dtype=jnp.int32))
    q_s, ckv_s, kr_s, u_s, qlat_s = _inproj(xs, w_all, tab_s, g_kv, (wq2l,), tm=Bd, sample=True)
    q_rope = q_s.reshape(Bd, MLA_HEADS, HEAD_BLOCK)[:, :, :ROPE_DIM]
    rpool = jnp.swapaxes(cache_krope[l], 1, 2)
    o_lat, conv_p = _decode_and_prompt_conv(page_table, qlat_s.reshape(Bd, MLA_HEADS, KV_LORA), q_rope,
                                            ckv_s[:, None, :], kr_s[:, None, :], cache_ckv, rpool,
                                            u_p, conv_w[l], conv_params, seq=S)
    y_p = _post_prompt(xp, attn_p, conv_p, mkv_p, ln, wo_b, wxq_b, wxo_b, wup_b, wdown_b,
                       batch=B, seq=S, tm=512, mem_tokens=mem_tokens)
    conv_s, new_state = _conv_sample(state_conv[l], u_s[:, None, :], conv_w[l], conv_params, bb=16)
    x1_s, q2_s = _post1_sample(xs, o_lat.reshape(Bd, MLA_HEADS * KV_LORA), conv_s.reshape(Bd, CONV_CH),
                               wuv_h, ln, wo_b, wxq_b)
    o_s = _cross_sample(q2_s.reshape(Bd, MEM_HEADS, MEM_HEAD_DIM), cache_mem_k[l], cache_mem_v[l], bb=4)
    y_s = _post2_sample(x1_s, o_s.reshape(Bd, D_MODEL), ln, wxo_b, wup_b, wdown_b)

    mem_shape = (1, B, mem_tokens, MEM_HEADS, MEM_HEAD_DIM)
    return (y_p.reshape(B, S, D_MODEL),
            y_s.reshape(Bd, T, D_MODEL),
            ckv_p.reshape(1, B, S, KV_LORA),
            jnp.swapaxes(kr_p, 1, 2)[None],
            u_p.reshape(B, S, CONV_CH)[None, :, S - CONV_STATE:, :],
            mk_p.reshape(mem_shape),
            mv_p.reshape(mem_shape),
            ckv_s.reshape(1, Bd, T, KV_LORA),
            kr_s.reshape(1, Bd, T, ROPE_DIM),
            new_state[None])
```

```python
import functools
import math

import jax
import jax.numpy as jnp
from jax import lax
from jax.experimental import pallas as pl
from jax.experimental.pallas import tpu as pltpu

D_MODEL = 1024
MLA_HEADS = 8
MLA_HEAD_DIM = 64
ROPE_DIM = 32
KV_LORA = D_MODEL // 4
ATTN_WIDTH = MLA_HEADS * MLA_HEAD_DIM
CONV_CH = D_MODEL - ATTN_WIDTH
CONV_K = 31
CONV_STATE = CONV_K - 1
MEM_HEADS = 4
MEM_HEAD_DIM = D_MODEL // MEM_HEADS
D_FF = 4 * D_MODEL
ROPE_BASE = 10000.0
LN_EPS = 1e-5
DEPTH = 1
ALPHA = (2.0 * DEPTH) ** 0.25
Q_COLS = MLA_HEADS * (MLA_HEAD_DIM + ROPE_DIM)
LOG2E = math.log2(math.e)
MLA_QSCALE = (MLA_HEAD_DIM + ROPE_DIM) ** -0.5 * LOG2E
MEM_QSCALE = MEM_HEAD_DIM ** -0.5 * LOG2E

LANES = 128
SUBLANES = 8
BF16_SUBLANES = 16
HEAD_BLOCK = LANES
HALF_ROPE = ROPE_DIM // 2
NEG_BIG = -1e30
MIB = 1024 * 1024

F32 = jnp.float32
BF16 = jnp.bfloat16

QK_COLS = (MLA_HEADS + 1) * HEAD_BLOCK
IN_COLS_PADDED = QK_COLS + KV_LORA + 2 * CONV_CH


def _dot(a, b):
    return jnp.dot(a, b, preferred_element_type=F32)


def _dot_nt(a, b):
    return lax.dot_general(a, b, (((1,), (1,)), ((), ())), preferred_element_type=F32)


def _layer_norm(x, g, b):
    mu = jnp.mean(x, axis=-1, keepdims=True)
    xc = x - mu
    var = jnp.mean(xc * xc, axis=-1, keepdims=True)
    return xc * lax.rsqrt(var + LN_EPS) * g + b


def _resident(shape):
    nd = len(shape)
    return pl.BlockSpec(shape, lambda *_: (0,) * nd, pipeline_mode=pl.Buffered(1))


def _rope_block(zb, cos_t, sin_a, sin_b):
    return zb * cos_t + pltpu.roll(zb, LANES - HALF_ROPE, 1) * sin_a + pltpu.roll(zb, HALF_ROPE, 1) * sin_b


def _inproj_common(x_ref, w_ref, tab_ref, g_ref):
    x = x_ref[...].astype(BF16)
    tab = tab_ref[...]
    cos_t, sin_a, sin_b = tab[:, :LANES], tab[:, LANES:2 * LANES], tab[:, 2 * LANES:]
    zq = _dot(x, w_ref[:, :QK_COLS])
    blocks = [_rope_block(zq[:, h * HEAD_BLOCK:(h + 1) * HEAD_BLOCK], cos_t, sin_a, sin_b)
              for h in range(MLA_HEADS + 1)]
    zc = _dot(x, w_ref[:, QK_COLS:QK_COLS + KV_LORA])
    c = zc * lax.rsqrt(jnp.mean(zc * zc, axis=-1, keepdims=True) + LN_EPS) * g_ref[...]
    zg = _dot(x, w_ref[:, QK_COLS + KV_LORA:])
    u = zg[:, :CONV_CH] * jax.nn.sigmoid(zg[:, CONV_CH:])
    return blocks, c, u


V_ONES_ROW = MLA_HEAD_DIM
V_BLOCK = -(-(MLA_HEAD_DIM + 1) // BF16_SUBLANES) * BF16_SUBLANES


def _inproj_prompt_kernel(x_ref, w_ref, tab_ref, g_ref, wuk_ref, wuvt_ref,
                          q_ref, ckv_ref, kr_ref, u_ref, k_ref, vt_ref):
    blocks, c, u = _inproj_common(x_ref, w_ref, tab_ref, g_ref)
    kr = blocks[MLA_HEADS]
    for h in range(MLA_HEADS):
        q_ref[:, h * HEAD_BLOCK:(h + 1) * HEAD_BLOCK] = (blocks[h] * MLA_QSCALE).astype(BF16)
    kr_ref[0] = kr.T[0:ROPE_DIM, :]
    ckv_ref[...] = c
    u_ref[...] = u
    cb = c.astype(BF16)
    kn = _dot(cb, wuk_ref[...])
    for h in range(MLA_HEADS):
        k_ref[:, h * HEAD_BLOCK:(h + 1) * HEAD_BLOCK] = (kn[:, h * HEAD_BLOCK:(h + 1) * HEAD_BLOCK] + kr).astype(BF16)
    vt = _dot_nt(wuvt_ref[...], cb)
    row = lax.broadcasted_iota(jnp.int32, vt.shape, 0)
    vt_ref[0] = jnp.where(row % V_BLOCK == V_ONES_ROW, 1.0, vt).astype(BF16)


def _inproj_sample_kernel(x_ref, w_ref, tab_ref, g_ref, wq2l_ref,
                          q_ref, ckv_ref, kr_ref, u_ref, qlat_ref):
    blocks, c, u = _inproj_common(x_ref, w_ref, tab_ref, g_ref)
    kr = blocks[MLA_HEADS]
    for h in range(MLA_HEADS):
        qb = blocks[h] * MLA_QSCALE
        q_ref[:, h * HEAD_BLOCK:(h + 1) * HEAD_BLOCK] = qb
        qlat_ref[:, h * KV_LORA:(h + 1) * KV_LORA] = _dot(qb.astype(BF16), wq2l_ref[h])
    kr_ref[...] = kr[:, :ROPE_DIM]
    ckv_ref[...] = c
    u_ref[...] = u


def _inproj(x2d, w_all, tab, g, extra_ws, *, tm, sample):
    n = x2d.shape[0]
    n_tiles = n // tm
    n_tab_blocks = tab.shape[0] // tm
    row = lambda width: pl.BlockSpec((tm, width), lambda i: (i, 0))
    in_specs = [row(D_MODEL), _resident(w_all.shape),
                pl.BlockSpec((tm, 3 * LANES), lambda i: (i % n_tab_blocks, 0)),
                _resident(g.shape)] + [_resident(w.shape) for w in extra_ws]
    widths = [(D_MODEL, F32 if sample else BF16), (KV_LORA, F32), (ROPE_DIM, F32), (CONV_CH, F32)]
    widths.append((MLA_HEADS * KV_LORA, F32) if sample else (MLA_HEADS * HEAD_BLOCK, BF16))
    out_specs = [row(w) for w, _ in widths]
    out_shape = [jax.ShapeDtypeStruct((n, w), dt) for w, dt in widths]
    if not sample:
        out_specs.append(pl.BlockSpec((1, MLA_HEADS * V_BLOCK, tm), lambda i: (i, 0, 0)))
        out_shape.append(jax.ShapeDtypeStruct((n_tiles, MLA_HEADS * V_BLOCK, tm), BF16))
        out_specs[2] = pl.BlockSpec((1, ROPE_DIM, tm), lambda i: (i // n_tab_blocks, 0, i % n_tab_blocks))
        out_shape[2] = jax.ShapeDtypeStruct((n_tiles // n_tab_blocks, ROPE_DIM, tab.shape[0]), F32)
    return pl.pallas_call(
        _inproj_sample_kernel if sample else _inproj_prompt_kernel,
        grid=(n_tiles,),
        in_specs=in_specs,
        out_specs=out_specs,
        out_shape=out_shape,
        compiler_params=pltpu.CompilerParams(dimension_semantics=("arbitrary",), vmem_limit_bytes=48 * MIB),
        name="inproj_sample" if sample else "inproj_prompt",
    )(x2d, w_all, tab, g, *extra_ws)


CONV_HALO = 32
CONV_ROWS = 64


def _conv_tail(y, cb, g, b):
    t = _layer_norm(y + cb, g, b)
    return t * jax.nn.sigmoid(t)


def _conv_sample_kernel(s_ref, u_ref, w_ref, p_ref, o_ref, ns_ref):
    st = s_ref[...]
    u = u_ref[...]
    w_hist = w_ref[0:CONV_STATE, :]
    y = jnp.sum(st * w_hist[None, :, :], axis=1, keepdims=True) + u * w_ref[CONV_STATE:CONV_K, :][None]
    o_ref[...] = _conv_tail(y, p_ref[0:1, :][None], p_ref[1:2, :][None], p_ref[2:3, :][None]).astype(o_ref.dtype)
    ns_ref[:, 0:CONV_STATE - 1, :] = s_ref[:, 1:CONV_STATE, :]
    ns_ref[:, CONV_STATE - 1:CONV_STATE, :] = u


def _conv_sample(state, u, conv_w, conv_params, *, bb):
    nb = state.shape[0]
    return pl.pallas_call(
        _conv_sample_kernel,
        grid=(nb // bb,),
        in_specs=[pl.BlockSpec((bb, CONV_STATE, CONV_CH), lambda i: (i, 0, 0)),
                  pl.BlockSpec((bb, 1, CONV_CH), lambda i: (i, 0, 0)),
                  _resident(conv_w.shape), _resident(conv_params.shape)],
        out_specs=[pl.BlockSpec((bb, 1, CONV_CH), lambda i: (i, 0, 0)),
                   pl.BlockSpec((bb, CONV_STATE, CONV_CH), lambda i: (i, 0, 0))],
        out_shape=[jax.ShapeDtypeStruct((nb, 1, CONV_CH), BF16),
                   jax.ShapeDtypeStruct(state.shape, F32)],
        compiler_params=pltpu.CompilerParams(dimension_semantics=("arbitrary",), vmem_limit_bytes=32 * MIB),
        name="conv_sample",
    )(state, u, conv_w, conv_params)


FLASH_HEAD_GROUP = 4


def _flash_kernel(q_ref, k_ref, vt_ref, o_ref, *state, t):
    i = pl.program_id(1)
    kpos = lax.broadcasted_iota(jnp.int32, (t, t), 0)
    qpos = lax.broadcasted_iota(jnp.int32, (t, t), 1)
    causal = kpos <= qpos

    m_refs, acc_refs = state[:MLA_HEADS], state[MLA_HEADS:]
    for h in range(MLA_HEADS):
        m_refs[h][...] = jnp.full((1, t), NEG_BIG, F32)
        acc_refs[h][...] = jnp.zeros((V_BLOCK, t), F32)

    def kv_tile(j, masked):
        rows = pl.ds(pl.multiple_of(j * t, t), t)
        for g in range(0, MLA_HEADS, FLASH_HEAD_GROUP):
            heads = range(g, g + FLASH_HEAD_GROUP)
            hs = {h: slice(h * HEAD_BLOCK, (h + 1) * HEAD_BLOCK) for h in heads}
            vs = {h: slice(h * V_BLOCK, (h + 1) * V_BLOCK) for h in heads}
            sts = {h: _dot_nt(k_ref[rows, hs[h]], q_ref[:, hs[h]]) for h in heads}
            pts, alphas = {}, {}
            for h in heads:
                st = jnp.where(causal, sts[h], NEG_BIG) if masked else sts[h]
                m_old = m_refs[h][...]
                m_new = jnp.maximum(m_old, jnp.max(st, axis=0, keepdims=True))
                m_refs[h][...] = m_new
                alphas[h] = jnp.exp2(m_old - m_new)
                pts[h] = jnp.exp2(st - m_new).astype(BF16)
            for h in heads:
                acc_refs[h][...] = alphas[h] * acc_refs[h][...] + _dot(vt_ref[j, vs[h], :], pts[h])

    def off_diag(j, carry):
        kv_tile(j, False)
        return carry

    lax.fori_loop(0, i, off_diag, 0)
    kv_tile(i, True)

    def finish(h):
        acc = acc_refs[h][...]
        return acc[0:MLA_HEAD_DIM, :] / acc[V_ONES_ROW:V_ONES_ROW + 1, :]

    for pair in range(MLA_HEADS // 2):
        o_t = jnp.concatenate([finish(2 * pair), finish(2 * pair + 1)], axis=0)
        o_ref[:, pair * LANES:(pair + 1) * LANES] = o_t.T.astype(o_ref.dtype)


def _flash(q, k, vt, *, batch, seq, t):
    n_t = seq // t
    return pl.pallas_call(
        functools.partial(_flash_kernel, t=t),
        grid=(batch, n_t),
        in_specs=[pl.BlockSpec((t, q.shape[1]), lambda b, i: (b * n_t + i, 0)),
                  pl.BlockSpec((seq, k.shape[1]), lambda b, i: (b, 0)),
                  pl.BlockSpec((n_t, vt.shape[1], t), lambda b, i: (b, 0, 0))],
        out_specs=pl.BlockSpec((t, ATTN_WIDTH), lambda b, i: (b * n_t + i, 0)),
        out_shape=jax.ShapeDtypeStruct((batch * seq, ATTN_WIDTH), BF16),
        scratch_shapes=[pltpu.VMEM((1, t), F32)] * MLA_HEADS + [pltpu.VMEM((V_BLOCK, t), F32)] * MLA_HEADS,
        compiler_params=pltpu.CompilerParams(dimension_semantics=("arbitrary", "arbitrary"),
                                             vmem_limit_bytes=48 * MIB),
        name="mla_prompt_flash",
    )(q, k, vt)


def _memkv_kernel(m_ref, w_ref, k_ref, v_ref, kvb_ref):
    kv = _dot(m_ref[...].astype(BF16), w_ref[...])
    kvb_ref[...] = kv.astype(BF16)
    for h in range(MEM_HEADS):
        k_ref[:, h, :] = kv[:, h * MEM_HEAD_DIM:(h + 1) * MEM_HEAD_DIM]
        v_ref[:, h, :] = kv[:, D_MODEL + h * MEM_HEAD_DIM:D_MODEL + (h + 1) * MEM_HEAD_DIM]


def _memkv(mem2d, w_xkv, *, tm):
    n = mem2d.shape[0]
    row = pl.BlockSpec((tm, D_MODEL), lambda i: (i, 0))
    cache = pl.BlockSpec((tm, MEM_HEADS, MEM_HEAD_DIM), lambda i: (i, 0, 0))
    return pl.pallas_call(
        _memkv_kernel,
        grid=(n // tm,),
        in_specs=[row, _resident(w_xkv.shape)],
        out_specs=[cache, cache, pl.BlockSpec((tm, 2 * D_MODEL), lambda i: (i, 0))],
        out_shape=[jax.ShapeDtypeStruct((n, MEM_HEADS, MEM_HEAD_DIM), F32)] * 2
        + [jax.ShapeDtypeStruct((n, 2 * D_MODEL), BF16)],
        compiler_params=pltpu.CompilerParams(dimension_semantics=("arbitrary",), vmem_limit_bytes=40 * MIB),
        name="mem_kv",
    )(mem2d, w_xkv)


FFN_CHUNK = 1024


def _mix_ln1_q(xs, mixes, ln_ref, wxq_ref):
    x1s = [_layer_norm(ALPHA * x + mix, ln_ref[0:1, :], ln_ref[1:2, :]) for x, mix in zip(xs, mixes)]
    return x1s, [_dot(x1.astype(BF16), wxq_ref[...]) for x1 in x1s]


def _xo_ffn(x1s, o_heads_per_slab, ln_ref, wxo_ref, wup_ref, wdown_ref):
    atts = []
    for o_heads in o_heads_per_slab:
        att = None
        for h, oh in enumerate(o_heads):
            t = _dot(oh.astype(BF16), wxo_ref[h * MEM_HEAD_DIM:(h + 1) * MEM_HEAD_DIM, :])
            att = t if att is None else att + t
        atts.append(att)
    x2s = [_layer_norm(ALPHA * x1 + att, ln_ref[2:3, :], ln_ref[3:4, :]) for x1, att in zip(x1s, atts)]
    x2bs = [x2.astype(BF16) for x2 in x2s]
    ffns = [None] * len(x2s)
    for c in range(D_FF // FFN_CHUNK):
        cs = slice(c * FFN_CHUNK, (c + 1) * FFN_CHUNK)
        hcols = [jnp.maximum(_dot(x2b, wup_ref[:, cs]), 0.0) for x2b in x2bs]
        for n, hcol in enumerate(hcols):
            t = _dot((hcol * hcol).astype(BF16), wdown_ref[cs, :])
            ffns[n] = t if ffns[n] is None else ffns[n] + t
    return [_layer_norm(ALPHA * x2 + ffn, ln_ref[4:5, :], ln_ref[5:6, :]) for x2, ffn in zip(x2s, ffns)]


POST_SLABS = 2


def _post_prompt_kernel(x_ref, a_ref, c_ref, mkv_ref, ln_ref,
                        wo_ref, wxq_ref, wxo_ref, wup_ref, wdown_ref, y_ref):
    rows_per_slab = x_ref.shape[0] // POST_SLABS
    slabs = [slice(n * rows_per_slab, (n + 1) * rows_per_slab) for n in range(POST_SLABS)]
    head_cols = [slice(h * MEM_HEAD_DIM, (h + 1) * MEM_HEAD_DIM) for h in range(MEM_HEADS)]

    def mix(n):
        r = slabs[n]
        return _dot(a_ref[r, :], wo_ref[0:ATTN_WIDTH, :]) + _dot(c_ref[r, :], wo_ref[ATTN_WIDTH:, :])

    def ln1(n, mix_n):
        return _layer_norm(ALPHA * x_ref[slabs[n], :] + mix_n, ln_ref[0:1, :], ln_ref[1:2, :])

    def xq(x1):
        return _dot(x1.astype(BF16), wxq_ref[...]) * MEM_QSCALE

    def scores(q2):
        return [_dot_nt(q2[:, hs].astype(BF16), mkv_ref[:, hs]) for hs in head_cols]

    def softmax(ss):
        ps = [jnp.exp2(s - jnp.max(s, axis=-1, keepdims=True)) for s in ss]
        return ps, [1.0 / jnp.sum(p, axis=-1, keepdims=True) for p in ps]

    def values(ps, inv_ls):
        return [_dot(p.astype(BF16), mkv_ref[:, D_MODEL + h * MEM_HEAD_DIM:D_MODEL + (h + 1) * MEM_HEAD_DIM]) * inv_l
                for h, (p, inv_l) in enumerate(zip(ps, inv_ls))]

    def xo(o_heads):
        att = None
        for h, oh in enumerate(o_heads):
            t = _dot(oh.astype(BF16), wxo_ref[head_cols[h], :])
            att = t if att is None else att + t
        return att

    def ln2(x1, att):
        return _layer_norm(ALPHA * x1 + att, ln_ref[2:3, :], ln_ref[3:4, :])

    mix_a, mix_b = mix(0), mix(1)
    x1_a = ln1(0, mix_a)
    q2_a = xq(x1_a)
    x1_b = ln1(1, mix_b)
    s_a = scores(q2_a)
    q2_b = xq(x1_b)
    p_a = softmax(s_a)
    o_a = values(*p_a)
    s_b = scores(q2_b)
    att_a = xo(o_a)
    p_b = softmax(s_b)
    o_b = values(*p_b)
    x2_a = ln2(x1_a, att_a)
    att_b = xo(o_b)
    x2_b = ln2(x1_b, att_b)

    x2s = [x2_a, x2_b]
    x2bs = [x2.astype(BF16) for x2 in x2s]
    ffns = [None] * POST_SLABS
    for c in range(D_FF // FFN_CHUNK):
        cs = slice(c * FFN_CHUNK, (c + 1) * FFN_CHUNK)
        hcols = [jnp.maximum(_dot(x2b, wup_ref[:, cs]), 0.0) for x2b in x2bs]
        for n, hcol in enumerate(hcols):
            t = _dot((hcol * hcol).astype(BF16), wdown_ref[cs, :])
            ffns[n] = t if ffns[n] is None else ffns[n] + t
    for n in range(POST_SLABS):
        y_ref[slabs[n], :] = _layer_norm(ALPHA * x2s[n] + ffns[n], ln_ref[4:5, :], ln_ref[5:6, :])


def _post_prompt(x2d, attn, conv, mkv, ln, w_o, w_xq, w_xo, w_up, w_down, *, batch, seq, tm, mem_tokens):
    n_t = seq // tm
    row = lambda width: pl.BlockSpec((tm, width), lambda b, t: (b * n_t + t, 0))
    mem = pl.BlockSpec((mem_tokens, 2 * D_MODEL), lambda b, t: (b, 0))
    weights = (ln, w_o, w_xq, w_xo, w_up, w_down)
    return pl.pallas_call(
        _post_prompt_kernel,
        grid=(batch, n_t),
        in_specs=[row(D_MODEL), row(ATTN_WIDTH), row(CONV_CH), mem] + [_resident(w.shape) for w in weights],
        out_specs=row(D_MODEL),
        out_shape=jax.ShapeDtypeStruct((batch * seq, D_MODEL), F32),
        compiler_params=pltpu.CompilerParams(dimension_semantics=("arbitrary", "arbitrary"),
                                             vmem_limit_bytes=56 * MIB),
        name="post_prompt",
    )(x2d, attn, conv, mkv, *weights)


def _post1_sample_kernel(x_ref, ol_ref, c_ref, wuv_ref, ln_ref, wo_ref, wxq_ref, x1_ref, q2_ref):
    mix = _dot(c_ref[...], wo_ref[ATTN_WIDTH:, :])
    for h in range(MLA_HEADS):
        a_h = _dot(ol_ref[:, h * KV_LORA:(h + 1) * KV_LORA].astype(BF16), wuv_ref[h])
        mix = mix + _dot(a_h.astype(BF16), wo_ref[h * MLA_HEAD_DIM:(h + 1) * MLA_HEAD_DIM, :])
    (x1,), (q2,) = _mix_ln1_q([x_ref[...]], [mix], ln_ref, wxq_ref)
    x1_ref[...] = x1
    q2_ref[...] = q2 * MEM_QSCALE


def _post1_sample(x2d, o_lat, conv, w_uv, ln, w_o, w_xq):
    args = (x2d, o_lat, conv, w_uv, ln, w_o, w_xq)
    n = x2d.shape[0]
    return pl.pallas_call(
        _post1_sample_kernel,
        grid=(1,),
        in_specs=[_resident(a.shape) for a in args],
        out_specs=[pl.BlockSpec((n, D_MODEL), lambda i: (0, 0))] * 2,
        out_shape=[jax.ShapeDtypeStruct((n, D_MODEL), F32)] * 2,
        compiler_params=pltpu.CompilerParams(dimension_semantics=("arbitrary",), vmem_limit_bytes=32 * MIB),
        name="post1_sample",
    )(*args)


def _cross_sample_kernel(q_ref, k_ref, v_ref, o_ref, *, bb):
    pairs = k_ref.shape[1] // 2
    for i in range(bb):
        q = q_ref[i]
        q2 = jnp.concatenate([q, q], axis=0)[None]
        k = k_ref[i].reshape(pairs, 2 * MEM_HEADS, MEM_HEAD_DIM)
        v = v_ref[i].reshape(pairs, 2 * MEM_HEADS, MEM_HEAD_DIM)
        s = jnp.sum(k * q2, axis=-1, keepdims=True)
        m = jnp.max(s, axis=0, keepdims=True)
        m = jnp.maximum(m[:, :MEM_HEADS], m[:, MEM_HEADS:])
        p = jnp.exp2(s - jnp.concatenate([m, m], axis=1))
        l = jnp.sum(p, axis=0)
        o = jnp.sum(p * v, axis=0)
        o_ref[i] = (o[:MEM_HEADS] + o[MEM_HEADS:]) / (l[:MEM_HEADS] + l[MEM_HEADS:])


def _cross_sample(q2, mem_k, mem_v, *, bb):
    nb, m_tok = mem_k.shape[:2]
    q_spec = pl.BlockSpec((bb, MEM_HEADS, MEM_HEAD_DIM), lambda i: (i, 0, 0))
    kv_spec = pl.BlockSpec((bb, m_tok, MEM_HEADS, MEM_HEAD_DIM), lambda i: (i, 0, 0, 0))
    return pl.pallas_call(
        functools.partial(_cross_sample_kernel, bb=bb),
        grid=(nb // bb,),
        in_specs=[q_spec, kv_spec, kv_spec],
        out_specs=q_spec,
        out_shape=jax.ShapeDtypeStruct((nb, MEM_HEADS, MEM_HEAD_DIM), F32),
        compiler_params=pltpu.CompilerParams(dimension_semantics=("arbitrary",), vmem_limit_bytes=40 * MIB),
        name="cross_sample",
    )(q2, mem_k, mem_v)


def _post2_sample_kernel(x1_ref, o_ref, ln_ref, wxo_ref, wup_ref, wdown_ref, y_ref):
    o_heads = [o_ref[:, h * MEM_HEAD_DIM:(h + 1) * MEM_HEAD_DIM] for h in range(MEM_HEADS)]
    (y_ref[...],) = _xo_ffn([x1_ref[...]], [o_heads], ln_ref, wxo_ref, wup_ref, wdown_ref)


def _post2_sample(x1, o, ln, w_xo, w_up, w_down):
    args = (x1, o, ln, w_xo, w_up, w_down)
    n = x1.shape[0]
    return pl.pallas_call(
        _post2_sample_kernel,
        grid=(1,),
        in_specs=[_resident(a.shape) for a in args],
        out_specs=pl.BlockSpec((n, D_MODEL), lambda i: (0, 0)),
        out_shape=jax.ShapeDtypeStruct((n, D_MODEL), F32),
        compiler_params=pltpu.CompilerParams(dimension_semantics=("arbitrary",), vmem_limit_bytes=40 * MIB),
        name="post2_sample",
    )(*args)


DECODE_CHUNK_PAGES = 16
DECODE_ISSUE_UNROLL = 8


def _decode_kernel(pt_ref, ql_ref, qr_ref, cn_ref, krn_ref, u_ref, cw_ref, cp_ref, cpool_ref, rpool_ref,
                   o_ref, conv_ref, cbuf, rbuf, sem, ext_ref, sh_ref, *, n_pages, page_size, conv_tiles_per_seq):
    b = pl.program_id(0)
    nb = pl.num_programs(0)
    conv_rows = u_ref.shape[0]

    @pl.when(b % conv_tiles_per_seq == 0)
    def _():
        ext_ref[0:CONV_HALO, :] = jnp.zeros((CONV_HALO, CONV_CH), F32)

    @pl.when(b % conv_tiles_per_seq != 0)
    def _():
        ext_ref[0:CONV_HALO, :] = ext_ref[conv_rows:conv_rows + CONV_HALO, :]

    ext_ref[CONV_HALO:CONV_HALO + conv_rows, :] = u_ref[...]
    conv_off = CONV_HALO - CONV_STATE
    for r in range(1, SUBLANES):
        sh_ref[r - 1] = ext_ref[r:r + sh_ref.shape[1], :]

    def conv_chunk(r0):
        parts = []
        for c0 in range(0, CONV_CH, LANES):
            ch = slice(c0, c0 + LANES)
            acc = jnp.zeros((CONV_ROWS, LANES), F32)
            for k in range(CONV_K):
                a, r = divmod(k + conv_off, SUBLANES)
                lo = r0 + SUBLANES * a
                src = ext_ref[lo:lo + CONV_ROWS, ch] if r == 0 else sh_ref[r - 1, lo:lo + CONV_ROWS, ch]
                acc = acc + cw_ref[k:k + 1, ch] * src
            parts.append(acc)
        y = jnp.concatenate(parts, axis=1)
        conv_ref[r0:r0 + CONV_ROWS, :] = _conv_tail(y, cp_ref[0:1, :], cp_ref[1:2, :], cp_ref[2:3, :]
                                                    ).astype(conv_ref.dtype)

    conv_chunks = list(range(0, conv_rows, CONV_ROWS))

    def start_all(bi, slot):
        def body(p, carry):
            pg = pt_ref[bi * n_pages + p]
            pltpu.make_async_copy(cpool_ref.at[0, pg], cbuf.at[slot, p], sem.at[slot, 0]).start()
            pltpu.make_async_copy(rpool_ref.at[pg], rbuf.at[slot, p], sem.at[slot, 1]).start()
            return carry
        lax.fori_loop(0, n_pages, body, 0, unroll=DECODE_ISSUE_UNROLL)

    def wait_all(slot):
        pltpu.make_async_copy(cpool_ref.at[0, pl.ds(0, n_pages)], cbuf.at[slot], sem.at[slot, 0]).wait()
        pltpu.make_async_copy(rpool_ref.at[pl.ds(0, n_pages)], rbuf.at[slot], sem.at[slot, 1]).wait()

    slot = b % 2

    @pl.when(b == 0)
    def _():
        start_all(0, 0)

    @pl.when(b + 1 < nb)
    def _():
        start_all(b + 1, 1 - slot)

    wait_all(slot)

    ql32 = ql_ref[0]
    qr32 = qr_ref[0]
    c_new = cn_ref[0]
    kr_new = krn_ref[0]
    ql, qr = ql32.astype(BF16), qr32.astype(BF16)

    def chunk(c):
        pages = range(c * DECODE_CHUNK_PAGES, (c + 1) * DECODE_CHUNK_PAGES)
        cb = cbuf[slot, pages.start:pages.stop].reshape(DECODE_CHUNK_PAGES * page_size, KV_LORA).astype(BF16)
        rb = jnp.concatenate([rbuf[slot, p] for p in pages], axis=1).astype(BF16)
        return cb, _dot_nt(ql, cb) + _dot(qr, rb)

    m = (jnp.sum(ql32 * c_new, axis=-1, keepdims=True)
         + jnp.sum(qr32 * kr_new, axis=-1, keepdims=True))
    l = jnp.ones_like(m)
    o = jnp.broadcast_to(c_new, (MLA_HEADS, KV_LORA))
    n_chunks = n_pages // DECODE_CHUNK_PAGES
    nxt = chunk(0)
    for c in range(n_chunks):
        cb, s = nxt
        if c + 1 < n_chunks:
            nxt = chunk(c + 1)
        m_new = jnp.maximum(m, jnp.max(s, axis=-1, keepdims=True))
        alpha = jnp.exp2(m - m_new)
        p = jnp.exp2(s - m_new)
        l = alpha * l + jnp.sum(p, axis=-1, keepdims=True)
        o = alpha * o + _dot(p.astype(BF16), cb)
        m = m_new
    o_ref[0] = o / l
    for r0 in conv_chunks:
        conv_chunk(r0)


def _decode_and_prompt_conv(page_table, q_lat, q_rope, c_new, kr_new, cpool, rpool, u2d, conv_w, conv_params, *, seq):
    nb, n_pages = page_table.shape
    page_size = cpool.shape[2]
    assert n_pages % DECODE_CHUNK_PAGES == 0 and n_pages % DECODE_ISSUE_UNROLL == 0
    conv_rows, rem = divmod(u2d.shape[0], nb)
    assert rem == 0 and seq % conv_rows == 0 and conv_rows % CONV_ROWS == 0 and conv_rows >= CONV_HALO
    per_seq = lambda *tail: pl.BlockSpec((1,) + tail, lambda b, pt: (b, 0, 0))
    conv_tile = pl.BlockSpec((conv_rows, CONV_CH), lambda b, pt: (b, 0))
    whole = lambda a: pl.BlockSpec(a.shape, lambda b, pt: (0, 0))
    grid_spec = pltpu.PrefetchScalarGridSpec(
        num_scalar_prefetch=1,
        grid=(nb,),
        in_specs=[per_seq(MLA_HEADS, KV_LORA), per_seq(MLA_HEADS, ROPE_DIM), per_seq(1, KV_LORA), per_seq(1, ROPE_DIM),
                  conv_tile, whole(conv_w), whole(conv_params),
                  pl.BlockSpec(memory_space=pl.ANY),
                  pl.BlockSpec(memory_space=pl.ANY)],
        out_specs=[per_seq(MLA_HEADS, KV_LORA), conv_tile],
        scratch_shapes=[pltpu.VMEM((2, n_pages, page_size, KV_LORA), F32),
                        pltpu.VMEM((2, n_pages, ROPE_DIM, page_size), F32),
                        pltpu.SemaphoreType.DMA((2, 2)),
                        pltpu.VMEM((conv_rows + CONV_HALO, CONV_CH), F32),
                        pltpu.VMEM((SUBLANES - 1, conv_rows + CONV_HALO - SUBLANES, CONV_CH), F32)],
    )
    return pl.pallas_call(
        functools.partial(_decode_kernel, n_pages=n_pages, page_size=page_size, conv_tiles_per_seq=seq // conv_rows),
        grid_spec=grid_spec,
        out_shape=[jax.ShapeDtypeStruct((nb, MLA_HEADS, KV_LORA), F32),
                   jax.ShapeDtypeStruct((u2d.shape[0], CONV_CH), BF16)],
        compiler_params=pltpu.CompilerParams(dimension_semantics=("arbitrary",), vmem_limit_bytes=48 * MIB),
        name="mla_sample_paged_and_prompt_conv",
    )(page_table.reshape(-1), q_lat, q_rope, c_new, kr_new, u2d, conv_w, conv_params, cpool, rpool)


def _rope_tables(pos):
    inv_freq = jnp.exp(-math.log(ROPE_BASE) * jnp.arange(HALF_ROPE, dtype=F32) / HALF_ROPE)
    ang = pos.astype(F32)[:, None] * inv_freq[None, :]
    cos, sin = jnp.cos(ang), jnp.sin(ang)
    n = pos.shape[0]
    zeros = jnp.zeros((n, LANES - HALF_ROPE), F32)
    cos_t = jnp.concatenate([cos, cos, jnp.ones((n, LANES - ROPE_DIM), F32)], axis=1)
    sin_a = jnp.concatenate([-sin, zeros], axis=1)
    sin_b = jnp.concatenate([jnp.zeros((n, HALF_ROPE), F32), sin, zeros[:, HALF_ROPE:]], axis=1)
    return jnp.concatenate([cos_t, sin_a, sin_b], axis=1)


def _pad_cols(w, left, right):
    return jnp.pad(w, [(0, 0)] * (w.ndim - 1) + [(left, right)])


def _layout_w_in(w_in):
    wq = w_in[:, :Q_COLS].reshape(D_MODEL, MLA_HEADS, MLA_HEAD_DIM + ROPE_DIM)
    wq = jnp.concatenate([wq[..., MLA_HEAD_DIM:], wq[..., :MLA_HEAD_DIM]], axis=-1)
    wq = _pad_cols(wq, 0, HEAD_BLOCK - (MLA_HEAD_DIM + ROPE_DIM)).reshape(D_MODEL, MLA_HEADS * HEAD_BLOCK)
    c0 = Q_COLS
    wc = w_in[:, c0:c0 + KV_LORA]
    wkr = _pad_cols(w_in[:, c0 + KV_LORA:c0 + KV_LORA + ROPE_DIM], 0, HEAD_BLOCK - ROPE_DIM)
    wg = w_in[:, c0 + KV_LORA + ROPE_DIM:]
    return jnp.concatenate([wq, wkr, wc, wg], axis=1).astype(BF16)


def kernel(x_prompt, x_sample, cache_ckv, cache_krope, state_conv, cache_mem_k, cache_mem_v, page_table, mem_prompt, w_in, kv_norm_g, w_uk, w_uv, conv_w, conv_b, conv_ln_g, conv_ln_b, w_o, ln1_g, ln1_b, w_xq, w_xk, w_xv, w_xo, ln2_g, ln2_b, w_up, w_down, ln3_g, ln3_b):
    assert w_in.shape[0] == DEPTH == 1
    B, S, _ = x_prompt.shape
    Bd, T, _ = x_sample.shape
    assert T == 1
    n_pool, page_size, _ = cache_ckv.shape[1:]
    n_pages = page_table.shape[1]
    past_len = n_pages * page_size
    mem_tokens = mem_prompt.shape[1]
    l = 0

    w_all = _layout_w_in(w_in[l])
    g_kv = kv_norm_g[l][None, :]
    wuk_blk = _pad_cols(w_uk[l], ROPE_DIM, HEAD_BLOCK - ROPE_DIM - MLA_HEAD_DIM
                        ).reshape(KV_LORA, MLA_HEADS * HEAD_BLOCK).astype(BF16)
    wuv_t = jnp.pad(jnp.transpose(w_uv[l], (1, 2, 0)), ((0, 0), (0, V_BLOCK - MLA_HEAD_DIM), (0, 0))
                    ).reshape(MLA_HEADS * V_BLOCK, KV_LORA).astype(BF16)
    wq2l = jnp.pad(jnp.transpose(w_uk[l], (1, 2, 0)),
                   ((0, 0), (ROPE_DIM, HEAD_BLOCK - ROPE_DIM - MLA_HEAD_DIM), (0, 0))).astype(BF16)
    wuv_h = jnp.transpose(w_uv[l], (1, 0, 2)).astype(BF16)
    conv_params = jnp.stack([conv_b[l], conv_ln_g[l], conv_ln_b[l]])
    ln = jnp.stack([ln1_g[l], ln1_b[l], ln2_g[l], ln2_b[l], ln3_g[l], ln3_b[l],
                    jnp.zeros_like(ln1_g[l]), jnp.zeros_like(ln1_g[l])])
    wo_b, wxq_b, wxo_b = w_o[l].astype(BF16), w_xq[l].astype(BF16), w_xo[l].astype(BF16)
    wup_b, wdown_b = w_up[l].astype(BF16), w_down[l].astype(BF16)
    w_xkv = jnp.concatenate([w_xk[l], w_xv[l]], axis=1).astype(BF16)

    xp = x_prompt.reshape(B * S, D_MODEL)
    tab_p = _rope_tables(jnp.arange(S, dtype=jnp.int32))
    attn_tile = 512
    q_p, ckv_p, kr_p, u_p, k_p, vt_p = _inproj(xp, w_all, tab_p, g_kv, (wuk_blk, wuv_t), tm=attn_tile, sample=False)
    attn_p = _flash(q_p, k_p, vt_p, batch=B, seq=S, t=attn_tile)
    mk_p, mv_p, mkv_p = _memkv(mem_prompt.reshape(B * mem_tokens, D_MODEL), w_xkv, tm=512)

    xs = x_sample.reshape(Bd, D_MODEL)
    tab_s = _rope_tables(jnp.full((Bd,), past_len, dtype=jnp.int32))
    q_s, ckv_s, kr_s, u_s, qlat_s = _inproj(xs, w_all, tab_s, g_kv, (wq2l,), tm=Bd, sample=True)
    q_rope = q_s.reshape(Bd, MLA_HEADS, HEAD_BLOCK)[:, :, :ROPE_DIM]
    rpool = jnp.swapaxes(cache_krope[l], 1, 2)
    o_lat, conv_p = _decode_and_prompt_conv(page_table, qlat_s.reshape(Bd, MLA_HEADS, KV_LORA), q_rope,
                                            ckv_s[:, None, :], kr_s[:, None, :], cache_ckv, rpool,
                                            u_p, conv_w[l], conv_params, seq=S)
    y_p = _post_prompt(xp, attn_p, conv_p, mkv_p, ln, wo_b, wxq_b, wxo_b, wup_b, wdown_b,
                       batch=B, seq=S, tm=512, mem_tokens=mem_tokens)
    conv_s, new_state = _conv_sample(state_conv[l], u_s[:, None, :], conv_w[l], conv_params, bb=16)
    x1_s, q2_s = _post1_sample(xs, o_lat.reshape(Bd, MLA_HEADS * KV_LORA), conv_s.reshape(Bd, CONV_CH),
                               wuv_h, ln, wo_b, wxq_b)
    o_s = _cross_sample(q2_s.reshape(Bd, MEM_HEADS, MEM_HEAD_DIM), cache_mem_k[l], cache_mem_v[l], bb=4)
    y_s = _post2_sample(x1_s, o_s.reshape(Bd, D_MODEL), ln, wxo_b, wup_b, wdown_b)

    mem_shape = (1, B, mem_tokens, MEM_HEADS, MEM_HEAD_DIM)
    return (y_p.reshape(B, S, D_MODEL),
            y_s.reshape(Bd, T, D_MODEL),
            ckv_p.reshape(1, B, S, KV_LORA),
            jnp.swapaxes(kr_p, 1, 2)[None],
            u_p.reshape(B, S, CONV_CH)[None, :, S - CONV_STATE:, :],
            mk_p.reshape(mem_shape),
            mv_p.reshape(mem_shape),
            ckv_s.reshape(1, Bd, T, KV_LORA),
            kr_s.reshape(1, Bd, T, ROPE_DIM),
            new_state[None])
```

```python
import functools
import math

import jax
import jax.numpy as jnp
from jax import lax
from jax.experimental import pallas as pl
from jax.experimental.pallas import tpu as pltpu

D_MODEL = 1024
MLA_HEADS = 8
MLA_HEAD_DIM = 64
ROPE_DIM = 32
KV_LORA = D_MODEL // 4
ATTN_WIDTH = MLA_HEADS * MLA_HEAD_DIM
CONV_CH = D_MODEL - ATTN_WIDTH
CONV_K = 31
CONV_STATE = CONV_K - 1
MEM_HEADS = 4
MEM_HEAD_DIM = D_MODEL // MEM_HEADS
D_FF = 4 * D_MODEL
ROPE_BASE = 10000.0
LN_EPS = 1e-5
DEPTH = 1
ALPHA = (2.0 * DEPTH) ** 0.25
Q_COLS = MLA_HEADS * (MLA_HEAD_DIM + ROPE_DIM)
LOG2E = math.log2(math.e)
MLA_QSCALE = (MLA_HEAD_DIM + ROPE_DIM) ** -0.5 * LOG2E
MEM_QSCALE = MEM_HEAD_DIM ** -0.5 * LOG2E

LANES = 128
SUBLANES = 8
BF16_SUBLANES = 16
HEAD_BLOCK = LANES
HALF_ROPE = ROPE_DIM // 2
NEG_BIG = -1e30
MIB = 1024 * 1024

F32 = jnp.float32
BF16 = jnp.bfloat16

QK_COLS = (MLA_HEADS + 1) * HEAD_BLOCK


def _dot(a, b):
    return jnp.dot(a, b, preferred_element_type=F32)


def _dot_nt(a, b):
    return lax.dot_general(a, b, (((1,), (1,)), ((), ())), preferred_element_type=F32)


def _layer_norm(x, g, b):
    mu = jnp.mean(x, axis=-1, keepdims=True)
    xc = x - mu
    var = jnp.mean(xc * xc, axis=-1, keepdims=True)
    return xc * lax.rsqrt(var + LN_EPS) * g + b


def _resident(shape):
    nd = len(shape)
    return pl.BlockSpec(shape, lambda *_: (0,) * nd, pipeline_mode=pl.Buffered(1))


def _rope_block(zb, cos_t, sin_a, sin_b):
    return zb * cos_t + pltpu.roll(zb, LANES - HALF_ROPE, 1) * sin_a + pltpu.roll(zb, HALF_ROPE, 1) * sin_b


def _inproj_common(x_ref, w_ref, tab_ref, g_ref):
    x = x_ref[...].astype(BF16)
    tab = tab_ref[...]
    cos_t, sin_a, sin_b = tab[:, :LANES], tab[:, LANES:2 * LANES], tab[:, 2 * LANES:]
    zq = _dot(x, w_ref[:, :QK_COLS])
    blocks = [_rope_block(zq[:, h * HEAD_BLOCK:(h + 1) * HEAD_BLOCK], cos_t, sin_a, sin_b)
              for h in range(MLA_HEADS + 1)]
    zc = _dot(x, w_ref[:, QK_COLS:QK_COLS + KV_LORA])
    c = zc * lax.rsqrt(jnp.mean(zc * zc, axis=-1, keepdims=True) + LN_EPS) * g_ref[...]
    zg = _dot(x, w_ref[:, QK_COLS + KV_LORA:])
    u = zg[:, :CONV_CH] * jax.nn.sigmoid(zg[:, CONV_CH:])
    return blocks, c, u


V_ONES_ROW = MLA_HEAD_DIM
V_BLOCK = -(-(MLA_HEAD_DIM + 1) // BF16_SUBLANES) * BF16_SUBLANES


def _inproj_prompt_kernel(x_ref, w_ref, tab_ref, g_ref, wuk_ref, wuvt_ref,
                          q_ref, ckv_ref, kr_ref, u_ref, k_ref, vt_ref):
    blocks, c, u = _inproj_common(x_ref, w_ref, tab_ref, g_ref)
    kr = blocks[MLA_HEADS]
    for h in range(MLA_HEADS):
        q_ref[:, h * HEAD_BLOCK:(h + 1) * HEAD_BLOCK] = (blocks[h] * MLA_QSCALE).astype(BF16)
    kr_ref[0] = kr.T[0:ROPE_DIM, :]
    ckv_ref[...] = c
    u_ref[...] = u
    cb = c.astype(BF16)
    kn = _dot(cb, wuk_ref[...])
    for h in range(MLA_HEADS):
        k_ref[:, h * HEAD_BLOCK:(h + 1) * HEAD_BLOCK] = (kn[:, h * HEAD_BLOCK:(h + 1) * HEAD_BLOCK] + kr).astype(BF16)
    vt = _dot_nt(wuvt_ref[...], cb)
    row = lax.broadcasted_iota(jnp.int32, vt.shape, 0)
    vt_ref[0] = jnp.where(row % V_BLOCK == V_ONES_ROW, 1.0, vt).astype(BF16)


def _inproj_sample_kernel(x_ref, w_ref, tab_ref, g_ref, wq2l_ref,
                          q_ref, ckv_ref, kr_ref, u_ref, qlat_ref):
    blocks, c, u = _inproj_common(x_ref, w_ref, tab_ref, g_ref)
    kr = blocks[MLA_HEADS]
    for h in range(MLA_HEADS):
        qb = blocks[h] * MLA_QSCALE
        q_ref[:, h * HEAD_BLOCK:(h + 1) * HEAD_BLOCK] = qb
        qlat_ref[:, h * KV_LORA:(h + 1) * KV_LORA] = _dot(qb.astype(BF16), wq2l_ref[h])
    kr_ref[...] = kr[:, :ROPE_DIM]
    ckv_ref[...] = c
    u_ref[...] = u


def _inproj(x2d, w_all, tab, g, extra_ws, *, tm, sample):
    n = x2d.shape[0]
    n_tiles = n // tm
    n_tab_blocks = tab.shape[0] // tm
    row = lambda width: pl.BlockSpec((tm, width), lambda i: (i, 0))
    in_specs = [row(D_MODEL), _resident(w_all.shape),
                pl.BlockSpec((tm, 3 * LANES), lambda i: (i % n_tab_blocks, 0)),
                _resident(g.shape)] + [_resident(w.shape) for w in extra_ws]
    widths = [(D_MODEL, F32 if sample else BF16), (KV_LORA, F32), (ROPE_DIM, F32), (CONV_CH, F32)]
    widths.append((MLA_HEADS * KV_LORA, F32) if sample else (MLA_HEADS * HEAD_BLOCK, BF16))
    out_specs = [row(w) for w, _ in widths]
    out_shape = [jax.ShapeDtypeStruct((n, w), dt) for w, dt in widths]
    if not sample:
        out_specs.append(pl.BlockSpec((1, MLA_HEADS * V_BLOCK, tm), lambda i: (i, 0, 0)))
        out_shape.append(jax.ShapeDtypeStruct((n_tiles, MLA_HEADS * V_BLOCK, tm), BF16))
        out_specs[2] = pl.BlockSpec((1, ROPE_DIM, tm), lambda i: (i // n_tab_blocks, 0, i % n_tab_blocks))
        out_shape[2] = jax.ShapeDtypeStruct((n_tiles // n_tab_blocks, ROPE_DIM, tab.shape[0]), F32)
    return pl.pallas_call(
        _inproj_sample_kernel if sample else _inproj_prompt_kernel,
        grid=(n_tiles,),
        in_specs=in_specs,
        out_specs=out_specs,
        out_shape=out_shape,
        compiler_params=pltpu.CompilerParams(dimension_semantics=("arbitrary",), vmem_limit_bytes=48 * MIB),
        name="inproj_sample" if sample else "inproj_prompt",
    )(x2d, w_all, tab, g, *extra_ws)


CONV_HALO = 32
CONV_ROWS = 64


def _conv_tail(y, cb, g, b):
    t = _layer_norm(y + cb, g, b)
    return t * jax.nn.sigmoid(t)


def _conv_sample_kernel(s_ref, u_ref, w_ref, p_ref, o_ref, ns_ref):
    st = s_ref[...]
    u = u_ref[...]
    w_hist = w_ref[0:CONV_STATE, :]
    y = jnp.sum(st * w_hist[None, :, :], axis=1, keepdims=True) + u * w_ref[CONV_STATE:CONV_K, :][None]
    o_ref[...] = _conv_tail(y, p_ref[0:1, :][None], p_ref[1:2, :][None], p_ref[2:3, :][None]).astype(o_ref.dtype)
    ns_ref[:, 0:CONV_STATE - 1, :] = s_ref[:, 1:CONV_STATE, :]
    ns_ref[:, CONV_STATE - 1:CONV_STATE, :] = u


def _conv_sample(state, u, conv_w, conv_params, *, bb):
    nb = state.shape[0]
    return pl.pallas_call(
        _conv_sample_kernel,
        grid=(nb // bb,),
        in_specs=[pl.BlockSpec((bb, CONV_STATE, CONV_CH), lambda i: (i, 0, 0)),
                  pl.BlockSpec((bb, 1, CONV_CH), lambda i: (i, 0, 0)),
                  _resident(conv_w.shape), _resident(conv_params.shape)],
        out_specs=[pl.BlockSpec((bb, 1, CONV_CH), lambda i: (i, 0, 0)),
                   pl.BlockSpec((bb, CONV_STATE, CONV_CH), lambda i: (i, 0, 0))],
        out_shape=[jax.ShapeDtypeStruct((nb, 1, CONV_CH), BF16),
                   jax.ShapeDtypeStruct(state.shape, F32)],
        compiler_params=pltpu.CompilerParams(dimension_semantics=("arbitrary",), vmem_limit_bytes=32 * MIB),
        name="conv_sample",
    )(state, u, conv_w, conv_params)


FLASH_HEAD_GROUP = 4


def _flash_kernel(q_ref, k_ref, vt_ref, o_ref, *state, t):
    i = pl.program_id(1)
    kpos = lax.broadcasted_iota(jnp.int32, (t, t), 0)
    qpos = lax.broadcasted_iota(jnp.int32, (t, t), 1)
    causal = kpos <= qpos

    m_refs, acc_refs = state[:MLA_HEADS], state[MLA_HEADS:]
    for h in range(MLA_HEADS):
        m_refs[h][...] = jnp.full((1, t), NEG_BIG, F32)
        acc_refs[h][...] = jnp.zeros((V_BLOCK, t), F32)

    def kv_tile(j, masked):
        rows = pl.ds(pl.multiple_of(j * t, t), t)
        for g in range(0, MLA_HEADS, FLASH_HEAD_GROUP):
            heads = range(g, g + FLASH_HEAD_GROUP)
            hs = {h: slice(h * HEAD_BLOCK, (h + 1) * HEAD_BLOCK) for h in heads}
            vs = {h: slice(h * V_BLOCK, (h + 1) * V_BLOCK) for h in heads}
            sts = {h: _dot_nt(k_ref[rows, hs[h]], q_ref[:, hs[h]]) for h in heads}
            pts, alphas = {}, {}
            for h in heads:
                st = jnp.where(causal, sts[h], NEG_BIG) if masked else sts[h]
                m_old = m_refs[h][...]
                m_new = jnp.maximum(m_old, jnp.max(st, axis=0, keepdims=True))
                m_refs[h][...] = m_new
                alphas[h] = jnp.exp2(m_old - m_new)
                pts[h] = jnp.exp2(st - m_new).astype(BF16)
            for h in heads:
                acc_refs[h][...] = alphas[h] * acc_refs[h][...] + _dot(vt_ref[j, vs[h], :], pts[h])

    def off_diag(j, carry):
        kv_tile(j, False)
        return carry

    lax.fori_loop(0, i, off_diag, 0)
    kv_tile(i, True)

    def finish(h):
        acc = acc_refs[h][...]
        return acc[0:MLA_HEAD_DIM, :] / acc[V_ONES_ROW:V_ONES_ROW + 1, :]

    for pair in range(MLA_HEADS // 2):
        o_t = jnp.concatenate([finish(2 * pair), finish(2 * pair + 1)], axis=0)
        o_ref[:, pair * LANES:(pair + 1) * LANES] = o_t.T.astype(o_ref.dtype)


def _flash(q, k, vt, *, batch, seq, t):
    n_t = seq // t
    return pl.pallas_call(
        functools.partial(_flash_kernel, t=t),
        grid=(batch, n_t),
        in_specs=[pl.BlockSpec((t, q.shape[1]), lambda b, i: (b * n_t + i, 0)),
                  pl.BlockSpec((seq, k.shape[1]), lambda b, i: (b, 0)),
                  pl.BlockSpec((n_t, vt.shape[1], t), lambda b, i: (b, 0, 0))],
        out_specs=pl.BlockSpec((t, ATTN_WIDTH), lambda b, i: (b * n_t + i, 0)),
        out_shape=jax.ShapeDtypeStruct((batch * seq, ATTN_WIDTH), BF16),
        scratch_shapes=[pltpu.VMEM((1, t), F32)] * MLA_HEADS + [pltpu.VMEM((V_BLOCK, t), F32)] * MLA_HEADS,
        compiler_params=pltpu.CompilerParams(dimension_semantics=("arbitrary", "arbitrary"),
                                             vmem_limit_bytes=48 * MIB),
        name="mla_prompt_flash",
    )(q, k, vt)


def _memkv_kernel(m_ref, w_ref, k_ref, v_ref, kvb_ref):
    kv = _dot(m_ref[...].astype(BF16), w_ref[...])
    kvb_ref[...] = kv.astype(BF16)
    for h in range(MEM_HEADS):
        k_ref[:, h, :] = kv[:, h * MEM_HEAD_DIM:(h + 1) * MEM_HEAD_DIM]
        v_ref[:, h, :] = kv[:, D_MODEL + h * MEM_HEAD_DIM:D_MODEL + (h + 1) * MEM_HEAD_DIM]


def _memkv(mem2d, w_xkv, *, tm):
    n = mem2d.shape[0]
    row = pl.BlockSpec((tm, D_MODEL), lambda i: (i, 0))
    cache = pl.BlockSpec((tm, MEM_HEADS, MEM_HEAD_DIM), lambda i: (i, 0, 0))
    return pl.pallas_call(
        _memkv_kernel,
        grid=(n // tm,),
        in_specs=[row, _resident(w_xkv.shape)],
        out_specs=[cache, cache, pl.BlockSpec((tm, 2 * D_MODEL), lambda i: (i, 0))],
        out_shape=[jax.ShapeDtypeStruct((n, MEM_HEADS, MEM_HEAD_DIM), F32)] * 2
        + [jax.ShapeDtypeStruct((n, 2 * D_MODEL), BF16)],
        compiler_params=pltpu.CompilerParams(dimension_semantics=("arbitrary",), vmem_limit_bytes=40 * MIB),
        name="mem_kv",
    )(mem2d, w_xkv)


FFN_CHUNK = 1024


def _mix_ln1_q(xs, mixes, ln_ref, wxq_ref):
    x1s = [_layer_norm(ALPHA * x + mix, ln_ref[0:1, :], ln_ref[1:2, :]) for x, mix in zip(xs, mixes)]
    return x1s, [_dot(x1.astype(BF16), wxq_ref[...]) for x1 in x1s]


def _xo_ffn(x1s, o_heads_per_slab, ln_ref, wxo_ref, wup_ref, wdown_ref):
    atts = []
    for o_heads in o_heads_per_slab:
        att = None
        for h, oh in enumerate(o_heads):
            t = _dot(oh.astype(BF16), wxo_ref[h * MEM_HEAD_DIM:(h + 1) * MEM_HEAD_DIM, :])
            att = t if att is None else att + t
        atts.append(att)
    x2s = [_layer_norm(ALPHA * x1 + att, ln_ref[2:3, :], ln_ref[3:4, :]) for x1, att in zip(x1s, atts)]
    x2bs = [x2.astype(BF16) for x2 in x2s]
    ffns = [None] * len(x2s)
    for c in range(D_FF // FFN_CHUNK):
        cs = slice(c * FFN_CHUNK, (c + 1) * FFN_CHUNK)
        hcols = [jnp.maximum(_dot(x2b, wup_ref[:, cs]), 0.0) for x2b in x2bs]
        for n, hcol in enumerate(hcols):
            t = _dot((hcol * hcol).astype(BF16), wdown_ref[cs, :])
            ffns[n] = t if ffns[n] is None else ffns[n] + t
    return [_layer_norm(ALPHA * x2 + ffn, ln_ref[4:5, :], ln_ref[5:6, :]) for x2, ffn in zip(x2s, ffns)]


POST_SLABS = 2


def _post_prompt_kernel(x_ref, a_ref, c_ref, mkv_ref, ln_ref,
                        wo_ref, wxq_ref, wxo_ref, wup_ref, wdown_ref, y_ref):
    rows_per_slab = x_ref.shape[0] // POST_SLABS
    slabs = [slice(n * rows_per_slab, (n + 1) * rows_per_slab) for n in range(POST_SLABS)]
    head_cols = [slice(h * MEM_HEAD_DIM, (h + 1) * MEM_HEAD_DIM) for h in range(MEM_HEADS)]

    def mix(n):
        r = slabs[n]
        return _dot(a_ref[r, :], wo_ref[0:ATTN_WIDTH, :]) + _dot(c_ref[r, :], wo_ref[ATTN_WIDTH:, :])

    def ln1(n, mix_n):
        return _layer_norm(ALPHA * x_ref[slabs[n], :] + mix_n, ln_ref[0:1, :], ln_ref[1:2, :])

    def xq(x1):
        return _dot(x1.astype(BF16), wxq_ref[...]) * MEM_QSCALE

    def scores(q2):
        return [_dot_nt(q2[:, hs].astype(BF16), mkv_ref[:, hs]) for hs in head_cols]

    def softmax(ss):
        ps = [jnp.exp2(s - jnp.max(s, axis=-1, keepdims=True)) for s in ss]
        return ps, [1.0 / jnp.sum(p, axis=-1, keepdims=True) for p in ps]

    def values(ps, inv_ls):
        return [_dot(p.astype(BF16), mkv_ref[:, D_MODEL + h * MEM_HEAD_DIM:D_MODEL + (h + 1) * MEM_HEAD_DIM]) * inv_l
                for h, (p, inv_l) in enumerate(zip(ps, inv_ls))]

    def xo(o_heads):
        att = None
        for h, oh in enumerate(o_heads):
            t = _dot(oh.astype(BF16), wxo_ref[head_cols[h], :])
            att = t if att is None else att + t
        return att

    def ln2(x1, att):
        return _layer_norm(ALPHA * x1 + att, ln_ref[2:3, :], ln_ref[3:4, :])

    mix_a, mix_b = mix(0), mix(1)
    x1_a = ln1(0, mix_a)
    q2_a = xq(x1_a)
    x1_b = ln1(1, mix_b)
    s_a = scores(q2_a)
    q2_b = xq(x1_b)
    p_a = softmax(s_a)
    o_a = values(*p_a)
    s_b = scores(q2_b)
    att_a = xo(o_a)
    p_b = softmax(s_b)
    o_b = values(*p_b)
    x2_a = ln2(x1_a, att_a)
    att_b = xo(o_b)
    x2_b = ln2(x1_b, att_b)

    x2s = [x2_a, x2_b]
    x2bs = [x2.astype(BF16) for x2 in x2s]
    ffns = [None] * POST_SLABS
    for c in range(D_FF // FFN_CHUNK):
        cs = slice(c * FFN_CHUNK, (c + 1) * FFN_CHUNK)
        hcols = [jnp.maximum(_dot(x2b, wup_ref[:, cs]), 0.0) for x2b in x2bs]
        for n, hcol in enumerate(hcols):
            t = _dot((hcol * hcol).astype(BF16), wdown_ref[cs, :])
            ffns[n] = t if ffns[n] is None else ffns[n] + t
    for n in range(POST_SLABS):
        y_ref[slabs[n], :] = _layer_norm(ALPHA * x2s[n] + ffns[n], ln_ref[4:5, :], ln_ref[5:6, :])


def _post_prompt(x2d, attn, conv, mkv, ln, w_o, w_xq, w_xo, w_up, w_down, *, batch, seq, tm, mem_tokens):
    n_t = seq // tm
    row = lambda width: pl.BlockSpec((tm, width), lambda b, t: (b * n_t + t, 0))
    mem = pl.BlockSpec((mem_tokens, 2 * D_MODEL), lambda b, t: (b, 0))
    weights = (ln, w_o, w_xq, w_xo, w_up, w_down)
    return pl.pallas_call(
        _post_prompt_kernel,
        grid=(batch, n_t),
        in_specs=[row(D_MODEL), row(ATTN_WIDTH), row(CONV_CH), mem] + [_resident(w.shape) for w in weights],
        out_specs=row(D_MODEL),
        out_shape=jax.ShapeDtypeStruct((batch * seq, D_MODEL), F32),
        compiler_params=pltpu.CompilerParams(dimension_semantics=("arbitrary", "arbitrary"),
                                             vmem_limit_bytes=56 * MIB),
        name="post_prompt",
    )(x2d, attn, conv, mkv, *weights)


def _post1_sample_kernel(x_ref, ol_ref, c_ref, wuv_ref, ln_ref, wo_ref, wxq_ref, x1_ref, q2_ref):
    mix = _dot(c_ref[...], wo_ref[ATTN_WIDTH:, :])
    for h in range(MLA_HEADS):
        a_h = _dot(ol_ref[:, h * KV_LORA:(h + 1) * KV_LORA].astype(BF16), wuv_ref[h])
        mix = mix + _dot(a_h.astype(BF16), wo_ref[h * MLA_HEAD_DIM:(h + 1) * MLA_HEAD_DIM, :])
    (x1,), (q2,) = _mix_ln1_q([x_ref[...]], [mix], ln_ref, wxq_ref)
    x1_ref[...] = x1
    q2_ref[...] = q2 * MEM_QSCALE


def _post1_sample(x2d, o_lat, conv, w_uv, ln, w_o, w_xq):
    args = (x2d, o_lat, conv, w_uv, ln, w_o, w_xq)
    n = x2d.shape[0]
    return pl.pallas_call(
        _post1_sample_kernel,
        grid=(1,),
        in_specs=[_resident(a.shape) for a in args],
        out_specs=[pl.BlockSpec((n, D_MODEL), lambda i: (0, 0))] * 2,
        out_shape=[jax.ShapeDtypeStruct((n, D_MODEL), F32)] * 2,
        compiler_params=pltpu.CompilerParams(dimension_semantics=("arbitrary",), vmem_limit_bytes=32 * MIB),
        name="post1_sample",
    )(*args)


def _cross_sample_kernel(q_ref, k_ref, v_ref, o_ref, *, bb):
    pairs = k_ref.shape[1] // 2
    for i in range(bb):
        q = q_ref[i]
        q2 = jnp.concatenate([q, q], axis=0)[None]
        k = k_ref[i].reshape(pairs, 2 * MEM_HEADS, MEM_HEAD_DIM)
        v = v_ref[i].reshape(pairs, 2 * MEM_HEADS, MEM_HEAD_DIM)
        s = jnp.sum(k * q2, axis=-1, keepdims=True)
        m = jnp.max(s, axis=0, keepdims=True)
        m = jnp.maximum(m[:, :MEM_HEADS], m[:, MEM_HEADS:])
        p = jnp.exp2(s - jnp.concatenate([m, m], axis=1))
        l = jnp.sum(p, axis=0)
        o = jnp.sum(p * v, axis=0)
        o_ref[i] = (o[:MEM_HEADS] + o[MEM_HEADS:]) / (l[:MEM_HEADS] + l[MEM_HEADS:])


def _cross_sample(q2, mem_k, mem_v, *, bb):
    nb, m_tok = mem_k.shape[:2]
    q_spec = pl.BlockSpec((bb, MEM_HEADS, MEM_HEAD_DIM), lambda i: (i, 0, 0))
    kv_spec = pl.BlockSpec((bb, m_tok, MEM_HEADS, MEM_HEAD_DIM), lambda i: (i, 0, 0, 0))
    return pl.pallas_call(
        functools.partial(_cross_sample_kernel, bb=bb),
        grid=(nb // bb,),
        in_specs=[q_spec, kv_spec, kv_spec],
        out_specs=q_spec,
        out_shape=jax.ShapeDtypeStruct((nb, MEM_HEADS, MEM_HEAD_DIM), F32),
        compiler_params=pltpu.CompilerParams(dimension_semantics=("arbitrary",), vmem_limit_bytes=40 * MIB),
        name="cross_sample",
    )(q2, mem_k, mem_v)


def _post2_sample_kernel(x1_ref, o_ref, ln_ref, wxo_ref, wup_ref, wdown_ref, y_ref):
    o_heads = [o_ref[:, h * MEM_HEAD_DIM:(h + 1) * MEM_HEAD_DIM] for h in range(MEM_HEADS)]
    (y_ref[...],) = _xo_ffn([x1_ref[...]], [o_heads], ln_ref, wxo_ref, wup_ref, wdown_ref)


def _post2_sample(x1, o, ln, w_xo, w_up, w_down):
    args = (x1, o, ln, w_xo, w_up, w_down)
    n = x1.shape[0]
    return pl.pallas_call(
        _post2_sample_kernel,
        grid=(1,),
        in_specs=[_resident(a.shape) for a in args],
        out_specs=pl.BlockSpec((n, D_MODEL), lambda i: (0, 0)),
        out_shape=jax.ShapeDtypeStruct((n, D_MODEL), F32),
        compiler_params=pltpu.CompilerParams(dimension_semantics=("arbitrary",), vmem_limit_bytes=40 * MIB),
        name="post2_sample",
    )(*args)


DECODE_CHUNK_PAGES = 16
DECODE_ISSUE_UNROLL = 8


def _decode_kernel(pt_ref, ql_ref, qr_ref, cn_ref, krn_ref, u_ref, cw_ref, cp_ref, cpool_ref, rpool_ref,
                   o_ref, conv_ref, cbuf, rbuf, sem, ext_ref, sh_ref, *, n_pages, page_size, conv_tiles_per_seq):
    b = pl.program_id(0)
    nb = pl.num_programs(0)
    conv_rows = u_ref.shape[0]

    @pl.when(b % conv_tiles_per_seq == 0)
    def _():
        ext_ref[0:CONV_HALO, :] = jnp.zeros((CONV_HALO, CONV_CH), F32)

    @pl.when(b % conv_tiles_per_seq != 0)
    def _():
        ext_ref[0:CONV_HALO, :] = ext_ref[conv_rows:conv_rows + CONV_HALO, :]

    ext_ref[CONV_HALO:CONV_HALO + conv_rows, :] = u_ref[...]
    conv_off = CONV_HALO - CONV_STATE
    for r in range(1, SUBLANES):
        sh_ref[r - 1] = ext_ref[r:r + sh_ref.shape[1], :]

    def conv_chunk(r0):
        parts = []
        for c0 in range(0, CONV_CH, LANES):
            ch = slice(c0, c0 + LANES)
            acc = jnp.zeros((CONV_ROWS, LANES), F32)
            for k in range(CONV_K):
                a, r = divmod(k + conv_off, SUBLANES)
                lo = r0 + SUBLANES * a
                src = ext_ref[lo:lo + CONV_ROWS, ch] if r == 0 else sh_ref[r - 1, lo:lo + CONV_ROWS, ch]
                acc = acc + cw_ref[k:k + 1, ch] * src
            parts.append(acc)
        y = jnp.concatenate(parts, axis=1)
        conv_ref[r0:r0 + CONV_ROWS, :] = _conv_tail(y, cp_ref[0:1, :], cp_ref[1:2, :], cp_ref[2:3, :]
                                                    ).astype(conv_ref.dtype)

    conv_chunks = list(range(0, conv_rows, CONV_ROWS))

    def start_all(bi, slot):
        def body(p, carry):
            pg = pt_ref[bi * n_pages + p]
            pltpu.make_async_copy(cpool_ref.at[0, pg], cbuf.at[slot, p], sem.at[slot, 0]).start()
            pltpu.make_async_copy(rpool_ref.at[pg], rbuf.at[slot, p], sem.at[slot, 1]).start()
            return carry
        lax.fori_loop(0, n_pages, body, 0, unroll=DECODE_ISSUE_UNROLL)

    def wait_all(slot):
        pltpu.make_async_copy(cpool_ref.at[0, pl.ds(0, n_pages)], cbuf.at[slot], sem.at[slot, 0]).wait()
        pltpu.make_async_copy(rpool_ref.at[pl.ds(0, n_pages)], rbuf.at[slot], sem.at[slot, 1]).wait()

    slot = b % 2

    @pl.when(b == 0)
    def _():
        start_all(0, 0)

    @pl.when(b + 1 < nb)
    def _():
        start_all(b + 1, 1 - slot)

    wait_all(slot)

    ql32 = ql_ref[0]
    qr32 = qr_ref[0]
    c_new = cn_ref[0]
    kr_new = krn_ref[0]
    ql, qr = ql32.astype(BF16), qr32.astype(BF16)

    def chunk(c):
        pages = range(c * DECODE_CHUNK_PAGES, (c + 1) * DECODE_CHUNK_PAGES)
        cb = cbuf[slot, pages.start:pages.stop].reshape(DECODE_CHUNK_PAGES * page_size, KV_LORA).astype(BF16)
        rb = jnp.concatenate([rbuf[slot, p] for p in pages], axis=1).astype(BF16)
        return cb, _dot_nt(ql, cb) + _dot(qr, rb)

    m = (jnp.sum(ql32 * c_new, axis=-1, keepdims=True)
         + jnp.sum(qr32 * kr_new, axis=-1, keepdims=True))
    l = jnp.ones_like(m)
    o = jnp.broadcast_to(c_new, (MLA_HEADS, KV_LORA))
    n_chunks = n_pages // DECODE_CHUNK_PAGES
    nxt = chunk(0)
    for c in range(n_chunks):
        cb, s = nxt
        if c + 1 < n_chunks:
            nxt = chunk(c + 1)
        m_new = jnp.maximum(m, jnp.max(s, axis=-1, keepdims=True))
        alpha = jnp.exp2(m - m_new)
        p = jnp.exp2(s - m_new)
        l = alpha * l + jnp.sum(p, axis=-1, keepdims=True)
        o = alpha * o + _dot(p.astype(BF16), cb)
        m = m_new
    o_ref[0] = o / l
    for r0 in conv_chunks:
        conv_chunk(r0)


def _decode_and_prompt_conv(page_table, q_lat, q_rope, c_new, kr_new, cpool, rpool, u2d, conv_w, conv_params, *, seq):
    nb, n_pages = page_table.shape
    page_size = cpool.shape[2]
    assert n_pages % DECODE_CHUNK_PAGES == 0 and n_pages % DECODE_ISSUE_UNROLL == 0
    conv_rows, rem = divmod(u2d.shape[0], nb)
    assert rem == 0 and seq % conv_rows == 0 and conv_rows % CONV_ROWS == 0 and conv_rows >= CONV_HALO
    per_seq = lambda *tail: pl.BlockSpec((1,) + tail, lambda b, pt: (b, 0, 0))
    conv_tile = pl.BlockSpec((conv_rows, CONV_CH), lambda b, pt: (b, 0))
    whole = lambda a: pl.BlockSpec(a.shape, lambda b, pt: (0, 0))
    grid_spec = pltpu.PrefetchScalarGridSpec(
        num_scalar_prefetch=1,
        grid=(nb,),
        in_specs=[per_seq(MLA_HEADS, KV_LORA), per_seq(MLA_HEADS, ROPE_DIM), per_seq(1, KV_LORA), per_seq(1, ROPE_DIM),
                  conv_tile, whole(conv_w), whole(conv_params),
                  pl.BlockSpec(memory_space=pl.ANY),
                  pl.BlockSpec(memory_space=pl.ANY)],
        out_specs=[per_seq(MLA_HEADS, KV_LORA), conv_tile],
        scratch_shapes=[pltpu.VMEM((2, n_pages, page_size, KV_LORA), F32),
                        pltpu.VMEM((2, n_pages, ROPE_DIM, page_size), F32),
                        pltpu.SemaphoreType.DMA((2, 2)),
                        pltpu.VMEM((conv_rows + CONV_HALO, CONV_CH), F32),
                        pltpu.VMEM((SUBLANES - 1, conv_rows + CONV_HALO - SUBLANES, CONV_CH), F32)],
    )
    return pl.pallas_call(
        functools.partial(_decode_kernel, n_pages=n_pages, page_size=page_size, conv_tiles_per_seq=seq // conv_rows),
        grid_spec=grid_spec,
        out_shape=[jax.ShapeDtypeStruct((nb, MLA_HEADS, KV_LORA), F32),
                   jax.ShapeDtypeStruct((u2d.shape[0], CONV_CH), BF16)],
        compiler_params=pltpu.CompilerParams(dimension_semantics=("arbitrary",), vmem_limit_bytes=48 * MIB),
        name="mla_sample_paged_and_prompt_conv",
    )(page_table.reshape(-1), q_lat, q_rope, c_new, kr_new, u2d, conv_w, conv_params, cpool, rpool)


def _rope_tables(pos):
    inv_freq = jnp.exp(-math.log(ROPE_BASE) * jnp.arange(HALF_ROPE, dtype=F32) / HALF_ROPE)
    ang = pos.astype(F32)[:, None] * inv_freq[None, :]
    cos, sin = jnp.cos(ang), jnp.sin(ang)
    n = pos.shape[0]
    zeros = jnp.zeros((n, LANES - HALF_ROPE), F32)
    cos_t = jnp.concatenate([cos, cos, jnp.ones((n, LANES - ROPE_DIM), F32)], axis=1)
    sin_a = jnp.concatenate([-sin, zeros], axis=1)
    sin_b = jnp.concatenate([jnp.zeros((n, HALF_ROPE), F32), sin, zeros[:, HALF_ROPE:]], axis=1)
    return jnp.concatenate([cos_t, sin_a, sin_b], axis=1)


def _pad_cols(w, left, right):
    return jnp.pad(w, [(0, 0)] * (w.ndim - 1) + [(left, right)])


def _layout_w_in(w_in):
    wq = w_in[:, :Q_COLS].reshape(D_MODEL, MLA_HEADS, MLA_HEAD_DIM + ROPE_DIM)
    wq = jnp.concatenate([wq[..., MLA_HEAD_DIM:], wq[..., :MLA_HEAD_DIM]], axis=-1)
    wq = _pad_cols(wq, 0, HEAD_BLOCK - (MLA_HEAD_DIM + ROPE_DIM)).reshape(D_MODEL, MLA_HEADS * HEAD_BLOCK)
    c0 = Q_COLS
    wc = w_in[:, c0:c0 + KV_LORA]
    wkr = _pad_cols(w_in[:, c0 + KV_LORA:c0 + KV_LORA + ROPE_DIM], 0, HEAD_BLOCK - ROPE_DIM)
    wg = w_in[:, c0 + KV_LORA + ROPE_DIM:]
    return jnp.concatenate([wq, wkr, wc, wg], axis=1).astype(BF16)


def kernel(x_prompt, x_sample, cache_ckv, cache_krope, state_conv, cache_mem_k, cache_mem_v, page_table, mem_prompt, w_in, kv_norm_g, w_uk, w_uv, conv_w, conv_b, conv_ln_g, conv_ln_b, w_o, ln1_g, ln1_b, w_xq, w_xk, w_xv, w_xo, ln2_g, ln2_b, w_up, w_down, ln3_g, ln3_b):
    assert w_in.shape[0] == DEPTH == 1
    B, S, _ = x_prompt.shape
    Bd, T, _ = x_sample.shape
    assert T == 1
    n_pool, page_size, _ = cache_ckv.shape[1:]
    n_pages = page_table.shape[1]
    past_len = n_pages * page_size
    mem_tokens = mem_prompt.shape[1]
    l = 0

    w_all = _layout_w_in(w_in[l])
    g_kv = kv_norm_g[l][None, :]
    wuk_blk = _pad_cols(w_uk[l], ROPE_DIM, HEAD_BLOCK - ROPE_DIM - MLA_HEAD_DIM
                        ).reshape(KV_LORA, MLA_HEADS * HEAD_BLOCK).astype(BF16)
    wuv_t = jnp.pad(jnp.transpose(w_uv[l], (1, 2, 0)), ((0, 0), (0, V_BLOCK - MLA_HEAD_DIM), (0, 0))
                    ).reshape(MLA_HEADS * V_BLOCK, KV_LORA).astype(BF16)
    wq2l = jnp.pad(jnp.transpose(w_uk[l], (1, 2, 0)),
                   ((0, 0), (ROPE_DIM, HEAD_BLOCK - ROPE_DIM - MLA_HEAD_DIM), (0, 0))).astype(BF16)
    wuv_h = jnp.transpose(w_uv[l], (1, 0, 2)).astype(BF16)
    conv_params = jnp.stack([conv_b[l], conv_ln_g[l], conv_ln_b[l]])
    ln = jnp.stack([ln1_g[l], ln1_b[l], ln2_g[l], ln2_b[l], ln3_g[l], ln3_b[l],
                    jnp.zeros_like(ln1_g[l]), jnp.zeros_like(ln1_g[l])])
    wo_b, wxq_b, wxo_b = w_o[l].astype(BF16), w_xq[l].astype(BF16), w_xo[l].astype(BF16)
    wup_b, wdown_b = w_up[l].astype(BF16), w_down[l].astype(BF16)
    w_xkv = jnp.concatenate([w_xk[l], w_xv[l]], axis=1).astype(BF16)

    xp = x_prompt.reshape(B * S, D_MODEL)
    tab_p = _rope_tables(jnp.arange(S, dtype=jnp.int32))
    attn_tile = 512
    q_p, ckv_p, kr_p, u_p, k_p, vt_p = _inproj(xp, w_all, tab_p, g_kv, (wuk_blk, wuv_t), tm=attn_tile, sample=False)
    attn_p = _flash(q_p, k_p, vt_p, batch=B, seq=S, t=attn_tile)
    mk_p, mv_p, mkv_p = _memkv(mem_prompt.reshape(B * mem_tokens, D_MODEL), w_xkv, tm=512)

    xs = x_sample.reshape(Bd, D_MODEL)
    tab_s = _rope_tables(jnp.full((Bd,), past_len, dtype=jnp.int32))
    q_s, ckv_s, kr_s, u_s, qlat_s = _inproj(xs, w_all, tab_s, g_kv, (wq2l,), tm=Bd, sample=True)
    q_rope = q_s.reshape(Bd, MLA_HEADS, HEAD_BLOCK)[:, :, :ROPE_DIM]
    rpool = jnp.swapaxes(cache_krope[l], 1, 2)
    o_lat, conv_p = _decode_and_prompt_conv(page_table, qlat_s.reshape(Bd, MLA_HEADS, KV_LORA), q_rope,
                                            ckv_s[:, None, :], kr_s[:, None, :], cache_ckv, rpool,
                                            u_p, conv_w[l], conv_params, seq=S)
    y_p = _post_prompt(xp, attn_p, conv_p, mkv_p, ln, wo_b, wxq_b, wxo_b, wup_b, wdown_b,
                       batch=B, seq=S, tm=512, mem_tokens=mem_tokens)
    conv_s, new_state = _conv_sample(state_conv[l], u_s[:, None, :], conv_w[l], conv_params, bb=16)
    x1_s, q2_s = _post1_sample(xs, o_lat.reshape(Bd, MLA_HEADS * KV_LORA), conv_s.reshape(Bd, CONV_CH),
                               wuv_h, ln, wo_b, wxq_b)
    o_s = _cross_sample(q2_s.reshape(Bd, MEM_HEADS, MEM_HEAD_DIM), cache_mem_k[l], cache_mem_v[l], bb=4)
    y_s = _post2_sample(x1_s, o_s.reshape(Bd, D_MODEL), ln, wxo_b, wup_b, wdown_b)

    mem_shape = (1, B, mem_tokens, MEM_HEADS, MEM_HEAD_DIM)
    return (y_p.reshape(B, S, D_MODEL),
            y_s.reshape(Bd, T, D_MODEL),
            ckv_p.reshape(1, B, S, KV_LORA),
            jnp.swapaxes(kr_p, 1, 2)[None],
            u_p.reshape(B, S, CONV_CH)[None, :, S - CONV_STATE:, :],
            mk_p.reshape(mem_shape),
            mv_p.reshape(mem_shape),
            ckv_s.reshape(1, Bd, T, KV_LORA),
            kr_s.reshape(1, Bd, T, ROPE_DIM),
            new_state[None])
```
